```python
import jax, jax.numpy as jnp
from jax import lax
import numpy as np

D_MODEL = 1024
BATCH = 8
SEQ = 16384
DEPTH = 4

D_FF = 2816
LRU_WIDTH = 1024
LRU_HEADS = 4
LRU_HEAD_DIM = LRU_WIDTH // LRU_HEADS
LRU_CONV = 4
LRU_PAD = (2, 1)
LRU_C = 8.0
SC_WIDTH = 512
SC_CONV = 3
SC_PAD = (1, 1)
SGU_WIDTH = 512
SGU_HEADS = 4
SGU_HEAD_DIM = SGU_WIDTH // SGU_HEADS
CHUNK = 128
N_BRANCH = 3
EPS = 1e-6

_PART = (LRU_WIDTH, LRU_WIDTH, SC_WIDTH, SC_WIDTH, SC_WIDTH,
         SGU_WIDTH, SGU_WIDTH, N_BRANCH * D_MODEL)
D_IN = sum(_PART)
SPLIT_POINTS = tuple(int(p) for p in np.cumsum(_PART)[:-1])

kernel_name = "hybrid_rglru_shortconv_sgu_encoder"


def rmsnorm(x, g):
    xf = x.astype(jnp.float32)
    y = xf * lax.rsqrt(jnp.mean(xf * xf, axis=-1, keepdims=True) + EPS)
    return y.astype(x.dtype) * g


def layernorm(x, g, b):
    xf = x.astype(jnp.float32)
    mu = jnp.mean(xf, axis=-1, keepdims=True)
    var = jnp.mean(jnp.square(xf - mu), axis=-1, keepdims=True)
    return ((xf - mu) * lax.rsqrt(var + EPS)).astype(x.dtype) * g + b


def swiglu(h, w_gate, w_up, w_down):
    return (jax.nn.silu(h @ w_gate) * (h @ w_up)) @ w_down


def depthwise_conv(x, w, pad):
    c = x.shape[-1]
    return lax.conv_general_dilated(
        x, w[:, None, :], window_strides=(1,), padding=[pad],
        dimension_numbers=("NWC", "WIO", "NWC"), feature_group_count=c)


def _lin_combine(left, right):
    a_l, b_l = left
    a_r, b_r = right
    return a_l * a_r, a_r * b_l + b_r


def rg_lru(x, w_a, b_a, w_x, b_x, lam, reverse):
    bsz, s, wdt = x.shape
    xh = x.reshape(bsz, s, LRU_HEADS, LRU_HEAD_DIM)
    r = jax.nn.sigmoid(jnp.einsum("bshd,hde->bshe", xh, w_a).reshape(bsz, s, wdt) + b_a)
    i = jax.nn.sigmoid(jnp.einsum("bshd,hde->bshe", xh, w_x).reshape(bsz, s, wdt) + b_x)
    log_a = (-LRU_C * jax.nn.softplus(-lam.astype(jnp.float32))) * r.astype(jnp.float32)
    a = jnp.exp(log_a)
    u = (i * x).astype(jnp.float32) * jnp.sqrt(-jnp.expm1(2.0 * log_a))
    _, h = lax.associative_scan(_lin_combine, (a, u), reverse=reverse, axis=1)
    return h.astype(x.dtype)


def spatial_gating(u, v, ln_g, ln_b, w_s, b_s):
    u = jax.nn.gelu(u)
    v = layernorm(jax.nn.gelu(v), ln_g, ln_b)
    bsz, s, _ = v.shape
    vc = v.reshape(bsz, s // CHUNK, CHUNK, SGU_HEADS, SGU_HEAD_DIM)
    mixed = jnp.einsum("gpq,bnqgc->bnpgc", w_s, vc) + b_s.T[:, :, None]
    return u * mixed.reshape(bsz, s, SGU_WIDTH)


def mixer_block(h, w_in, lru_conv_w, lru_conv_b, lru_wa, lru_ba, lru_wx, lru_bx,
                lru_lambda, lru_w_out, sc_conv_w, sc_w_out, sgu_ln_g, sgu_ln_b,
                sgu_w_s, sgu_b, sgu_w_out, w_o):
    z = h @ w_in
    lru_gate, lru_x, sc_b, sc_c, sc_x, sgu_u, sgu_v, merge = jnp.split(z, SPLIT_POINTS, axis=-1)
    xc = depthwise_conv(lru_x, lru_conv_w, LRU_PAD) + lru_conv_b
    h_fwd = rg_lru(xc, lru_wa[0], lru_ba[0], lru_wx[0], lru_bx[0], lru_lambda[0], reverse=False)
    h_bwd = rg_lru(xc, lru_wa[1], lru_ba[1], lru_wx[1], lru_bx[1], lru_lambda[1], reverse=True)
    y_a = ((h_fwd + h_bwd) * jax.nn.gelu(lru_gate)) @ lru_w_out
    y_b = (sc_b * depthwise_conv(sc_c * sc_x, sc_conv_w, SC_PAD)) @ sc_w_out
    y_c = spatial_gating(sgu_u, sgu_v, sgu_ln_g, sgu_ln_b, sgu_w_s, sgu_b) @ sgu_w_out
    g = jax.nn.sigmoid(merge).reshape(*merge.shape[:-1], N_BRANCH, D_MODEL)
    m = g[..., 0, :] * y_a + g[..., 1, :] * y_b + g[..., 2, :] * y_c
    return m @ w_o


def _fwd_setup_inputs(seed: int = 0) -> dict:
    key = jax.random.key(seed)
    ks = iter(jax.random.split(key, 32))
    L, D, F = DEPTH, D_MODEL, D_FF

    def w(shape, fan_in):
        return jax.random.normal(next(ks), shape, jnp.float32) * (fan_in ** -0.5)

    def gain(shape):
        return 1.0 + 0.02 * jax.random.normal(next(ks), shape, jnp.float32)

    def bias(shape):
        return 0.02 * jax.random.normal(next(ks), shape, jnp.float32)

    a0 = jax.random.uniform(next(ks), (L, 2, LRU_WIDTH), jnp.float32, 0.9, 0.999)
    s0 = a0 ** (1.0 / LRU_C)
    lru_lambda = jnp.log(s0) - jnp.log1p(-s0)
    return {
        "x": jax.random.normal(next(ks), (BATCH, SEQ, D), jnp.float32),
        "ffn1_pre_g": gain((L, D)),
        "ffn1_w_gate": w((L, D, F), D),
        "ffn1_w_up": w((L, D, F), D),
        "ffn1_w_down": w((L, F, D), F),
        "ffn1_post_g": gain((L, D)),
        "mix_pre_g": gain((L, D)),
        "w_in": w((L, D, D_IN), D),
        "lru_conv_w": w((L, LRU_CONV, LRU_WIDTH), LRU_CONV),
        "lru_conv_b": bias((L, LRU_WIDTH)),
        "lru_wa": w((L, 2, LRU_HEADS, LRU_HEAD_DIM, LRU_HEAD_DIM), LRU_HEAD_DIM),
        "lru_ba": bias((L, 2, LRU_WIDTH)),
        "lru_wx": w((L, 2, LRU_HEADS, LRU_HEAD_DIM, LRU_HEAD_DIM), LRU_HEAD_DIM),
        "lru_bx": bias((L, 2, LRU_WIDTH)),
        "lru_lambda": lru_lambda,
        "lru_w_out": w((L, LRU_WIDTH, D), LRU_WIDTH),
        "sc_conv_w": w((L, SC_CONV, SC_WIDTH), SC_CONV),
        "sc_w_out": w((L, SC_WIDTH, D), SC_WIDTH),
        "sgu_ln_g": gain((L, SGU_WIDTH)),
        "sgu_ln_b": bias((L, SGU_WIDTH)),
        "sgu_w_s": w((L, SGU_HEADS, CHUNK, CHUNK), CHUNK),
        "sgu_b": bias((L, SGU_HEADS, CHUNK)),
        "sgu_w_out": w((L, SGU_WIDTH, D), SGU_WIDTH),
        "w_o": w((L, D, D), D),
        "mix_post_g": gain((L, D)),
        "ffn2_pre_g": gain((L, D)),
        "ffn2_w_gate": w((L, D, F), D),
        "ffn2_w_up": w((L, D, F), D),
        "ffn2_w_down": w((L, F, D), F),
        "ffn2_post_g": gain((L, D)),
    }


def _fwd_reference(x, ffn1_pre_g, ffn1_w_gate, ffn1_w_up, ffn1_w_down, ffn1_post_g,
              mix_pre_g, w_in, lru_conv_w, lru_conv_b, lru_wa, lru_ba, lru_wx, lru_bx,
              lru_lambda, lru_w_out, sc_conv_w, sc_w_out, sgu_ln_g, sgu_ln_b,
              sgu_w_s, sgu_b, sgu_w_out, w_o, mix_post_g,
              ffn2_pre_g, ffn2_w_gate, ffn2_w_up, ffn2_w_down, ffn2_post_g):
    for l in range(DEPTH):
        f1 = swiglu(rmsnorm(x, ffn1_pre_g[l]), ffn1_w_gate[l], ffn1_w_up[l], ffn1_w_down[l])
        x = x + 0.5 * rmsnorm(f1, ffn1_post_g[l])
        mx = mixer_block(rmsnorm(x, mix_pre_g[l]), w_in[l], lru_conv_w[l], lru_conv_b[l],
                         lru_wa[l], lru_ba[l], lru_wx[l], lru_bx[l], lru_lambda[l],
                         lru_w_out[l], sc_conv_w[l], sc_w_out[l], sgu_ln_g[l], sgu_ln_b[l],
                         sgu_w_s[l], sgu_b[l], sgu_w_out[l], w_o[l])
        x = x + rmsnorm(mx, mix_post_g[l])
        f2 = swiglu(rmsnorm(x, ffn2_pre_g[l]), ffn2_w_gate[l], ffn2_w_up[l], ffn2_w_down[l])
        x = x + 0.5 * rmsnorm(f2, ffn2_post_g[l])
    return x


import jax as _jax
import jax.numpy as _jnp

TWIN_FORMAT = 'train_step'
FWD_PARAMS = ['x', 'ffn1_pre_g', 'ffn1_w_gate', 'ffn1_w_up', 'ffn1_w_down', 'ffn1_post_g', 'mix_pre_g', 'w_in', 'lru_conv_w', 'lru_conv_b', 'lru_wa', 'lru_ba', 'lru_wx', 'lru_bx', 'lru_lambda', 'lru_w_out', 'sc_conv_w', 'sc_w_out', 'sgu_ln_g', 'sgu_ln_b', 'sgu_w_s', 'sgu_b', 'sgu_w_out', 'w_o', 'mix_post_g', 'ffn2_pre_g', 'ffn2_w_gate', 'ffn2_w_up', 'ffn2_w_down', 'ffn2_post_g']
TWIN_WEIGHTS = ['ffn1_pre_g', 'ffn1_w_gate', 'ffn1_w_up', 'ffn1_w_down', 'ffn1_post_g', 'mix_pre_g', 'w_in', 'lru_conv_w', 'lru_conv_b', 'lru_wa', 'lru_ba', 'lru_wx', 'lru_bx', 'lru_lambda', 'lru_w_out', 'sc_conv_w', 'sc_w_out', 'sgu_ln_g', 'sgu_ln_b', 'sgu_w_s', 'sgu_b', 'sgu_w_out', 'w_o', 'mix_post_g', 'ffn2_pre_g', 'ffn2_w_gate', 'ffn2_w_up', 'ffn2_w_down', 'ffn2_post_g']
TWIN_DIFF_INPUT = 'x'
TWIN_INPUTS = ['x', 'ffn1_pre_g', 'ffn1_w_gate', 'ffn1_w_up', 'ffn1_w_down', 'ffn1_post_g', 'mix_pre_g', 'w_in', 'lru_conv_w', 'lru_conv_b', 'lru_wa', 'lru_ba', 'lru_wx', 'lru_bx', 'lru_lambda', 'lru_w_out', 'sc_conv_w', 'sc_w_out', 'sgu_ln_g', 'sgu_ln_b', 'sgu_w_s', 'sgu_b', 'sgu_w_out', 'w_o', 'mix_post_g', 'ffn2_pre_g', 'ffn2_w_gate', 'ffn2_w_up', 'ffn2_w_down', 'ffn2_post_g', 'loss_target', 'm_ffn1_pre_g', 'm_ffn1_w_gate', 'm_ffn1_w_up', 'm_ffn1_w_down', 'm_ffn1_post_g', 'm_mix_pre_g', 'm_w_in', 'm_lru_conv_w', 'm_lru_conv_b', 'm_lru_wa', 'm_lru_ba', 'm_lru_wx', 'm_lru_bx', 'm_lru_lambda', 'm_lru_w_out', 'm_sc_conv_w', 'm_sc_w_out', 'm_sgu_ln_g', 'm_sgu_ln_b', 'm_sgu_w_s', 'm_sgu_b', 'm_sgu_w_out', 'm_w_o', 'm_mix_post_g', 'm_ffn2_pre_g', 'm_ffn2_w_gate', 'm_ffn2_w_up', 'm_ffn2_w_down', 'm_ffn2_post_g', 'v_ffn1_pre_g', 'v_ffn1_w_gate', 'v_ffn1_w_up', 'v_ffn1_w_down', 'v_ffn1_post_g', 'v_mix_pre_g', 'v_w_in', 'v_lru_conv_w', 'v_lru_conv_b', 'v_lru_wa', 'v_lru_ba', 'v_lru_wx', 'v_lru_bx', 'v_lru_lambda', 'v_lru_w_out', 'v_sc_conv_w', 'v_sc_w_out', 'v_sgu_ln_g', 'v_sgu_ln_b', 'v_sgu_w_s', 'v_sgu_b', 'v_sgu_w_out', 'v_w_o', 'v_mix_post_g', 'v_ffn2_pre_g', 'v_ffn2_w_gate', 'v_ffn2_w_up', 'v_ffn2_w_down', 'v_ffn2_post_g']
TWIN_OUTPUTS = ['loss', 'grad_x', 'grad_ffn1_pre_g', 'grad_ffn1_w_gate', 'grad_ffn1_w_up', 'grad_ffn1_w_down', 'grad_ffn1_post_g', 'grad_mix_pre_g', 'grad_w_in', 'grad_lru_conv_w', 'grad_lru_conv_b', 'grad_lru_wa', 'grad_lru_ba', 'grad_lru_wx', 'grad_lru_bx', 'grad_lru_lambda', 'grad_lru_w_out', 'grad_sc_conv_w', 'grad_sc_w_out', 'grad_sgu_ln_g', 'grad_sgu_ln_b', 'grad_sgu_w_s', 'grad_sgu_b', 'grad_sgu_w_out', 'grad_w_o', 'grad_mix_post_g', 'grad_ffn2_pre_g', 'grad_ffn2_w_gate', 'grad_ffn2_w_up', 'grad_ffn2_w_down', 'grad_ffn2_post_g', 'delta_ffn1_pre_g', 'delta_ffn1_w_gate', 'delta_ffn1_w_up', 'delta_ffn1_w_down', 'delta_ffn1_post_g', 'delta_mix_pre_g', 'delta_w_in', 'delta_lru_conv_w', 'delta_lru_conv_b', 'delta_lru_wa', 'delta_lru_ba', 'delta_lru_wx', 'delta_lru_bx', 'delta_lru_lambda', 'delta_lru_w_out', 'delta_sc_conv_w', 'delta_sc_w_out', 'delta_sgu_ln_g', 'delta_sgu_ln_b', 'delta_sgu_w_s', 'delta_sgu_b', 'delta_sgu_w_out', 'delta_w_o', 'delta_mix_post_g', 'delta_ffn2_pre_g', 'delta_ffn2_w_gate', 'delta_ffn2_w_up', 'delta_ffn2_w_down', 'delta_ffn2_post_g', 'new_m_ffn1_pre_g', 'new_m_ffn1_w_gate', 'new_m_ffn1_w_up', 'new_m_ffn1_w_down', 'new_m_ffn1_post_g', 'new_m_mix_pre_g', 'new_m_w_in', 'new_m_lru_conv_w', 'new_m_lru_conv_b', 'new_m_lru_wa', 'new_m_lru_ba', 'new_m_lru_wx', 'new_m_lru_bx', 'new_m_lru_lambda', 'new_m_lru_w_out', 'new_m_sc_conv_w', 'new_m_sc_w_out', 'new_m_sgu_ln_g', 'new_m_sgu_ln_b', 'new_m_sgu_w_s', 'new_m_sgu_b', 'new_m_sgu_w_out', 'new_m_w_o', 'new_m_mix_post_g', 'new_m_ffn2_pre_g', 'new_m_ffn2_w_gate', 'new_m_ffn2_w_up', 'new_m_ffn2_w_down', 'new_m_ffn2_post_g', 'new_v_ffn1_pre_g', 'new_v_ffn1_w_gate', 'new_v_ffn1_w_up', 'new_v_ffn1_w_down', 'new_v_ffn1_post_g', 'new_v_mix_pre_g', 'new_v_w_in', 'new_v_lru_conv_w', 'new_v_lru_conv_b', 'new_v_lru_wa', 'new_v_lru_ba', 'new_v_lru_wx', 'new_v_lru_bx', 'new_v_lru_lambda', 'new_v_lru_w_out', 'new_v_sc_conv_w', 'new_v_sc_w_out', 'new_v_sgu_ln_g', 'new_v_sgu_ln_b', 'new_v_sgu_w_s', 'new_v_sgu_b', 'new_v_sgu_w_out', 'new_v_w_o', 'new_v_mix_post_g', 'new_v_ffn2_pre_g', 'new_v_ffn2_w_gate', 'new_v_ffn2_w_up', 'new_v_ffn2_w_down', 'new_v_ffn2_post_g']
TWIN_LEAF_KINDS = {'loss': 'loss', 'grad_x': 'grad_x', 'grad_ffn1_pre_g': 'grad_w', 'grad_ffn1_w_gate': 'grad_w', 'grad_ffn1_w_up': 'grad_w', 'grad_ffn1_w_down': 'grad_w', 'grad_ffn1_post_g': 'grad_w', 'grad_mix_pre_g': 'grad_w', 'grad_w_in': 'grad_w', 'grad_lru_conv_w': 'grad_w', 'grad_lru_conv_b': 'grad_w', 'grad_lru_wa': 'grad_w', 'grad_lru_ba': 'grad_w', 'grad_lru_wx': 'grad_w', 'grad_lru_bx': 'grad_w', 'grad_lru_lambda': 'grad_w', 'grad_lru_w_out': 'grad_w', 'grad_sc_conv_w': 'grad_w', 'grad_sc_w_out': 'grad_w', 'grad_sgu_ln_g': 'grad_w', 'grad_sgu_ln_b': 'grad_w', 'grad_sgu_w_s': 'grad_w', 'grad_sgu_b': 'grad_w', 'grad_sgu_w_out': 'grad_w', 'grad_w_o': 'grad_w', 'grad_mix_post_g': 'grad_w', 'grad_ffn2_pre_g': 'grad_w', 'grad_ffn2_w_gate': 'grad_w', 'grad_ffn2_w_up': 'grad_w', 'grad_ffn2_w_down': 'grad_w', 'grad_ffn2_post_g': 'grad_w', 'delta_ffn1_pre_g': 'delta_w', 'delta_ffn1_w_gate': 'delta_w', 'delta_ffn1_w_up': 'delta_w', 'delta_ffn1_w_down': 'delta_w', 'delta_ffn1_post_g': 'delta_w', 'delta_mix_pre_g': 'delta_w', 'delta_w_in': 'delta_w', 'delta_lru_conv_w': 'delta_w', 'delta_lru_conv_b': 'delta_w', 'delta_lru_wa': 'delta_w', 'delta_lru_ba': 'delta_w', 'delta_lru_wx': 'delta_w', 'delta_lru_bx': 'delta_w', 'delta_lru_lambda': 'delta_w', 'delta_lru_w_out': 'delta_w', 'delta_sc_conv_w': 'delta_w', 'delta_sc_w_out': 'delta_w', 'delta_sgu_ln_g': 'delta_w', 'delta_sgu_ln_b': 'delta_w', 'delta_sgu_w_s': 'delta_w', 'delta_sgu_b': 'delta_w', 'delta_sgu_w_out': 'delta_w', 'delta_w_o': 'delta_w', 'delta_mix_post_g': 'delta_w', 'delta_ffn2_pre_g': 'delta_w', 'delta_ffn2_w_gate': 'delta_w', 'delta_ffn2_w_up': 'delta_w', 'delta_ffn2_w_down': 'delta_w', 'delta_ffn2_post_g': 'delta_w', 'new_m_ffn1_pre_g': 'new_m', 'new_m_ffn1_w_gate': 'new_m', 'new_m_ffn1_w_up': 'new_m', 'new_m_ffn1_w_down': 'new_m', 'new_m_ffn1_post_g': 'new_m', 'new_m_mix_pre_g': 'new_m', 'new_m_w_in': 'new_m', 'new_m_lru_conv_w': 'new_m', 'new_m_lru_conv_b': 'new_m', 'new_m_lru_wa': 'new_m', 'new_m_lru_ba': 'new_m', 'new_m_lru_wx': 'new_m', 'new_m_lru_bx': 'new_m', 'new_m_lru_lambda': 'new_m', 'new_m_lru_w_out': 'new_m', 'new_m_sc_conv_w': 'new_m', 'new_m_sc_w_out': 'new_m', 'new_m_sgu_ln_g': 'new_m', 'new_m_sgu_ln_b': 'new_m', 'new_m_sgu_w_s': 'new_m', 'new_m_sgu_b': 'new_m', 'new_m_sgu_w_out': 'new_m', 'new_m_w_o': 'new_m', 'new_m_mix_post_g': 'new_m', 'new_m_ffn2_pre_g': 'new_m', 'new_m_ffn2_w_gate': 'new_m', 'new_m_ffn2_w_up': 'new_m', 'new_m_ffn2_w_down': 'new_m', 'new_m_ffn2_post_g': 'new_m', 'new_v_ffn1_pre_g': 'new_v', 'new_v_ffn1_w_gate': 'new_v', 'new_v_ffn1_w_up': 'new_v', 'new_v_ffn1_w_down': 'new_v', 'new_v_ffn1_post_g': 'new_v', 'new_v_mix_pre_g': 'new_v', 'new_v_w_in': 'new_v', 'new_v_lru_conv_w': 'new_v', 'new_v_lru_conv_b': 'new_v', 'new_v_lru_wa': 'new_v', 'new_v_lru_ba': 'new_v', 'new_v_lru_wx': 'new_v', 'new_v_lru_bx': 'new_v', 'new_v_lru_lambda': 'new_v', 'new_v_lru_w_out': 'new_v', 'new_v_sc_conv_w': 'new_v', 'new_v_sc_w_out': 'new_v', 'new_v_sgu_ln_g': 'new_v', 'new_v_sgu_ln_b': 'new_v', 'new_v_sgu_w_s': 'new_v', 'new_v_sgu_b': 'new_v', 'new_v_sgu_w_out': 'new_v', 'new_v_w_o': 'new_v', 'new_v_mix_post_g': 'new_v', 'new_v_ffn2_pre_g': 'new_v', 'new_v_ffn2_w_gate': 'new_v', 'new_v_ffn2_w_up': 'new_v', 'new_v_ffn2_w_down': 'new_v', 'new_v_ffn2_post_g': 'new_v'}


def _forward(args):
    return _fwd_reference(*[args[k] for k in FWD_PARAMS])


def _output_shape():
    def fwd():
        inp = _fwd_setup_inputs(0)
        return _fwd_reference(*[inp[k] for k in FWD_PARAMS])
    out = _jax.eval_shape(fwd)
    return out.shape, out.dtype

N_MICROBATCH = 1
ADAM_LR = 0.001
ADAM_B1 = 0.9
ADAM_B2 = 0.999
ADAM_EPS = 1e-08
ADAM_WD = 0.01
ADAM_STEP = 10
PER_EXAMPLE_BATCH_AXIS = {'x': 0, 'loss_target': 0}
SHARED_INPUTS = []
_WEIGHT_DTYPES = {'ffn1_pre_g': _jnp.float32, 'ffn1_w_gate': _jnp.float32, 'ffn1_w_up': _jnp.float32, 'ffn1_w_down': _jnp.float32, 'ffn1_post_g': _jnp.float32, 'mix_pre_g': _jnp.float32, 'w_in': _jnp.float32, 'lru_conv_w': _jnp.float32, 'lru_conv_b': _jnp.float32, 'lru_wa': _jnp.float32, 'lru_ba': _jnp.float32, 'lru_wx': _jnp.float32, 'lru_bx': _jnp.float32, 'lru_lambda': _jnp.float32, 'lru_w_out': _jnp.float32, 'sc_conv_w': _jnp.float32, 'sc_w_out': _jnp.float32, 'sgu_ln_g': _jnp.float32, 'sgu_ln_b': _jnp.float32, 'sgu_w_s': _jnp.float32, 'sgu_b': _jnp.float32, 'sgu_w_out': _jnp.float32, 'w_o': _jnp.float32, 'mix_post_g': _jnp.float32, 'ffn2_pre_g': _jnp.float32, 'ffn2_w_gate': _jnp.float32, 'ffn2_w_up': _jnp.float32, 'ffn2_w_down': _jnp.float32, 'ffn2_post_g': _jnp.float32}
MOMENT_SCALE = {'ffn1_pre_g': 2.666173e+00, 'ffn1_w_gate': 1.092500e+00, 'ffn1_w_up': 1.120468e+00, 'ffn1_w_down': 1.854051e+00, 'ffn1_post_g': 2.885487e+01, 'mix_pre_g': 3.786674e+00, 'w_in': 1.448698e+00, 'lru_conv_w': 4.065386e+00, 'lru_conv_b': 9.116274e+01, 'lru_wa': 1.089144e+00, 'lru_ba': 7.359872e-01, 'lru_wx': 2.108579e+00, 'lru_bx': 8.595664e-01, 'lru_lambda': 1.227467e+00, 'lru_w_out': 5.322844e+00, 'sc_conv_w': 2.238295e+00, 'sc_w_out': 1.576057e+00, 'sgu_ln_g': 1.552182e+00, 'sgu_ln_b': 1.589806e+00, 'sgu_w_s': 1.484039e+00, 'sgu_b': 1.509384e+00, 'sgu_w_out': 1.047398e+00, 'w_o': 3.674425e+00, 'mix_post_g': 1.266770e+02, 'ffn2_pre_g': 1.569795e+00, 'ffn2_w_gate': 6.151872e-01, 'ffn2_w_up': 7.705860e-01, 'ffn2_w_down': 1.283175e+00, 'ffn2_post_g': 3.080683e+01}


def _to_microbatches(a, axis):
    t = _jnp.moveaxis(a, axis, 0)
    t = t.reshape((N_MICROBATCH, t.shape[0] // N_MICROBATCH) + t.shape[1:])
    return _jnp.moveaxis(t, 1, axis + 1)


def setup_inputs(seed: int = 0) -> dict:
    inp = _fwd_setup_inputs(seed)
    key = _jax.random.fold_in(_jax.random.key(seed), 7919)
    shape, _ = _output_shape()
    out = dict(inp)
    out["loss_target"] = _jax.random.normal(_jax.random.fold_in(key, 0), shape, _jnp.float32)
    for i, name in enumerate(TWIN_WEIGHTS):
        w = inp[name].astype(_jnp.float32)
        if MOMENT_SCALE is None:
            s = _jnp.sqrt(_jnp.mean(_jnp.square(w)) + 1e-30)
        else:
            s = MOMENT_SCALE[name]
        km, kv = _jax.random.split(_jax.random.fold_in(key, i + 1))
        out[name] = w
        out["m_" + name] = s * _jax.random.normal(km, w.shape, _jnp.float32)
        out["v_" + name] = (s * s) * _jax.random.uniform(kv, w.shape, _jnp.float32, 0.5, 1.5)
    if N_MICROBATCH > 1:
        for name, axis in PER_EXAMPLE_BATCH_AXIS.items():
            out[name] = _to_microbatches(out[name], axis)
    return {'x': out['x'], 'ffn1_pre_g': out['ffn1_pre_g'], 'ffn1_w_gate': out['ffn1_w_gate'], 'ffn1_w_up': out['ffn1_w_up'], 'ffn1_w_down': out['ffn1_w_down'], 'ffn1_post_g': out['ffn1_post_g'], 'mix_pre_g': out['mix_pre_g'], 'w_in': out['w_in'], 'lru_conv_w': out['lru_conv_w'], 'lru_conv_b': out['lru_conv_b'], 'lru_wa': out['lru_wa'], 'lru_ba': out['lru_ba'], 'lru_wx': out['lru_wx'], 'lru_bx': out['lru_bx'], 'lru_lambda': out['lru_lambda'], 'lru_w_out': out['lru_w_out'], 'sc_conv_w': out['sc_conv_w'], 'sc_w_out': out['sc_w_out'], 'sgu_ln_g': out['sgu_ln_g'], 'sgu_ln_b': out['sgu_ln_b'], 'sgu_w_s': out['sgu_w_s'], 'sgu_b': out['sgu_b'], 'sgu_w_out': out['sgu_w_out'], 'w_o': out['w_o'], 'mix_post_g': out['mix_post_g'], 'ffn2_pre_g': out['ffn2_pre_g'], 'ffn2_w_gate': out['ffn2_w_gate'], 'ffn2_w_up': out['ffn2_w_up'], 'ffn2_w_down': out['ffn2_w_down'], 'ffn2_post_g': out['ffn2_post_g'], 'loss_target': out['loss_target'], 'm_ffn1_pre_g': out['m_ffn1_pre_g'], 'm_ffn1_w_gate': out['m_ffn1_w_gate'], 'm_ffn1_w_up': out['m_ffn1_w_up'], 'm_ffn1_w_down': out['m_ffn1_w_down'], 'm_ffn1_post_g': out['m_ffn1_post_g'], 'm_mix_pre_g': out['m_mix_pre_g'], 'm_w_in': out['m_w_in'], 'm_lru_conv_w': out['m_lru_conv_w'], 'm_lru_conv_b': out['m_lru_conv_b'], 'm_lru_wa': out['m_lru_wa'], 'm_lru_ba': out['m_lru_ba'], 'm_lru_wx': out['m_lru_wx'], 'm_lru_bx': out['m_lru_bx'], 'm_lru_lambda': out['m_lru_lambda'], 'm_lru_w_out': out['m_lru_w_out'], 'm_sc_conv_w': out['m_sc_conv_w'], 'm_sc_w_out': out['m_sc_w_out'], 'm_sgu_ln_g': out['m_sgu_ln_g'], 'm_sgu_ln_b': out['m_sgu_ln_b'], 'm_sgu_w_s': out['m_sgu_w_s'], 'm_sgu_b': out['m_sgu_b'], 'm_sgu_w_out': out['m_sgu_w_out'], 'm_w_o': out['m_w_o'], 'm_mix_post_g': out['m_mix_post_g'], 'm_ffn2_pre_g': out['m_ffn2_pre_g'], 'm_ffn2_w_gate': out['m_ffn2_w_gate'], 'm_ffn2_w_up': out['m_ffn2_w_up'], 'm_ffn2_w_down': out['m_ffn2_w_down'], 'm_ffn2_post_g': out['m_ffn2_post_g'], 'v_ffn1_pre_g': out['v_ffn1_pre_g'], 'v_ffn1_w_gate': out['v_ffn1_w_gate'], 'v_ffn1_w_up': out['v_ffn1_w_up'], 'v_ffn1_w_down': out['v_ffn1_w_down'], 'v_ffn1_post_g': out['v_ffn1_post_g'], 'v_mix_pre_g': out['v_mix_pre_g'], 'v_w_in': out['v_w_in'], 'v_lru_conv_w': out['v_lru_conv_w'], 'v_lru_conv_b': out['v_lru_conv_b'], 'v_lru_wa': out['v_lru_wa'], 'v_lru_ba': out['v_lru_ba'], 'v_lru_wx': out['v_lru_wx'], 'v_lru_bx': out['v_lru_bx'], 'v_lru_lambda': out['v_lru_lambda'], 'v_lru_w_out': out['v_lru_w_out'], 'v_sc_conv_w': out['v_sc_conv_w'], 'v_sc_w_out': out['v_sc_w_out'], 'v_sgu_ln_g': out['v_sgu_ln_g'], 'v_sgu_ln_b': out['v_sgu_ln_b'], 'v_sgu_w_s': out['v_sgu_w_s'], 'v_sgu_b': out['v_sgu_b'], 'v_sgu_w_out': out['v_sgu_w_out'], 'v_w_o': out['v_w_o'], 'v_mix_post_g': out['v_mix_post_g'], 'v_ffn2_pre_g': out['v_ffn2_pre_g'], 'v_ffn2_w_gate': out['v_ffn2_w_gate'], 'v_ffn2_w_up': out['v_ffn2_w_up'], 'v_ffn2_w_down': out['v_ffn2_w_down'], 'v_ffn2_post_g': out['v_ffn2_post_g']}


def _loss(weights, diff, rest, loss_target):
    with _jax.named_scope("forward"):
        args = {**rest, TWIN_DIFF_INPUT: diff, **{k: w.astype(_WEIGHT_DTYPES[k]) for k, w in weights.items()}}
        y = _forward(args)
    with _jax.named_scope("loss_head"):
        err = _jnp.square(y.astype(_jnp.float32) - loss_target)
        return 0.5 * _jnp.sum(_jnp.mean(err, axis=-1)) if err.ndim else 0.5 * err


def _adamw(w, g, m, v):
    m = ADAM_B1 * m + (1.0 - ADAM_B1) * g
    v = ADAM_B2 * v + (1.0 - ADAM_B2) * _jnp.square(g)
    m_hat = m / (1.0 - ADAM_B1 ** ADAM_STEP)
    v_hat = v / (1.0 - ADAM_B2 ** ADAM_STEP)
    delta = -ADAM_LR * (m_hat / (_jnp.sqrt(v_hat) + ADAM_EPS) + ADAM_WD * w)
    return delta, m, v


def reference(x, ffn1_pre_g, ffn1_w_gate, ffn1_w_up, ffn1_w_down, ffn1_post_g, mix_pre_g, w_in, lru_conv_w, lru_conv_b, lru_wa, lru_ba, lru_wx, lru_bx, lru_lambda, lru_w_out, sc_conv_w, sc_w_out, sgu_ln_g, sgu_ln_b, sgu_w_s, sgu_b, sgu_w_out, w_o, mix_post_g, ffn2_pre_g, ffn2_w_gate, ffn2_w_up, ffn2_w_down, ffn2_post_g, loss_target, m_ffn1_pre_g, m_ffn1_w_gate, m_ffn1_w_up, m_ffn1_w_down, m_ffn1_post_g, m_mix_pre_g, m_w_in, m_lru_conv_w, m_lru_conv_b, m_lru_wa, m_lru_ba, m_lru_wx, m_lru_bx, m_lru_lambda, m_lru_w_out, m_sc_conv_w, m_sc_w_out, m_sgu_ln_g, m_sgu_ln_b, m_sgu_w_s, m_sgu_b, m_sgu_w_out, m_w_o, m_mix_post_g, m_ffn2_pre_g, m_ffn2_w_gate, m_ffn2_w_up, m_ffn2_w_down, m_ffn2_post_g, v_ffn1_pre_g, v_ffn1_w_gate, v_ffn1_w_up, v_ffn1_w_down, v_ffn1_post_g, v_mix_pre_g, v_w_in, v_lru_conv_w, v_lru_conv_b, v_lru_wa, v_lru_ba, v_lru_wx, v_lru_bx, v_lru_lambda, v_lru_w_out, v_sc_conv_w, v_sc_w_out, v_sgu_ln_g, v_sgu_ln_b, v_sgu_w_s, v_sgu_b, v_sgu_w_out, v_w_o, v_mix_post_g, v_ffn2_pre_g, v_ffn2_w_gate, v_ffn2_w_up, v_ffn2_w_down, v_ffn2_post_g):
    given = dict(x=x, ffn1_pre_g=ffn1_pre_g, ffn1_w_gate=ffn1_w_gate, ffn1_w_up=ffn1_w_up, ffn1_w_down=ffn1_w_down, ffn1_post_g=ffn1_post_g, mix_pre_g=mix_pre_g, w_in=w_in, lru_conv_w=lru_conv_w, lru_conv_b=lru_conv_b, lru_wa=lru_wa, lru_ba=lru_ba, lru_wx=lru_wx, lru_bx=lru_bx, lru_lambda=lru_lambda, lru_w_out=lru_w_out, sc_conv_w=sc_conv_w, sc_w_out=sc_w_out, sgu_ln_g=sgu_ln_g, sgu_ln_b=sgu_ln_b, sgu_w_s=sgu_w_s, sgu_b=sgu_b, sgu_w_out=sgu_w_out, w_o=w_o, mix_post_g=mix_post_g, ffn2_pre_g=ffn2_pre_g, ffn2_w_gate=ffn2_w_gate, ffn2_w_up=ffn2_w_up, ffn2_w_down=ffn2_w_down, ffn2_post_g=ffn2_post_g, loss_target=loss_target, m_ffn1_pre_g=m_ffn1_pre_g, m_ffn1_w_gate=m_ffn1_w_gate, m_ffn1_w_up=m_ffn1_w_up, m_ffn1_w_down=m_ffn1_w_down, m_ffn1_post_g=m_ffn1_post_g, m_mix_pre_g=m_mix_pre_g, m_w_in=m_w_in, m_lru_conv_w=m_lru_conv_w, m_lru_conv_b=m_lru_conv_b, m_lru_wa=m_lru_wa, m_lru_ba=m_lru_ba, m_lru_wx=m_lru_wx, m_lru_bx=m_lru_bx, m_lru_lambda=m_lru_lambda, m_lru_w_out=m_lru_w_out, m_sc_conv_w=m_sc_conv_w, m_sc_w_out=m_sc_w_out, m_sgu_ln_g=m_sgu_ln_g, m_sgu_ln_b=m_sgu_ln_b, m_sgu_w_s=m_sgu_w_s, m_sgu_b=m_sgu_b, m_sgu_w_out=m_sgu_w_out, m_w_o=m_w_o, m_mix_post_g=m_mix_post_g, m_ffn2_pre_g=m_ffn2_pre_g, m_ffn2_w_gate=m_ffn2_w_gate, m_ffn2_w_up=m_ffn2_w_up, m_ffn2_w_down=m_ffn2_w_down, m_ffn2_post_g=m_ffn2_post_g, v_ffn1_pre_g=v_ffn1_pre_g, v_ffn1_w_gate=v_ffn1_w_gate, v_ffn1_w_up=v_ffn1_w_up, v_ffn1_w_down=v_ffn1_w_down, v_ffn1_post_g=v_ffn1_post_g, v_mix_pre_g=v_mix_pre_g, v_w_in=v_w_in, v_lru_conv_w=v_lru_conv_w, v_lru_conv_b=v_lru_conv_b, v_lru_wa=v_lru_wa, v_lru_ba=v_lru_ba, v_lru_wx=v_lru_wx, v_lru_bx=v_lru_bx, v_lru_lambda=v_lru_lambda, v_lru_w_out=v_lru_w_out, v_sc_conv_w=v_sc_conv_w, v_sc_w_out=v_sc_w_out, v_sgu_ln_g=v_sgu_ln_g, v_sgu_ln_b=v_sgu_ln_b, v_sgu_w_s=v_sgu_w_s, v_sgu_b=v_sgu_b, v_sgu_w_out=v_sgu_w_out, v_w_o=v_w_o, v_mix_post_g=v_mix_post_g, v_ffn2_pre_g=v_ffn2_pre_g, v_ffn2_w_gate=v_ffn2_w_gate, v_ffn2_w_up=v_ffn2_w_up, v_ffn2_w_down=v_ffn2_w_down, v_ffn2_post_g=v_ffn2_post_g)
    weights = {n: given[n] for n in TWIN_WEIGHTS}
    shared = {n: given[n] for n in SHARED_INPUTS}
    per_example = {n: given[n] for n in ['x']}
    grad_fn = _jax.value_and_grad(_loss, argnums=(0, 1))

    def one_microbatch(ex, loss_target):
        ex = dict(ex)
        diff = ex.pop(TWIN_DIFF_INPUT)
        return grad_fn(weights, diff, {**shared, **ex}, loss_target)

    if N_MICROBATCH == 1:
        loss, (grad_w, grad_x) = one_microbatch(per_example, given["loss_target"])
    else:
        def body(carry, xs):
            loss_sum, grad_sum = carry
            l_k, (gw_k, gx_k) = one_microbatch(xs[0], xs[1])
            with _jax.named_scope("update"):
                return (loss_sum + l_k, _jax.tree.map(_jnp.add, grad_sum, gw_k)), gx_k

        init = (_jnp.zeros((), _jnp.float32), _jax.tree.map(_jnp.zeros_like, weights))
        (loss, grad_w), grad_x = _jax.lax.scan(body, init, (per_example, given["loss_target"]))
    with _jax.named_scope("update"):
        delta_w, new_m, new_v = {}, {}, {}
        for n in TWIN_WEIGHTS:
            delta_w[n], new_m[n], new_v[n] = _adamw(weights[n], grad_w[n], given["m_" + n], given["v_" + n])
    return (loss, grad_x, *[grad_w[n] for n in TWIN_WEIGHTS], *[delta_w[n] for n in TWIN_WEIGHTS],
            *[new_m[n] for n in TWIN_WEIGHTS], *[new_v[n] for n in TWIN_WEIGHTS])
```

```python
import functools

import jax
import jax.numpy as jnp
import numpy as np
from jax import lax
from jax.experimental import pallas as pl
from jax.experimental.pallas import tpu as pltpu

F32 = jnp.float32
BF16 = jnp.bfloat16
MESH = pl.DeviceIdType.MESH

D_MODEL = 1024
D_FF = 2816
N_SHARD = 4
F_SH = D_FF // N_SHARD
D_IN = 7680
D_IN_SH = D_IN // N_SHARD
ZA_W = 4608
ZM_W = 3072
LRU_W = 1024
LRU_HD = 256
SC_W = 512
SGU_W = 512
CHUNK = 128
DEPTH = 4
EPS = 1e-6
LRU_C = 8.0

ADAM_LR, ADAM_B1, ADAM_B2, ADAM_EPS, ADAM_WD, ADAM_STEP = 0.001, 0.9, 0.999, 1e-08, 0.01, 10

VMEM_LIMIT = 56 * 2 ** 20
PACK_C = 1024
AG_ROWS = 7168
RS_ROWS = 7296
RS_TILE = 192
SMALL_ROWS = 32


def _cp(sem=("arbitrary",)):
    return pltpu.CompilerParams(dimension_semantics=sem, vmem_limit_bytes=VMEM_LIMIT)


def _full(shape):
    return pl.BlockSpec(shape, lambda *_: (0,) * len(shape))


def _res(shape):
    return pl.BlockSpec(shape, lambda *_: (0,) * len(shape), pipeline_mode=pl.Buffered(1))


def _dot(a, b):
    return jnp.dot(a, b, preferred_element_type=F32)


def _dot_nt(a, b):
    return lax.dot_general(a, b, (((1,), (1,)), ((), ())), preferred_element_type=F32)


def _dot_tn(a, b):
    return lax.dot_general(a, b, (((0,), (0,)), ((), ())), preferred_element_type=F32)


def _sigmoid(x):
    return 1.0 / (1.0 + jnp.exp(-x))


_GELU_K = 0.7978845608028654
_GELU_C = 0.044715


def _gelu(x):
    return 0.5 * x * (1.0 + jnp.tanh(_GELU_K * (x + _GELU_C * x * x * x)))


def _gelu_and_grad(x):
    t = jnp.tanh(_GELU_K * (x + _GELU_C * x * x * x))
    g = 0.5 * x * (1.0 + t)
    dg = 0.5 * (1.0 + t) + 0.5 * x * (1.0 - t * t) * (_GELU_K * (1.0 + 3.0 * _GELU_C * x * x))
    return g, dg


def _rms_fwd(x, g):
    rs = lax.rsqrt(jnp.mean(x * x, axis=-1, keepdims=True) + EPS)
    return x * rs * g


def _rms_bwd(dy, x, g):
    rs = lax.rsqrt(jnp.mean(x * x, axis=-1, keepdims=True) + EPS)
    n = x * rs
    dn = dy * g
    dx = rs * (dn - n * jnp.mean(dn * n, axis=-1, keepdims=True))
    return dx, jnp.sum(dy * n, axis=0, keepdims=True)


def _acc(ref, val, first):
    @pl.when(first)
    def _():
        ref[...] = val

    @pl.when(jnp.logical_not(first))
    def _():
        ref[...] += val


def _shift(xm, d, prev, nxt, rows):
    tm = xm.shape[0]
    if d == 0:
        return xm
    y = pltpu.roll(xm, (-d) % tm, 0)
    if d < 0:
        hb = prev.shape[0]
        for r in range(-d):
            y = jnp.where(rows == r, prev[hb + r + d:hb + r + d + 1, :], y)
    else:
        for r in range(d):
            y = jnp.where(rows == tm - d + r, nxt[r:r + 1, :], y)
    return y


def _halo(tm, hb, w, col, nt, rev=False):
    r = tm // hb
    last = nt * r - 1
    ti = (lambda i: nt - 1 - i) if rev else (lambda i: i)
    return [pl.BlockSpec((tm, w), lambda i: (ti(i), col)),
            pl.BlockSpec((hb, w), lambda i: (jnp.maximum(ti(i) * r - 1, 0), col)),
            pl.BlockSpec((hb, w), lambda i: (jnp.minimum((ti(i) + 1) * r, last), col))]


def _edges(prev_ref, next_ref, ti, nt):
    prev = jnp.where(ti > 0, prev_ref[...].astype(F32), 0.0)
    nxt = jnp.where(ti < nt - 1, next_ref[...].astype(F32), 0.0)
    return prev, nxt


def _ffn_fwd(x, pre_g, wg, wu, wd, post_g, tm):
    s = x.shape[0]
    nt = s // tm

    def body(x_ref, pg_ref, wg_ref, wu_ref, wd_ref, qg_ref, o_ref, g_ref, u_ref, f_ref):
        xv = x_ref[...]
        hb = _rms_fwd(xv, pg_ref[...]).astype(BF16)
        f = jnp.zeros((tm, D_MODEL), F32)
        for k in range(N_SHARD):
            g = _dot(hb, wg_ref[k])
            u = _dot(hb, wu_ref[k])
            g_ref[k] = g.astype(BF16)
            u_ref[k] = u.astype(BF16)
            a = (g * _sigmoid(g)) * u
            f = f + _dot(a.astype(BF16), wd_ref[k])
        f_ref[...] = f.astype(BF16)
        o_ref[...] = xv + 0.5 * _rms_fwd(f, qg_ref[...])

    row = pl.BlockSpec((tm, D_MODEL), lambda i: (i, 0))
    gu = pl.BlockSpec((N_SHARD, tm, F_SH), lambda i: (0, i, 0))
    return pl.pallas_call(
        body, name="ffn_fwd", grid=(nt,),
        in_specs=[row, _full((1, D_MODEL)), _res((N_SHARD, D_MODEL, F_SH)), _res((N_SHARD, D_MODEL, F_SH)),
                  _res((N_SHARD, F_SH, D_MODEL)), _full((1, D_MODEL))],
        out_specs=[row, gu, gu, row],
        out_shape=[jax.ShapeDtypeStruct((s, D_MODEL), F32), jax.ShapeDtypeStruct((N_SHARD, s, F_SH), BF16),
                   jax.ShapeDtypeStruct((N_SHARD, s, F_SH), BF16), jax.ShapeDtypeStruct((s, D_MODEL), BF16)],
        compiler_params=_cp(("parallel",)),
    )(x, pre_g, wg, wu, wd, post_g)


def _ffn_bwd(dy, x, g, u, f, pre_g, post_g, wg, wu, wd, tm):
    s = x.shape[0]
    nt = s // tm

    def body(dy_ref, x_ref, g_ref, u_ref, f_ref, pg_ref, qg_ref, wg_ref, wu_ref, wd_ref,
             dx_ref, h_ref, a_ref, dg_ref, du_ref, df_ref, dpg_ref, dqg_ref):
        first = pl.program_id(0) == 0
        dyv = dy_ref[...]
        xv = x_ref[...]
        df, dq = _rms_bwd(0.5 * dyv, f_ref[...].astype(F32), qg_ref[...])
        dfb = df.astype(BF16)
        df_ref[...] = dfb
        h_ref[...] = _rms_fwd(xv, pg_ref[...]).astype(BF16)
        dh = jnp.zeros((tm, D_MODEL), F32)
        for k in range(N_SHARD):
            gv = g_ref[k].astype(F32)
            uv = u_ref[k].astype(F32)
            sg = _sigmoid(gv)
            silu = gv * sg
            a_ref[k] = (silu * uv).astype(BF16)
            da = _dot_nt(dfb, wd_ref[k])
            dgb = (da * uv * (sg * (1.0 + gv * (1.0 - sg)))).astype(BF16)
            dub = (da * silu).astype(BF16)
            dg_ref[k] = dgb
            du_ref[k] = dub
            dh = dh + _dot_nt(dgb, wg_ref[k]) + _dot_nt(dub, wu_ref[k])
        dxn, dp = _rms_bwd(dh, xv, pg_ref[...])
        dx_ref[...] = dyv + dxn
        _acc(dpg_ref, dp, first)
        _acc(dqg_ref, dq, first)

    row = pl.BlockSpec((tm, D_MODEL), lambda i: (i, 0))
    gu = pl.BlockSpec((N_SHARD, tm, F_SH), lambda i: (0, i, 0))
    vec = _full((1, D_MODEL))
    big = jax.ShapeDtypeStruct((N_SHARD, s, F_SH), BF16)
    return pl.pallas_call(
        body, name="ffn_bwd", grid=(nt,),
        in_specs=[row, row, gu, gu, row, vec, vec, _res((N_SHARD, D_MODEL, F_SH)), _res((N_SHARD, D_MODEL, F_SH)),
                  _res((N_SHARD, F_SH, D_MODEL))],
        out_specs=[row, row, gu, gu, gu, row, vec, vec],
        out_shape=[jax.ShapeDtypeStruct((s, D_MODEL), F32), jax.ShapeDtypeStruct((s, D_MODEL), BF16), big, big, big,
                   jax.ShapeDtypeStruct((s, D_MODEL), BF16), jax.ShapeDtypeStruct((1, D_MODEL), F32),
                   jax.ShapeDtypeStruct((1, D_MODEL), F32)],
        compiler_params=_cp(),
    )(dy, x, g, u, f, pre_g, post_g, wg, wu, wd)


def _tn(lhs, rhs, name, nb, lhs_blk, lhs_map, rhs_blk, rhs_map, out_shape, out_blk, out_map, ts):
    s = lhs.shape[-2]
    ns = s // ts

    def body(l_ref, r_ref, o_ref):
        _acc(o_ref, _dot_tn(l_ref[...], r_ref[...]), pl.program_id(1) == 0)

    return pl.pallas_call(
        body, name=name, grid=(nb, ns),
        in_specs=[pl.BlockSpec(lhs_blk, lhs_map), pl.BlockSpec(rhs_blk, rhs_map)],
        out_specs=pl.BlockSpec(out_blk, out_map),
        out_shape=jax.ShapeDtypeStruct(out_shape, F32),
        compiler_params=_cp(("parallel", "arbitrary")),
    )(lhs, rhs)


def _tn_plain(lhs, rhs, name, ts):
    m, n = lhs.shape[1], rhs.shape[1]
    return _tn(lhs, rhs, name, 1, (ts, m), lambda b, t: (t, 0), (ts, n), lambda b, t: (t, 0),
               (m, n), (m, n), lambda b, t: (0, 0), ts)


def _ffn_wgrads(h, a, dg, du, df, ts):
    s = h.shape[0]
    sh3 = (N_SHARD, D_MODEL, F_SH)
    lm2 = lambda b, t: (t, 0)
    bm3 = lambda b, t: (b, t, 0)
    om3 = lambda b, t: (b, 0, 0)
    d_wg = _tn(h, dg, "ffn_dwg", N_SHARD, (ts, D_MODEL), lm2, (None, ts, F_SH), bm3, sh3, (None, D_MODEL, F_SH), om3, ts)
    d_wu = _tn(h, du, "ffn_dwu", N_SHARD, (ts, D_MODEL), lm2, (None, ts, F_SH), bm3, sh3, (None, D_MODEL, F_SH), om3, ts)
    d_wd = _tn(a, df, "ffn_dwd", N_SHARD, (None, ts, F_SH), bm3, (ts, D_MODEL), lm2, (N_SHARD, F_SH, D_MODEL),
               (None, F_SH, D_MODEL), om3, ts)
    return d_wg, d_wu, d_wd


def _mix_in_fwd(x, pre_g, w_in, tm):
    s = x.shape[0]
    nt = s // tm
    cw = 512

    def body(x_ref, pg_ref, w_ref, h_ref, za_ref, zm_ref):
        hb = _rms_fwd(x_ref[...], pg_ref[...]).astype(BF16)
        h_ref[...] = hb
        for j in range(ZA_W // cw):
            za_ref[:, j * cw:(j + 1) * cw] = _dot(hb, w_ref[:, j * cw:(j + 1) * cw]).astype(BF16)
        for j in range(ZM_W // cw):
            zm_ref[:, j * cw:(j + 1) * cw] = _dot(hb, w_ref[:, ZA_W + j * cw:ZA_W + (j + 1) * cw]).astype(BF16)

    row = pl.BlockSpec((tm, D_MODEL), lambda i: (i, 0))
    return pl.pallas_call(
        body, name="mix_in_fwd", grid=(nt,),
        in_specs=[row, _full((1, D_MODEL)), _res((D_MODEL, D_IN))],
        out_specs=[row, pl.BlockSpec((tm, ZA_W), lambda i: (i, 0)), pl.BlockSpec((tm, ZM_W), lambda i: (i, 0))],
        out_shape=[jax.ShapeDtypeStruct((s, D_MODEL), BF16), jax.ShapeDtypeStruct((s, ZA_W), BF16),
                   jax.ShapeDtypeStruct((s, ZM_W), BF16)],
        compiler_params=_cp(("parallel",)),
    )(x, pre_g, w_in)


def _lru_conv(xm, prev, nxt, rows, cw_ref, cb_ref):
    acc = cb_ref[...] + cw_ref[2:3, :] * xm
    acc = acc + cw_ref[0:1, :] * _shift(xm, -2, prev, nxt, rows)
    acc = acc + cw_ref[1:2, :] * _shift(xm, -1, prev, nxt, rows)
    acc = acc + cw_ref[3:4, :] * _shift(xm, 1, prev, nxt, rows)
    return acc


def _lru_gates(xc, wa_ref, ba, wx_ref, bx, lam):
    xcb = xc.astype(BF16)
    ra = jnp.concatenate([_dot(xcb[:, LRU_HD * h:LRU_HD * (h + 1)], wa_ref[h]) for h in range(4)], axis=1)
    xa = jnp.concatenate([_dot(xcb[:, LRU_HD * h:LRU_HD * (h + 1)], wx_ref[h]) for h in range(4)], axis=1)
    r = _sigmoid(ra + ba)
    ig = _sigmoid(xa + bx)
    e = jnp.exp(-jnp.abs(lam))
    log1p_e = jnp.where(e < 1e-2, e * (1.0 - e * (0.5 - e * (1.0 / 3.0))), jnp.log(1.0 + e))
    cl = -LRU_C * (jnp.maximum(-lam, 0.0) + log1p_e)
    la = cl * r
    a = jnp.exp(la)
    y2 = 2.0 * la
    em = jnp.where(y2 > -0.02, -y2 * (1.0 + y2 * (0.5 + y2 * (1.0 / 6.0 + y2 * (1.0 / 24.0)))), 1.0 - jnp.exp(y2))
    return xcb, r, ig, cl, a, jnp.sqrt(em)


def _scan_rows(a_s, b_s, o_s, carry, tm, descending):
    nb = tm // 8

    def blk(j, h):
        jb = (nb - 1 - j) if descending else j
        base = pl.multiple_of(jb * 8, 8)
        for r in range(8):
            t = base + ((7 - r) if descending else r)
            h = a_s[pl.ds(t, 1), :] * h + b_s[pl.ds(t, 1), :]
            o_s[pl.ds(t, 1), :] = h
        return h

    return lax.fori_loop(0, nb, blk, carry)


def _lru_fwd(za, conv_w, conv_b, wa, ba, wx, bx, lam, rev, tm):
    s = za.shape[0]
    nt = s // tm

    def body(x_ref, xp_ref, xn_ref, cw_ref, cb_ref, wa_ref, ba_ref, wx_ref, bx_ref, lam_ref, h_ref,
             a_s, u_s, o_s, c_s):
        i = pl.program_id(0)
        ti = (nt - 1 - i) if rev else i
        rows = lax.broadcasted_iota(jnp.int32, (tm, 1), 0)
        prev, nxt = _edges(xp_ref, xn_ref, ti, nt)
        xc = _lru_conv(x_ref[...].astype(F32), prev, nxt, rows, cw_ref, cb_ref)
        _, r, ig, cl, a, mult = _lru_gates(xc, wa_ref, ba_ref[...], wx_ref, bx_ref[...], lam_ref[...])
        a_s[...] = a
        u_s[...] = ig * xc * mult

        @pl.when(i == 0)
        def _():
            c_s[...] = jnp.zeros_like(c_s)

        c_s[...] = _scan_rows(a_s, u_s, o_s, c_s[...], tm, rev)
        h_ref[...] = o_s[...].astype(BF16)

    vec = _full((1, LRU_W))
    hd = _full((4, LRU_HD, LRU_HD))
    ti = (lambda i: nt - 1 - i) if rev else (lambda i: i)
    return pl.pallas_call(
        body, name="lru_fwd_rev" if rev else "lru_fwd", grid=(nt,),
        in_specs=_halo(tm, 16, LRU_W, 1, nt, rev) + [_full((4, LRU_W)), vec, hd, vec, hd, vec, vec],
        out_specs=pl.BlockSpec((tm, LRU_W), lambda i: (ti(i), 0)),
        out_shape=jax.ShapeDtypeStruct((s, LRU_W), BF16),
        scratch_shapes=[pltpu.VMEM((tm, LRU_W), F32), pltpu.VMEM((tm, LRU_W), F32), pltpu.VMEM((tm, LRU_W), F32),
                        pltpu.VMEM((1, LRU_W), F32)],
        compiler_params=_cp(),
    )(za, za, za, conv_w, conv_b, wa, ba, wx, bx, lam)


def _sc_conv(c_ref, cp_ref, cn_ref, x_ref, xp_ref, xn_ref, w_ref, ti, nt, rows):
    cprev, cnext = _edges(cp_ref, cn_ref, ti, nt)
    xprev, xnext = _edges(xp_ref, xn_ref, ti, nt)
    cv = c_ref[...].astype(F32)
    xv = x_ref[...].astype(F32)
    p = cv * xv
    pm1 = _shift(p, -1, cprev * xprev, cnext * xnext, rows)
    pp1 = _shift(p, 1, cprev * xprev, cnext * xnext, rows)
    conv = w_ref[0:1, :] * pm1 + w_ref[1:2, :] * p + w_ref[2:3, :] * pp1
    return cv, xv, p, pm1, pp1, conv


def _sgu_norm(vz, g_ref, b_ref):
    vg, dvg = _gelu_and_grad(vz)
    mu = jnp.mean(vg, axis=-1, keepdims=True)
    cen = vg - mu
    rs = lax.rsqrt(jnp.mean(cen * cen, axis=-1, keepdims=True) + EPS)
    vn = cen * rs
    return dvg, vn, rs, vn * g_ref[...] + b_ref[...]


def _mix_out_fwd(x, za, zm, hf, hr, lru_w_out, sc_conv_w, sc_w_out, ln_g, ln_b, w_s, b_s_t, sgu_w_out, w_o, post_g, tm):
    s = x.shape[0]
    nt = s // tm
    nc = tm // CHUNK

    def body(x_ref, gate_ref, scb_ref, scc_ref, sccp_ref, sccn_ref, scx_ref, scxp_ref, scxn_ref, su_ref, sv_ref,
             zm_ref, hf_ref, hr_ref, wlo_ref, scw_ref, wso_ref, lg_ref, lb_ref, ws_ref, bs_ref, wgo_ref, wo_ref,
             qg_ref, o_ref, yain_ref, q_ref, ycin_ref, ya_ref, yb_ref, yc_ref, m_ref, mx_ref, mixed_s):
        ti = pl.program_id(0)
        rows = lax.broadcasted_iota(jnp.int32, (tm, 1), 0)
        hs = hf_ref[...].astype(F32) + hr_ref[...].astype(F32)
        yain = (hs * _gelu(gate_ref[...].astype(F32))).astype(BF16)
        yain_ref[...] = yain
        ya = _dot(yain, wlo_ref[...])
        _, _, _, _, _, conv = _sc_conv(scc_ref, sccp_ref, sccn_ref, scx_ref, scxp_ref, scxn_ref, scw_ref, ti, nt, rows)
        qb = (scb_ref[...].astype(F32) * conv).astype(BF16)
        q_ref[...] = qb
        yb = _dot(qb, wso_ref[...])
        _, _, _, v = _sgu_norm(sv_ref[...].astype(F32), lg_ref, lb_ref)
        vb = v.astype(BF16)
        for n in range(nc):
            for g in range(4):
                blk = vb[n * CHUNK:(n + 1) * CHUNK, g * CHUNK:(g + 1) * CHUNK]
                mixed_s[n * CHUNK:(n + 1) * CHUNK, g * CHUNK:(g + 1) * CHUNK] = _dot(ws_ref[g], blk) + bs_ref[:, g:g + 1]
        ycin = (_gelu(su_ref[...].astype(F32)) * mixed_s[...]).astype(BF16)
        ycin_ref[...] = ycin
        yc = _dot(ycin, wgo_ref[...])
        m = (_sigmoid(zm_ref[:, 0:D_MODEL].astype(F32)) * ya + _sigmoid(zm_ref[:, D_MODEL:2 * D_MODEL].astype(F32)) * yb
             + _sigmoid(zm_ref[:, 2 * D_MODEL:3 * D_MODEL].astype(F32)) * yc)
        mb = m.astype(BF16)
        mx = _dot(mb, wo_ref[...])
        ya_ref[...] = ya.astype(BF16)
        yb_ref[...] = yb.astype(BF16)
        yc_ref[...] = yc.astype(BF16)
        m_ref[...] = mb
        mx_ref[...] = mx.astype(BF16)
        o_ref[...] = x_ref[...] + _rms_fwd(mx, qg_ref[...])

    row = pl.BlockSpec((tm, D_MODEL), lambda i: (i, 0))
    half = pl.BlockSpec((tm, 512), lambda i: (i, 0))
    col = lambda c: pl.BlockSpec((tm, 512), lambda i: (i, c))
    in_specs = ([row, pl.BlockSpec((tm, LRU_W), lambda i: (i, 0)), col(4)] + _halo(tm, 16, SC_W, 5, nt) + _halo(tm, 16, SC_W, 6, nt)
                + [col(7), col(8), pl.BlockSpec((tm, ZM_W), lambda i: (i, 0)), row, row,
                   _full((LRU_W, D_MODEL)), _full((3, SC_W)), _full((SC_W, D_MODEL)), _full((1, SGU_W)), _full((1, SGU_W)),
                   _full((4, CHUNK, CHUNK)), _full((CHUNK, 4)), _full((SGU_W, D_MODEL)), _full((D_MODEL, D_MODEL)),
                   _full((1, D_MODEL))])
    bf = lambda w: jax.ShapeDtypeStruct((s, w), BF16)
    return pl.pallas_call(
        body, name="mix_out_fwd", grid=(nt,),
        in_specs=in_specs,
        out_specs=[row, row, half, half, row, row, row, row, row],
        out_shape=[jax.ShapeDtypeStruct((s, D_MODEL), F32), bf(LRU_W), bf(SC_W), bf(SGU_W), bf(D_MODEL), bf(D_MODEL),
                   bf(D_MODEL), bf(D_MODEL), bf(D_MODEL)],
        scratch_shapes=[pltpu.VMEM((tm, SGU_W), F32)],
        compiler_params=_cp(("parallel",)),
    )(x, za, za, za, za, za, za, za, za, za, za, zm, hf, hr, lru_w_out, sc_conv_w, sc_w_out, ln_g, ln_b, w_s, b_s_t,
      sgu_w_out, w_o, post_g)


def _mix_out_bwd(dy, za, zm, hf, hr, ya, yb, yc, mx, lru_w_out, sc_conv_w, sc_w_out, ln_g, ln_b, w_s, w_s_t, b_s_t,
                 sgu_w_out, w_o, post_g, tm):
    s = dy.shape[0]
    nt = s // tm
    nc = tm // CHUNK

    def body(dy_ref, gate_ref, scb_ref, scc_ref, sccp_ref, sccn_ref, scx_ref, scxp_ref, scxn_ref, su_ref, sv_ref,
             zm_ref, hf_ref, hr_ref, ya_ref, yb_ref, yc_ref, mx_ref, wlo_ref, scw_ref, wso_ref, lg_ref, lb_ref,
             ws_ref, wst_ref, bs_ref, wgo_ref, wo_ref, qg_ref,
             dmx_ref, dya_ref, dyb_ref, dyc_ref, dgate_ref, dscb_ref, dsu_ref, dsv_ref, dzm_ref, dhs_ref, dcp_ref,
             dqg_ref, dlg_ref, dlb_ref, dws_ref, dbs_ref, mixed_s, dv_s):
        ti = pl.program_id(0)
        first = ti == 0
        rows = lax.broadcasted_iota(jnp.int32, (tm, 1), 0)
        dmx, dq = _rms_bwd(dy_ref[...], mx_ref[...].astype(F32), qg_ref[...])
        _acc(dqg_ref, dq, first)
        dmxb = dmx.astype(BF16)
        dmx_ref[...] = dmxb
        dm = _dot_nt(dmxb, wo_ref[...])
        dys = []
        for k, (y_ref, d_ref) in enumerate(((ya_ref, dya_ref), (yb_ref, dyb_ref), (yc_ref, dyc_ref))):
            gk = _sigmoid(zm_ref[:, k * D_MODEL:(k + 1) * D_MODEL].astype(F32))
            dyk = (dm * gk).astype(BF16)
            d_ref[...] = dyk
            dys.append(dyk)
            dzm_ref[:, k * D_MODEL:(k + 1) * D_MODEL] = (dm * y_ref[...].astype(F32) * gk * (1.0 - gk)).astype(BF16)
        dyain = _dot_nt(dys[0], wlo_ref[...])
        gg, dgg = _gelu_and_grad(gate_ref[...].astype(F32))
        hs = hf_ref[...].astype(F32) + hr_ref[...].astype(F32)
        dhs_ref[...] = dyain * gg
        dgate_ref[...] = (dyain * hs * dgg).astype(BF16)
        dq_b = _dot_nt(dys[1], wso_ref[...])
        _, _, _, _, _, conv = _sc_conv(scc_ref, sccp_ref, sccn_ref, scx_ref, scxp_ref, scxn_ref, scw_ref, ti, nt, rows)
        dscb_ref[...] = (dq_b * conv).astype(BF16)
        dcp_ref[...] = dq_b * scb_ref[...].astype(F32)
        dycin = _dot_nt(dys[2], wgo_ref[...])
        dvg, vn, rs, v = _sgu_norm(sv_ref[...].astype(F32), lg_ref, lb_ref)
        vb = v.astype(BF16)
        ug, dug = _gelu_and_grad(su_ref[...].astype(F32))
        dmixed = dycin * ug
        dmb = dmixed.astype(BF16)
        dws = [jnp.zeros((CHUNK, CHUNK), F32) for _ in range(4)]
        dbs = [jnp.zeros((CHUNK, CHUNK), F32) for _ in range(4)]
        for n in range(nc):
            for g in range(4):
                rs_, cs_ = slice(n * CHUNK, (n + 1) * CHUNK), slice(g * CHUNK, (g + 1) * CHUNK)
                mixed_s[rs_, cs_] = _dot(ws_ref[g], vb[rs_, cs_]) + bs_ref[:, g:g + 1]
                dv_s[rs_, cs_] = _dot(wst_ref[g], dmb[rs_, cs_])
                dws[g] = dws[g] + _dot_nt(dmb[rs_, cs_], vb[rs_, cs_])
                dbs[g] = dbs[g] + dmixed[rs_, cs_]
        for g in range(4):
            _acc(dws_ref.at[g], dws[g], first)
            _acc(dbs_ref.at[g], dbs[g], first)
        dsu_ref[...] = (dycin * mixed_s[...] * dug).astype(BF16)
        dv = dv_s[...]
        _acc(dlg_ref, jnp.sum(dv * vn, axis=0, keepdims=True), first)
        _acc(dlb_ref, jnp.sum(dv, axis=0, keepdims=True), first)
        dvn = dv * lg_ref[...]
        dcen = rs * (dvn - jnp.mean(dvn, axis=-1, keepdims=True) - vn * jnp.mean(dvn * vn, axis=-1, keepdims=True))
        dsv_ref[...] = (dcen * dvg).astype(BF16)

    row = pl.BlockSpec((tm, D_MODEL), lambda i: (i, 0))
    half = pl.BlockSpec((tm, 512), lambda i: (i, 0))
    col = lambda c: pl.BlockSpec((tm, 512), lambda i: (i, c))
    zmrow = pl.BlockSpec((tm, ZM_W), lambda i: (i, 0))
    sq = _full((4, CHUNK, CHUNK))
    in_specs = ([row, pl.BlockSpec((tm, LRU_W), lambda i: (i, 0)), col(4)] + _halo(tm, 16, SC_W, 5, nt) + _halo(tm, 16, SC_W, 6, nt)
                + [col(7), col(8), zmrow, row, row, row, row, row, row,
                   _full((LRU_W, D_MODEL)), _full((3, SC_W)), _full((SC_W, D_MODEL)), _full((1, SGU_W)), _full((1, SGU_W)),
                   sq, sq, _full((CHUNK, 4)), _full((SGU_W, D_MODEL)), _full((D_MODEL, D_MODEL)), _full((1, D_MODEL))])
    bf = lambda w: jax.ShapeDtypeStruct((s, w), BF16)
    return pl.pallas_call(
        body, name="mix_out_bwd", grid=(nt,),
        in_specs=in_specs,
        out_specs=[row, row, row, row, row, half, half, half, zmrow, row, half,
                   _full((1, D_MODEL)), _full((1, SGU_W)), _full((1, SGU_W)), sq, sq],
        out_shape=[bf(D_MODEL), bf(D_MODEL), bf(D_MODEL), bf(D_MODEL), bf(LRU_W), bf(SC_W), bf(SGU_W), bf(SGU_W), bf(ZM_W),
                   jax.ShapeDtypeStruct((s, LRU_W), F32), jax.ShapeDtypeStruct((s, SC_W), F32),
                   jax.ShapeDtypeStruct((1, D_MODEL), F32), jax.ShapeDtypeStruct((1, SGU_W), F32),
                   jax.ShapeDtypeStruct((1, SGU_W), F32), jax.ShapeDtypeStruct((4, CHUNK, CHUNK), F32),
                   jax.ShapeDtypeStruct((4, CHUNK, CHUNK), F32)],
        scratch_shapes=[pltpu.VMEM((tm, SGU_W), F32), pltpu.VMEM((tm, SGU_W), F32)],
        compiler_params=_cp(),
    )(dy, za, za, za, za, za, za, za, za, za, za, zm, hf, hr, ya, yb, yc, mx, lru_w_out, sc_conv_w, sc_w_out, ln_g, ln_b,
      w_s, w_s_t, b_s_t, sgu_w_out, w_o, post_g)


def _lru_bwd(za, h_dir, dhs, conv_w, conv_b, wa, ba, wx, bx, lam, rev, tm):
    s = za.shape[0]
    nt = s // tm
    back = not rev

    def body(x_ref, xp_ref, xn_ref, h_ref, hp_ref, hn_ref, dh_ref, cw_ref, cb_ref, wa_ref, ba_ref, wx_ref, bx_ref,
             lam_ref, dxc_ref, dwa_ref, dba_ref, dwx_ref, dbx_ref, dlam_ref, a_s, l_s, c_s):
        i = pl.program_id(0)
        first = i == 0
        ti = (nt - 1 - i) if back else i
        rows = lax.broadcasted_iota(jnp.int32, (tm, 1), 0)
        prev, nxt = _edges(xp_ref, xn_ref, ti, nt)
        xc = _lru_conv(x_ref[...].astype(F32), prev, nxt, rows, cw_ref, cb_ref)
        xcb, r, ig, cl, a, mult = _lru_gates(xc, wa_ref, ba_ref[...], wx_ref, bx_ref[...], lam_ref[...])
        a_s[...] = a

        @pl.when(first)
        def _():
            c_s[...] = jnp.zeros_like(c_s)

        nb = tm // 8

        def blk(j, c):
            jb = (nb - 1 - j) if back else j
            base = pl.multiple_of(jb * 8, 8)
            for q in range(8):
                t = base + ((7 - q) if back else q)
                lt = dh_ref[pl.ds(t, 1), :] + c
                l_s[pl.ds(t, 1), :] = lt
                c = a_s[pl.ds(t, 1), :] * lt
            return c

        c_s[...] = lax.fori_loop(0, nb, blk, c_s[...])

        du = l_s[...]
        hprev, hnext = _edges(hp_ref, hn_ref, ti, nt)
        hsh = _shift(h_ref[...].astype(F32), 1 if rev else -1, hprev, hnext, rows)
        dmult = du * ig * xc
        dla = du * hsh * a - dmult * (a * a) / mult
        di = du * xc * mult
        dxc = du * ig * mult
        _acc(dlam_ref, jnp.sum(dla * r, axis=0, keepdims=True), first)
        drp = dla * cl * r * (1.0 - r)
        dip = di * ig * (1.0 - ig)
        _acc(dba_ref, jnp.sum(drp, axis=0, keepdims=True), first)
        _acc(dbx_ref, jnp.sum(dip, axis=0, keepdims=True), first)
        drb = drp.astype(BF16)
        dib = dip.astype(BF16)
        parts = []
        for h in range(4):
            cs_ = slice(LRU_HD * h, LRU_HD * (h + 1))
            parts.append(_dot_nt(drb[:, cs_], wa_ref[h]) + _dot_nt(dib[:, cs_], wx_ref[h]))
            _acc(dwa_ref.at[h], _dot_tn(xcb[:, cs_], drb[:, cs_]), first)
            _acc(dwx_ref.at[h], _dot_tn(xcb[:, cs_], dib[:, cs_]), first)
        dxc_ref[...] = dxc + jnp.concatenate(parts, axis=1)

        @pl.when(i == nt - 1)
        def _():
            dlam_ref[...] = dlam_ref[...] * (LRU_C * _sigmoid(-lam_ref[...]))

    vec = _full((1, LRU_W))
    hd = _full((4, LRU_HD, LRU_HD))
    tix = (lambda i: nt - 1 - i) if back else (lambda i: i)
    rowspec = pl.BlockSpec((tm, LRU_W), lambda i: (tix(i), 0))
    return pl.pallas_call(
        body, name="lru_bwd_rev" if rev else "lru_bwd", grid=(nt,),
        in_specs=_halo(tm, 16, LRU_W, 1, nt, back) + _halo(tm, 16, LRU_W, 0, nt, back) + [rowspec, _full((4, LRU_W)), vec, hd, vec, hd, vec, vec],
        out_specs=[rowspec, hd, vec, hd, vec, vec],
        out_shape=[jax.ShapeDtypeStruct((s, LRU_W), F32), jax.ShapeDtypeStruct((4, LRU_HD, LRU_HD), F32),
                   jax.ShapeDtypeStruct((1, LRU_W), F32), jax.ShapeDtypeStruct((4, LRU_HD, LRU_HD), F32),
                   jax.ShapeDtypeStruct((1, LRU_W), F32), jax.ShapeDtypeStruct((1, LRU_W), F32)],
        scratch_shapes=[pltpu.VMEM((tm, LRU_W), F32), pltpu.VMEM((tm, LRU_W), F32), pltpu.VMEM((1, LRU_W), F32)],
        compiler_params=_cp(),
    )(za, za, za, h_dir, h_dir, h_dir, dhs, conv_w, conv_b, wa, ba, wx, bx, lam)


def _mix_in_bwd(dy, x, za, dxc_f, dxc_r, dcp, dgate, dscb, dsu, dsv, dzm, pre_g, lru_conv_w, sc_conv_w, w_in, tm):
    s = x.shape[0]
    nt = s // tm
    cw = 512

    def body(dy_ref, x_ref, lx_ref, lxp_ref, lxn_ref, scc_ref, sccp_ref, sccn_ref, scx_ref, scxp_ref, scxn_ref,
             df_ref, dfp_ref, dfn_ref, dr_ref, drp_ref, drn_ref, dcp_ref, dcpp_ref, dcpn_ref,
             dgate_ref, dscb_ref, dsu_ref, dsv_ref, dzm_ref, pg_ref, lcw_ref, scw_ref, w_ref,
             dx_ref, dz_ref, dpg_ref, dlcw_ref, dlcb_ref, dscw_ref):
        ti = pl.program_id(0)
        first = ti == 0
        rows = lax.broadcasted_iota(jnp.int32, (tm, 1), 0)
        fp, fn = _edges(dfp_ref, dfn_ref, ti, nt)
        rp, rn = _edges(drp_ref, drn_ref, ti, nt)
        dxc = df_ref[...] + dr_ref[...]
        dprev, dnext = fp + rp, fn + rn
        dlx = lcw_ref[2:3, :] * dxc
        dlx = dlx + lcw_ref[0:1, :] * _shift(dxc, 2, dprev, dnext, rows)
        dlx = dlx + lcw_ref[1:2, :] * _shift(dxc, 1, dprev, dnext, rows)
        dlx = dlx + lcw_ref[3:4, :] * _shift(dxc, -1, dprev, dnext, rows)
        lprev, lnext = _edges(lxp_ref, lxn_ref, ti, nt)
        lx = lx_ref[...].astype(F32)
        _acc(dlcb_ref, jnp.sum(dxc, axis=0, keepdims=True), first)
        for k, d in enumerate((-2, -1, 0, 1)):
            _acc(dlcw_ref.at[pl.ds(k, 1), :], jnp.sum(dxc * _shift(lx, d, lprev, lnext, rows), axis=0, keepdims=True), first)
        cv, xv, p, pm1, pp1, _ = _sc_conv(scc_ref, sccp_ref, sccn_ref, scx_ref, scxp_ref, scxn_ref, scw_ref, ti, nt, rows)
        cprev, cnext = _edges(dcpp_ref, dcpn_ref, ti, nt)
        dcp_v = dcp_ref[...]
        dp = (scw_ref[1:2, :] * dcp_v + scw_ref[0:1, :] * _shift(dcp_v, 1, cprev, cnext, rows)
              + scw_ref[2:3, :] * _shift(dcp_v, -1, cprev, cnext, rows))
        for k, pk in enumerate((pm1, p, pp1)):
            _acc(dscw_ref.at[pl.ds(k, 1), :], jnp.sum(dcp_v * pk, axis=0, keepdims=True), first)
        dz_ref[:, 0:1024] = dgate_ref[...]
        dz_ref[:, 1024:2048] = dlx.astype(BF16)
        dz_ref[:, 2048:2560] = dscb_ref[...]
        dz_ref[:, 2560:3072] = (dp * xv).astype(BF16)
        dz_ref[:, 3072:3584] = (dp * cv).astype(BF16)
        dz_ref[:, 3584:4096] = dsu_ref[...]
        dz_ref[:, 4096:4608] = dsv_ref[...]
        dz_ref[:, 4608:7680] = dzm_ref[...]
        dh = jnp.zeros((tm, D_MODEL), F32)
        for j in range(D_IN // cw):
            dh = dh + _dot_nt(dz_ref[:, j * cw:(j + 1) * cw], w_ref[:, j * cw:(j + 1) * cw])
        dxn, dpg = _rms_bwd(dh, x_ref[...], pg_ref[...])
        dx_ref[...] = dy_ref[...] + dxn
        _acc(dpg_ref, dpg, first)

    row = pl.BlockSpec((tm, D_MODEL), lambda i: (i, 0))
    half = pl.BlockSpec((tm, 512), lambda i: (i, 0))
    in_specs = ([row, row] + _halo(tm, 16, LRU_W, 1, nt) + _halo(tm, 16, SC_W, 5, nt) + _halo(tm, 16, SC_W, 6, nt)
                + _halo(tm, 8, LRU_W, 0, nt) + _halo(tm, 8, LRU_W, 0, nt) + _halo(tm, 8, SC_W, 0, nt)
                + [row, half, half, half, pl.BlockSpec((tm, ZM_W), lambda i: (i, 0)),
                   _full((1, D_MODEL)), _full((4, LRU_W)), _full((3, SC_W)), _res((D_MODEL, D_IN))])
    return pl.pallas_call(
        body, name="mix_in_bwd", grid=(nt,),
        in_specs=in_specs,
        out_specs=[row, pl.BlockSpec((tm, D_IN), lambda i: (i, 0)), _full((1, D_MODEL)), _full((4, LRU_W)),
                   _full((1, LRU_W)), _full((3, SC_W))],
        out_shape=[jax.ShapeDtypeStruct((s, D_MODEL), F32), jax.ShapeDtypeStruct((s, D_IN), BF16),
                   jax.ShapeDtypeStruct((1, D_MODEL), F32), jax.ShapeDtypeStruct((4, LRU_W), F32),
                   jax.ShapeDtypeStruct((1, LRU_W), F32), jax.ShapeDtypeStruct((3, SC_W), F32)],
        compiler_params=_cp(),
    )(dy, x, za, za, za, za, za, za, za, za, za, dxc_f, dxc_f, dxc_f, dxc_r, dxc_r, dxc_r, dcp, dcp, dcp,
      dgate, dscb, dsu, dsv, dzm, pre_g, lru_conv_w, sc_conv_w, w_in)


def _loss_head(y, target, tm):
    s = y.shape[0]
    nt = s // tm

    def body(y_ref, t_ref, dy_ref, acc_ref):
        err = y_ref[...] - t_ref[...]
        dy_ref[...] = err * (1.0 / D_MODEL)
        _acc(acc_ref, jnp.sum(err * err, axis=0, keepdims=True), pl.program_id(0) == 0)

    row = pl.BlockSpec((tm, D_MODEL), lambda i: (i, 0))
    return pl.pallas_call(
        body, name="loss_head", grid=(nt,), in_specs=[row, row], out_specs=[row, _full((1, D_MODEL))],
        out_shape=[jax.ShapeDtypeStruct((s, D_MODEL), F32), jax.ShapeDtypeStruct((1, D_MODEL), F32)],
        compiler_params=_cp(),
    )(y, target)


def _row_tile(rows, cols):
    cap = max(8, (2 ** 18) // cols)
    best = None
    for t in range(8, min(rows, cap) + 1, 8):
        if rows % t == 0:
            best = t
    return best if best is not None else rows


def _adamw(w, g, m, v, name):
    shape = w.shape
    cols = shape[-1]
    rows = int(np.prod(shape[:-1]))
    tr = _row_tile(rows, cols)
    bc1 = 1.0 - ADAM_B1 ** ADAM_STEP
    bc2 = 1.0 - ADAM_B2 ** ADAM_STEP

    def body(w_ref, g_ref, m_ref, v_ref, d_ref, nm_ref, nv_ref):
        gv = g_ref[...]
        mn = ADAM_B1 * m_ref[...] + (1.0 - ADAM_B1) * gv
        vn = ADAM_B2 * v_ref[...] + (1.0 - ADAM_B2) * (gv * gv)
        nm_ref[...] = mn
        nv_ref[...] = vn
        d_ref[...] = -ADAM_LR * ((mn / bc1) / (jnp.sqrt(vn / bc2) + ADAM_EPS) + ADAM_WD * w_ref[...])

    spec = pl.BlockSpec((tr, cols), lambda i: (i, 0))
    sds = jax.ShapeDtypeStruct((rows, cols), F32)
    outs = pl.pallas_call(
        body, name="adamw_" + name, grid=(rows // tr,), in_specs=[spec] * 4, out_specs=[spec] * 3,
        out_shape=[sds, sds, sds], compiler_params=_cp(("parallel",)),
    )(w.reshape(rows, cols), g.reshape(rows, cols), m.reshape(rows, cols), v.reshape(rows, cols))
    return [o.reshape(shape) for o in outs]


HBM = pl.BlockSpec(memory_space=pl.ANY)


def _place():
    x, y, c = lax.axis_index("x"), lax.axis_index("y"), lax.axis_index("c")
    chips = [(1 - x, y), (x, 1 - y), (1 - x, 1 - y)]
    return x, y, c, chips


def _all_gather(buf, name):
    r, cdim = buf.shape
    r2 = r // 2

    def body(b_ref, o_ref, ssem, rsem, lsem):
        x, y, c, chips = _place()
        me = 2 * x + y
        sib = (x, y, 1 - c)
        mine = pl.ds(pl.multiple_of(c * r2, 16), r2)
        other = pl.ds(pl.multiple_of((1 - c) * r2, 16), r2)

        def rc(k, src, dst, to):
            return pltpu.make_async_remote_copy(src_ref=src, dst_ref=dst, send_sem=ssem.at[k], recv_sem=rsem.at[k],
                                                device_id=to, device_id_type=MESH)

        local = pltpu.make_async_copy(b_ref, o_ref.at[me], lsem)
        local.start()
        sent = []
        for j, (cx, cy) in enumerate(chips):
            cp = rc(j, b_ref.at[mine], o_ref.at[me, mine], (cx, cy, c))
            cp.start()
            sent.append(cp)
        for j, (cx, cy) in enumerate(chips):
            blk = o_ref.at[2 * cx + cy, mine]
            rc(j, blk, blk, (cx, cy, c)).wait_recv()
            cp = rc(3 + j, blk, blk, sib)
            cp.start()
            sent.append(cp)
        for j, (cx, cy) in enumerate(chips):
            blk = o_ref.at[2 * cx + cy, other]
            rc(3 + j, blk, blk, sib).wait_recv()
        for cp in sent:
            cp.wait_send()
        local.wait()

    return pl.pallas_call(
        body, name=name, in_specs=[HBM], out_specs=HBM,
        out_shape=jax.ShapeDtypeStruct((N_SHARD, r, cdim), buf.dtype),
        scratch_shapes=[pltpu.SemaphoreType.DMA((6,)), pltpu.SemaphoreType.DMA((6,)), pltpu.SemaphoreType.DMA],
    )(buf)


def _pair_swap(send, name):
    def body(s_ref, r_ref, ssem, rsem):
        x, y, c, _ = _place()
        cp = pltpu.make_async_remote_copy(src_ref=s_ref, dst_ref=r_ref, send_sem=ssem, recv_sem=rsem,
                                          device_id=(x, y, 1 - c), device_id_type=MESH)
        cp.start()
        cp.wait()

    return pl.pallas_call(
        body, name=name, in_specs=[HBM], out_specs=HBM, out_shape=jax.ShapeDtypeStruct(send.shape, send.dtype),
        scratch_shapes=[pltpu.SemaphoreType.DMA, pltpu.SemaphoreType.DMA],
    )(send)


def _chip_exchange(blocks, name):
    def body(b_ref, o_ref, ssem, rsem, lsem):
        x, y, c, chips = _place()
        me = 2 * x + y
        local = pltpu.make_async_copy(b_ref.at[me], o_ref.at[me], lsem)
        local.start()
        sent = []
        for j, (cx, cy) in enumerate(chips):
            cp = pltpu.make_async_remote_copy(src_ref=b_ref.at[2 * cx + cy], dst_ref=o_ref.at[me], send_sem=ssem.at[j],
                                              recv_sem=rsem.at[j], device_id=(cx, cy, c), device_id_type=MESH)
            cp.start()
            sent.append(cp)
        for j, (cx, cy) in enumerate(chips):
            blk = o_ref.at[2 * cx + cy]
            pltpu.make_async_remote_copy(src_ref=blk, dst_ref=blk, send_sem=ssem.at[j], recv_sem=rsem.at[j],
                                         device_id=(cx, cy, c), device_id_type=MESH).wait_recv()
        for cp in sent:
            cp.wait_send()
        local.wait()

    return pl.pallas_call(
        body, name=name, in_specs=[HBM], out_specs=HBM, out_shape=jax.ShapeDtypeStruct(blocks.shape, blocks.dtype),
        scratch_shapes=[pltpu.SemaphoreType.DMA((3,)), pltpu.SemaphoreType.DMA((3,)), pltpu.SemaphoreType.DMA],
    )(blocks)


def _pair_gather(half, name):
    def body(h_ref, o_ref, ssem, rsem, lsem):
        x, y, c, _ = _place()
        local = pltpu.make_async_copy(h_ref, o_ref.at[c], lsem)
        local.start()
        cp = pltpu.make_async_remote_copy(src_ref=h_ref, dst_ref=o_ref.at[c], send_sem=ssem, recv_sem=rsem,
                                          device_id=(x, y, 1 - c), device_id_type=MESH)
        cp.start()
        got = o_ref.at[1 - c]
        pltpu.make_async_remote_copy(src_ref=got, dst_ref=got, send_sem=ssem, recv_sem=rsem,
                                     device_id=(x, y, 1 - c), device_id_type=MESH).wait_recv()
        cp.wait_send()
        local.wait()

    return pl.pallas_call(
        body, name=name, in_specs=[HBM], out_specs=HBM, out_shape=jax.ShapeDtypeStruct((2,) + half.shape, half.dtype),
        scratch_shapes=[pltpu.SemaphoreType.DMA, pltpu.SemaphoreType.DMA, pltpu.SemaphoreType.DMA],
    )(half)


def _pair_sum(keep, got):
    n, r2, cdim = keep.shape

    def body(a_ref, b_ref, o_ref):
        o_ref[...] = (a_ref[...].astype(F32) + b_ref[...].astype(F32)).astype(BF16)

    spec = pl.BlockSpec((1, RS_TILE, cdim), lambda k, i: (k, i, 0))
    return pl.pallas_call(
        body, name="rs_pair_sum", grid=(n, r2 // RS_TILE), in_specs=[spec, spec], out_specs=spec,
        out_shape=jax.ShapeDtypeStruct(keep.shape, BF16), compiler_params=_cp(("parallel", "parallel")),
    )(keep, got)


def _chip_sum(slots):
    n, r2, cdim = slots.shape

    def body(s_ref, o_ref):
        o_ref[...] = ((s_ref[0].astype(F32) + s_ref[1].astype(F32)) + s_ref[2].astype(F32)) + s_ref[3].astype(F32)

    return pl.pallas_call(
        body, name="rs_chip_sum", grid=(r2 // RS_TILE,),
        in_specs=[pl.BlockSpec((n, RS_TILE, cdim), lambda i: (0, i, 0))],
        out_specs=pl.BlockSpec((RS_TILE, cdim), lambda i: (i, 0)),
        out_shape=jax.ShapeDtypeStruct((r2, cdim), F32), compiler_params=_cp(("parallel",)),
    )(slots)


def _reduce_scatter(packed, tag):
    r2 = packed.shape[1] // 2
    c = lax.axis_index("c")
    keep = lax.dynamic_slice_in_dim(packed, c * r2, r2, axis=1)
    send = lax.dynamic_slice_in_dim(packed, (1 - c) * r2, r2, axis=1)
    got = _pair_swap(send, "rs_pair_swap" + tag)
    slots = _chip_exchange(_pair_sum(keep, got), "rs_chip_exchange" + tag)
    both = _pair_gather(_chip_sum(slots), "rs_pair_gather" + tag)
    return both.reshape(2 * r2, packed.shape[2])


BIG = [("ffn1_w_gate", (D_MODEL, F_SH)), ("ffn1_w_up", (D_MODEL, F_SH)), ("ffn1_w_down", (F_SH, D_MODEL)),
       ("ffn2_w_gate", (D_MODEL, F_SH)), ("ffn2_w_up", (D_MODEL, F_SH)), ("ffn2_w_down", (F_SH, D_MODEL)),
       ("w_in", (D_MODEL, D_IN_SH)), ("lru_wa", (2, 4, 64, LRU_HD)), ("lru_wx", (2, 4, 64, LRU_HD)),
       ("lru_w_out", (256, D_MODEL)), ("sc_w_out", (SC_W, 256)), ("sgu_w_out", (SGU_W, 256)), ("w_o", (256, D_MODEL))]
SMALL = [("lru_conv_w", (4, 256)), ("lru_ba", (2, 256)), ("lru_bx", (2, 256)), ("lru_lambda", (2, 256)),
         ("sc_conv_w", (3, 128))]
REPL = [("ffn1_pre_g", (D_MODEL,)), ("ffn1_post_g", (D_MODEL,)), ("mix_pre_g", (D_MODEL,)), ("lru_conv_b", (LRU_W,)),
        ("sgu_ln_g", (SGU_W,)), ("sgu_ln_b", (SGU_W,)), ("sgu_w_s", (4, CHUNK, CHUNK)), ("sgu_b", (4, CHUNK)),
        ("mix_post_g", (D_MODEL,)), ("ffn2_pre_g", (D_MODEL,)), ("ffn2_post_g", (D_MODEL,))]
WEIGHTS = ['ffn1_pre_g', 'ffn1_w_gate', 'ffn1_w_up', 'ffn1_w_down', 'ffn1_post_g', 'mix_pre_g', 'w_in', 'lru_conv_w',
           'lru_conv_b', 'lru_wa', 'lru_ba', 'lru_wx', 'lru_bx', 'lru_lambda', 'lru_w_out', 'sc_conv_w', 'sc_w_out',
           'sgu_ln_g', 'sgu_ln_b', 'sgu_w_s', 'sgu_b', 'sgu_w_out', 'w_o', 'mix_post_g', 'ffn2_pre_g', 'ffn2_w_gate',
           'ffn2_w_up', 'ffn2_w_down', 'ffn2_post_g']


def _seg_rows(shape):
    return -(-int(np.prod(shape)) // PACK_C)


def _pack(parts, rows_total, dtype):
    lead = parts[0].shape[0]
    segs = []
    used = 0
    for p in parts:
        flat = p.reshape(lead, -1).astype(dtype)
        nr = -(-flat.shape[1] // PACK_C)
        pad = nr * PACK_C - flat.shape[1]
        if pad:
            flat = jnp.pad(flat, ((0, 0), (0, pad)))
        segs.append(flat.reshape(lead, nr, PACK_C))
        used += nr
    if rows_total > used:
        segs.append(jnp.zeros((lead, rows_total - used, PACK_C), dtype))
    return jnp.concatenate(segs, axis=1)


def _unpack(buf, specs):
    lead = buf.shape[0]
    out = {}
    r0 = 0
    for name, shape in specs:
        nr = _seg_rows(shape)
        n = int(np.prod(shape))
        out[name] = buf[:, r0:r0 + nr].reshape(lead, nr * PACK_C)[:, :n].reshape((lead,) + tuple(shape))
        r0 += nr
    return out


def _full_from_shards(name, t):
    if name in ("ffn1_w_gate", "ffn1_w_up", "ffn1_w_down", "ffn2_w_gate", "ffn2_w_up", "ffn2_w_down"):
        return t
    if name in ("w_in", "sc_w_out", "sgu_w_out", "lru_conv_w", "lru_ba", "lru_bx", "lru_lambda", "sc_conv_w"):
        return jnp.moveaxis(t, 0, -2).reshape(t.shape[1:-1] + (N_SHARD * t.shape[-1],))
    if name in ("lru_wa", "lru_wx"):
        return jnp.moveaxis(t, 0, 2).reshape(2, 4, LRU_HD, LRU_HD)
    if name in ("lru_w_out", "w_o"):
        return t.reshape(N_SHARD * t.shape[1], t.shape[2])
    raise ValueError(name)


def _shards_from_full(name, gfull):
    if name in ("ffn1_w_gate", "ffn1_w_up", "ffn1_w_down", "ffn2_w_gate", "ffn2_w_up", "ffn2_w_down"):
        return gfull
    if name in ("w_in", "sc_w_out", "sgu_w_out", "lru_conv_w", "lru_ba", "lru_bx", "lru_lambda", "sc_conv_w"):
        lastdim = gfull.shape[-1] // N_SHARD
        return jnp.moveaxis(gfull.reshape(gfull.shape[:-1] + (N_SHARD, lastdim)), -2, 0)
    if name in ("lru_wa", "lru_wx"):
        return jnp.moveaxis(gfull.reshape(2, 4, N_SHARD, 64, LRU_HD), 2, 0)
    if name in ("lru_w_out", "w_o"):
        return gfull.reshape(N_SHARD, gfull.shape[0] // N_SHARD, gfull.shape[1])
    raise ValueError(name)


def _tiles(s):
    return dict(ffn=min(512, s), ffn_bwd=min(256, s), tn=min(512, s), mix_in=min(512, s), lru=min(512, s), mix=min(256, s), loss=min(512, s))


def _mixer_fwd(x, w, t):
    hb, za, zm = _mix_in_fwd(x, w["mix_pre_g"], w["w_in"], t["mix_in"])
    hf = _lru_fwd(za, w["lru_conv_w"], w["lru_conv_b"], w["lru_wa"][0], w["lru_ba"][0:1], w["lru_wx"][0],
                  w["lru_bx"][0:1], w["lru_lambda"][0:1], False, t["lru"])
    hr = _lru_fwd(za, w["lru_conv_w"], w["lru_conv_b"], w["lru_wa"][1], w["lru_ba"][1:2], w["lru_wx"][1],
                  w["lru_bx"][1:2], w["lru_lambda"][1:2], True, t["lru"])
    out, yain, q, ycin, ya, yb, yc, mb, mx = _mix_out_fwd(
        x, za, zm, hf, hr, w["lru_w_out"], w["sc_conv_w"], w["sc_w_out"], w["sgu_ln_g"], w["sgu_ln_b"], w["sgu_w_s"],
        w["sgu_b_t"], w["sgu_w_out"], w["w_o"], w["mix_post_g"], t["mix"])
    return out, dict(x=x, hb=hb, za=za, zm=zm, hf=hf, hr=hr, yain=yain, q=q, ycin=ycin, ya=ya, yb=yb, yc=yc, mb=mb, mx=mx)


def _mixer_bwd(dy, sv, w, t):
    g = {}
    (dmx, dya, dyb, dyc, dgate, dscb, dsu, dsv, dzm, dhs, dcp, g["mix_post_g"], g["sgu_ln_g"], g["sgu_ln_b"],
     g["sgu_w_s"], dbs) = _mix_out_bwd(
        dy, sv["za"], sv["zm"], sv["hf"], sv["hr"], sv["ya"], sv["yb"], sv["yc"], sv["mx"], w["lru_w_out"], w["sc_conv_w"],
        w["sc_w_out"], w["sgu_ln_g"], w["sgu_ln_b"], w["sgu_w_s"], w["sgu_w_s_t"], w["sgu_b_t"], w["sgu_w_out"], w["w_o"],
        w["mix_post_g"], t["mix"])
    g["sgu_b"] = jnp.sum(dbs, axis=-1)
    ts = t["tn"]
    g["w_o"] = _tn_plain(sv["mb"], dmx, "dw_o", ts)
    g["lru_w_out"] = _tn_plain(sv["yain"], dya, "dw_lru_out", ts)
    g["sc_w_out"] = _tn_plain(sv["q"], dyb, "dw_sc_out", ts)
    g["sgu_w_out"] = _tn_plain(sv["ycin"], dyc, "dw_sgu_out", ts)
    dxc, dwa, dba, dwx, dbx, dlam = [], [], [], [], [], []
    for d, rev in enumerate((False, True)):
        o = _lru_bwd(sv["za"], sv["hr"] if rev else sv["hf"], dhs, w["lru_conv_w"], w["lru_conv_b"], w["lru_wa"][d],
                     w["lru_ba"][d:d + 1], w["lru_wx"][d], w["lru_bx"][d:d + 1], w["lru_lambda"][d:d + 1], rev, t["lru"])
        for lst, val in zip((dxc, dwa, dba, dwx, dbx, dlam), o):
            lst.append(val)
    g["lru_wa"] = jnp.stack(dwa)
    g["lru_wx"] = jnp.stack(dwx)
    g["lru_ba"] = jnp.concatenate(dba, axis=0)
    g["lru_bx"] = jnp.concatenate(dbx, axis=0)
    g["lru_lambda"] = jnp.concatenate(dlam, axis=0)
    dx, dz, g["mix_pre_g"], g["lru_conv_w"], g["lru_conv_b"], g["sc_conv_w"] = _mix_in_bwd(
        dy, sv["x"], sv["za"], dxc[0], dxc[1], dcp, dgate, dscb, dsu, dsv, dzm, w["mix_pre_g"], w["lru_conv_w"],
        w["sc_conv_w"], w["w_in"], t["mix"])
    g["w_in"] = _tn(sv["hb"], dz, "dw_in", N_SHARD, (ts, D_MODEL), lambda b, s_: (s_, 0), (ts, D_IN_SH), lambda b, s_: (s_, b),
                    (D_MODEL, D_IN), (D_MODEL, D_IN_SH), lambda b, s_: (0, b), ts)
    return dx, g


def kernel(x, ffn1_pre_g, ffn1_w_gate, ffn1_w_up, ffn1_w_down, ffn1_post_g, mix_pre_g, w_in, lru_conv_w, lru_conv_b, lru_wa, lru_ba, lru_wx, lru_bx, lru_lambda, lru_w_out, sc_conv_w, sc_w_out, sgu_ln_g, sgu_ln_b, sgu_w_s, sgu_b, sgu_w_out, w_o, mix_post_g, ffn2_pre_g, ffn2_w_gate, ffn2_w_up, ffn2_w_down, ffn2_post_g, loss_target, m_ffn1_pre_g, m_ffn1_w_gate, m_ffn1_w_up, m_ffn1_w_down, m_ffn1_post_g, m_mix_pre_g, m_w_in, m_lru_conv_w, m_lru_conv_b, m_lru_wa, m_lru_ba, m_lru_wx, m_lru_bx, m_lru_lambda, m_lru_w_out, m_sc_conv_w, m_sc_w_out, m_sgu_ln_g, m_sgu_ln_b, m_sgu_w_s, m_sgu_b, m_sgu_w_out, m_w_o, m_mix_post_g, m_ffn2_pre_g, m_ffn2_w_gate, m_ffn2_w_up, m_ffn2_w_down, m_ffn2_post_g, v_ffn1_pre_g, v_ffn1_w_gate, v_ffn1_w_up, v_ffn1_w_down, v_ffn1_post_g, v_mix_pre_g, v_w_in, v_lru_conv_w, v_lru_conv_b, v_lru_wa, v_lru_ba, v_lru_wx, v_lru_bx, v_lru_lambda, v_lru_w_out, v_sc_conv_w, v_sc_w_out, v_sgu_ln_g, v_sgu_ln_b, v_sgu_w_s, v_sgu_b, v_sgu_w_out, v_w_o, v_mix_post_g, v_ffn2_pre_g, v_ffn2_w_gate, v_ffn2_w_up, v_ffn2_w_down, v_ffn2_post_g):
    args = (ffn1_pre_g, ffn1_w_gate, ffn1_w_up, ffn1_w_down, ffn1_post_g, mix_pre_g, w_in, lru_conv_w, lru_conv_b, lru_wa, lru_ba, lru_wx, lru_bx, lru_lambda, lru_w_out, sc_conv_w, sc_w_out, sgu_ln_g, sgu_ln_b, sgu_w_s, sgu_b, sgu_w_out, w_o, mix_post_g, ffn2_pre_g, ffn2_w_gate, ffn2_w_up, ffn2_w_down, ffn2_post_g)
    margs = (m_ffn1_pre_g, m_ffn1_w_gate, m_ffn1_w_up, m_ffn1_w_down, m_ffn1_post_g, m_mix_pre_g, m_w_in, m_lru_conv_w, m_lru_conv_b, m_lru_wa, m_lru_ba, m_lru_wx, m_lru_bx, m_lru_lambda, m_lru_w_out, m_sc_conv_w, m_sc_w_out, m_sgu_ln_g, m_sgu_ln_b, m_sgu_w_s, m_sgu_b, m_sgu_w_out, m_w_o, m_mix_post_g, m_ffn2_pre_g, m_ffn2_w_gate, m_ffn2_w_up, m_ffn2_w_down, m_ffn2_post_g)
    vargs = (v_ffn1_pre_g, v_ffn1_w_gate, v_ffn1_w_up, v_ffn1_w_down, v_ffn1_post_g, v_mix_pre_g, v_w_in, v_lru_conv_w, v_lru_conv_b, v_lru_wa, v_lru_ba, v_lru_wx, v_lru_bx, v_lru_lambda, v_lru_w_out, v_sc_conv_w, v_sc_w_out, v_sgu_ln_g, v_sgu_ln_b, v_sgu_w_s, v_sgu_b, v_sgu_w_out, v_w_o, v_mix_post_g, v_ffn2_pre_g, v_ffn2_w_gate, v_ffn2_w_up, v_ffn2_w_down, v_ffn2_post_g)
    wsh = dict(zip(WEIGHTS, args))
    msh = dict(zip(WEIGHTS, margs))
    vsh = dict(zip(WEIGHTS, vargs))
    xs = x[0]
    s = xs.shape[0]
    t = _tiles(s)

    small = _pack([wsh[n].reshape(1, -1) for n, _ in SMALL], SMALL_ROWS, F32)[0]
    small_all = _all_gather(small, "ag_small")
    small_un = _unpack(small_all, [(n, (DEPTH,) + sh) for n, sh in SMALL])
    layers = []
    for l in range(DEPTH):
        packed = _pack([wsh[n][l][None] for n, _ in BIG], AG_ROWS, BF16)[0]
        gathered = _all_gather(packed, "ag_weights_l%d" % l)
        un = _unpack(gathered, BIG)
        w = {n: _full_from_shards(n, un[n]) for n, _ in BIG}
        for n, _ in SMALL:
            w[n] = _full_from_shards(n, small_un[n][:, l])
        for n, _ in REPL:
            w[n] = wsh[n][l]
        for n in ("ffn1_pre_g", "ffn1_post_g", "mix_pre_g", "lru_conv_b", "sgu_ln_g", "sgu_ln_b", "mix_post_g", "ffn2_pre_g",
                  "ffn2_post_g"):
            w[n] = w[n].reshape(1, -1)
        w["sgu_w_s_t"] = jnp.swapaxes(w["sgu_w_s"], 1, 2).astype(BF16)
        w["sgu_w_s"] = w["sgu_w_s"].astype(BF16)
        w["sgu_b_t"] = w["sgu_b"].T
        layers.append(w)

    saved = []
    cur = xs
    for l in range(DEPTH):
        w = layers[l]
        sv = {}
        x1, g1, u1, f1 = _ffn_fwd(cur, w["ffn1_pre_g"], w["ffn1_w_gate"], w["ffn1_w_up"], w["ffn1_w_down"], w["ffn1_post_g"], t["ffn"])
        sv["ffn1"] = (cur, g1, u1, f1)
        x2, sv["mix"] = _mixer_fwd(x1, w, t)
        x3, g2, u2, f2 = _ffn_fwd(x2, w["ffn2_pre_g"], w["ffn2_w_gate"], w["ffn2_w_up"], w["ffn2_w_down"], w["ffn2_post_g"], t["ffn"])
        sv["ffn2"] = (x2, g2, u2, f2)
        saved.append(sv)
        cur = x3

    dy, sq = _loss_head(cur, loss_target[0], t["loss"])
    loss = lax.psum(0.5 * jnp.sum(sq) / D_MODEL, ("x", "y", "c"))

    grads = [None] * DEPTH
    for l in reversed(range(DEPTH)):
        w = layers[l]
        sv = saved[l]
        g = {}
        for tag in ("ffn2", "mix", "ffn1"):
            if tag == "mix":
                dy, gm = _mixer_bwd(dy, sv["mix"], w, t)
                g.update(gm)
                continue
            xin, gg, uu, ff = sv[tag]
            dy, hb, ab, dgb, dub, dfb, g[tag + "_pre_g"], g[tag + "_post_g"] = _ffn_bwd(
                dy, xin, gg, uu, ff, w[tag + "_pre_g"], w[tag + "_post_g"], w[tag + "_w_gate"], w[tag + "_w_up"],
                w[tag + "_w_down"], t["ffn_bwd"])
            g[tag + "_w_gate"], g[tag + "_w_up"], g[tag + "_w_down"] = _ffn_wgrads(hb, ab, dgb, dub, dfb, t["tn"])
        parts = [_shards_from_full(n, g[n]) for n, _ in BIG] + [_shards_from_full(n, g[n]) for n, _ in SMALL]
        parts += [jnp.broadcast_to(g[n].reshape((1,) + sh), (N_SHARD,) + sh) for n, sh in REPL]
        total = _reduce_scatter(_pack(parts, RS_ROWS, BF16), "_l%d" % l)
        un = _unpack(total[None], BIG + SMALL + REPL)
        grads[l] = {n: un[n][0] for n in WEIGHTS}
    grad_x = dy[None]

    gw, dw, nm, nv = [], [], [], []
    for n in WEIGHTS:
        gfull = jnp.stack([grads[l][n] for l in range(DEPTH)])
        d_, m_, v_ = _adamw(wsh[n], gfull, msh[n], vsh[n], n)
        gw.append(gfull)
        dw.append(d_)
        nm.append(m_)
        nv.append(v_)
    return (loss, grad_x, *gw, *dw, *nm, *nv)
```

```python
import functools

import jax
import jax.numpy as jnp
import numpy as np
from jax import lax
from jax.experimental import pallas as pl
from jax.experimental.pallas import tpu as pltpu

F32 = jnp.float32
BF16 = jnp.bfloat16
MESH = pl.DeviceIdType.MESH

D_MODEL = 1024
D_FF = 2816
N_SHARD = 4
F_SH = D_FF // N_SHARD
D_IN = 7680
D_IN_SH = D_IN // N_SHARD
ZA_W = 4608
ZM_W = 3072
LRU_W = 1024
LRU_HD = 256
SC_W = 512
SGU_W = 512
CHUNK = 128
DEPTH = 4
EPS = 1e-6
LRU_C = 8.0
LRU_RB = 16

ADAM_LR, ADAM_B1, ADAM_B2, ADAM_EPS, ADAM_WD, ADAM_STEP = 0.001, 0.9, 0.999, 1e-08, 0.01, 10

VMEM_LIMIT = 56 * 2 ** 20
PACK_C = 1024
AG_ROWS = 7168
RS_ROWS = 7296
RS_TILE = 192
SMALL_ROWS = 32


def _cp(sem=("arbitrary",)):
    return pltpu.CompilerParams(dimension_semantics=sem, vmem_limit_bytes=VMEM_LIMIT)


def _full(shape):
    return pl.BlockSpec(shape, lambda *_: (0,) * len(shape))


def _res(shape):
    return pl.BlockSpec(shape, lambda *_: (0,) * len(shape), pipeline_mode=pl.Buffered(1))


def _dot(a, b):
    return jnp.dot(a, b, preferred_element_type=F32)


def _dot_nt(a, b):
    return lax.dot_general(a, b, (((1,), (1,)), ((), ())), preferred_element_type=F32)


def _dot_tn(a, b):
    return lax.dot_general(a, b, (((0,), (0,)), ((), ())), preferred_element_type=F32)


def _sigmoid(x):
    return 0.5 * jnp.tanh(0.5 * x) + 0.5


_GELU_K = 0.7978845608028654
_GELU_C = 0.044715


def _gelu(x):
    return 0.5 * x * (1.0 + jnp.tanh(_GELU_K * (x + _GELU_C * x * x * x)))


def _gelu_and_grad(x):
    t = jnp.tanh(_GELU_K * (x + _GELU_C * x * x * x))
    g = 0.5 * x * (1.0 + t)
    dg = 0.5 * (1.0 + t) + 0.5 * x * (1.0 - t * t) * (_GELU_K * (1.0 + 3.0 * _GELU_C * x * x))
    return g, dg


def _rms_fwd(x, g):
    rs = lax.rsqrt(jnp.mean(x * x, axis=-1, keepdims=True) + EPS)
    return x * rs * g


def _rms_bwd(dy, x, g):
    rs = lax.rsqrt(jnp.mean(x * x, axis=-1, keepdims=True) + EPS)
    n = x * rs
    dn = dy * g
    dx = rs * (dn - n * jnp.mean(dn * n, axis=-1, keepdims=True))
    return dx, jnp.sum(dy * n, axis=0, keepdims=True)


def _acc(ref, val, first):
    @pl.when(first)
    def _():
        ref[...] = val

    @pl.when(jnp.logical_not(first))
    def _():
        ref[...] += val


def _shift(xm, d, prev, nxt, rows):
    tm = xm.shape[0]
    if d == 0:
        return xm
    y = pltpu.roll(xm, (-d) % tm, 0)
    rows8 = rows[0:8]
    if d < 0:
        hb = prev.shape[0]
        top = y[0:8]
        for r in range(-d):
            top = jnp.where(rows8 == r, prev[hb + r + d:hb + r + d + 1, :], top)
        return jnp.concatenate([top, y[8:]], axis=0)
    bot = y[tm - 8:]
    for r in range(d):
        bot = jnp.where(rows8 == 8 - d + r, nxt[r:r + 1, :], bot)
    return jnp.concatenate([y[:tm - 8], bot], axis=0)


def _halo(tm, hb, w, col, nt, rev=False):
    r = tm // hb
    last = nt * r - 1
    ti = (lambda i: nt - 1 - i) if rev else (lambda i: i)
    return [pl.BlockSpec((tm, w), lambda i: (ti(i), col)),
            pl.BlockSpec((hb, w), lambda i: (jnp.maximum(ti(i) * r - 1, 0), col)),
            pl.BlockSpec((hb, w), lambda i: (jnp.minimum((ti(i) + 1) * r, last), col))]


def _edges(prev_ref, next_ref, ti, nt):
    prev = jnp.where(ti > 0, prev_ref[...].astype(F32), 0.0)
    nxt = jnp.where(ti < nt - 1, next_ref[...].astype(F32), 0.0)
    return prev, nxt


def _ffn_fwd(x, pre_g, wg, wu, wd, post_g, tm):
    s = x.shape[0]
    nt = s // tm

    def body(x_ref, pg_ref, wg_ref, wu_ref, wd_ref, qg_ref, o_ref, g_ref, u_ref, f_ref):
        xv = x_ref[...]
        hb = _rms_fwd(xv, pg_ref[...]).astype(BF16)
        f = jnp.zeros((tm, D_MODEL), F32)
        for k in range(N_SHARD):
            g = _dot(hb, wg_ref[k])
            u = _dot(hb, wu_ref[k])
            g_ref[k] = g.astype(BF16)
            u_ref[k] = u.astype(BF16)
            a = (g * _sigmoid(g)) * u
            f = f + _dot(a.astype(BF16), wd_ref[k])
        f_ref[...] = f.astype(BF16)
        o_ref[...] = xv + 0.5 * _rms_fwd(f, qg_ref[...])

    row = pl.BlockSpec((tm, D_MODEL), lambda i: (i, 0))
    gu = pl.BlockSpec((N_SHARD, tm, F_SH), lambda i: (0, i, 0))
    return pl.pallas_call(
        body, name="ffn_fwd", grid=(nt,),
        in_specs=[row, _full((1, D_MODEL)), _res((N_SHARD, D_MODEL, F_SH)), _res((N_SHARD, D_MODEL, F_SH)),
                  _res((N_SHARD, F_SH, D_MODEL)), _full((1, D_MODEL))],
        out_specs=[row, gu, gu, row],
        out_shape=[jax.ShapeDtypeStruct((s, D_MODEL), F32), jax.ShapeDtypeStruct((N_SHARD, s, F_SH), BF16),
                   jax.ShapeDtypeStruct((N_SHARD, s, F_SH), BF16), jax.ShapeDtypeStruct((s, D_MODEL), BF16)],
        compiler_params=_cp(("parallel",)),
    )(x, pre_g, wg, wu, wd, post_g)


def _ffn_bwd(dy, x, g, u, f, pre_g, post_g, wg, wu, wd, tm):
    s = x.shape[0]
    nt = s // tm

    def body(dy_ref, x_ref, g_ref, u_ref, f_ref, pg_ref, qg_ref, wg_ref, wu_ref, wd_ref,
             dx_ref, h_ref, a_ref, dg_ref, du_ref, df_ref, dpg_ref, dqg_ref, dh_s, df_s):
        i = pl.program_id(0)
        k = pl.program_id(1)

        @pl.when(k == 0)
        def _():
            df, dq = _rms_bwd(0.5 * dy_ref[...], f_ref[...].astype(F32), qg_ref[...])
            dfb = df.astype(BF16)
            df_ref[...] = dfb
            df_s[...] = dfb
            h_ref[...] = _rms_fwd(x_ref[...], pg_ref[...]).astype(BF16)
            _acc(dqg_ref, dq, i == 0)

        gv = g_ref[...].astype(F32)
        uv = u_ref[...].astype(F32)
        sg = _sigmoid(gv)
        silu = gv * sg
        a_ref[...] = (silu * uv).astype(BF16)
        da = _dot_nt(df_s[...], wd_ref[k])
        dgb = (da * uv * (sg * (1.0 + gv * (1.0 - sg)))).astype(BF16)
        dub = (da * silu).astype(BF16)
        dg_ref[...] = dgb
        du_ref[...] = dub
        _acc(dh_s, _dot_nt(dgb, wg_ref[k]) + _dot_nt(dub, wu_ref[k]), k == 0)

        @pl.when(k == N_SHARD - 1)
        def _():
            dxn, dp = _rms_bwd(dh_s[...], x_ref[...], pg_ref[...])
            dx_ref[...] = dy_ref[...] + dxn
            _acc(dpg_ref, dp, i == 0)

    row = pl.BlockSpec((tm, D_MODEL), lambda i, k: (i, 0))
    gu = pl.BlockSpec((None, tm, F_SH), lambda i, k: (k, i, 0))
    vec = _full((1, D_MODEL))
    big = jax.ShapeDtypeStruct((N_SHARD, s, F_SH), BF16)
    return pl.pallas_call(
        body, name="ffn_bwd", grid=(nt, N_SHARD),
        in_specs=[row, row, gu, gu, row, vec, vec, _res((N_SHARD, D_MODEL, F_SH)), _res((N_SHARD, D_MODEL, F_SH)),
                  _res((N_SHARD, F_SH, D_MODEL))],
        out_specs=[row, row, gu, gu, gu, row, vec, vec],
        out_shape=[jax.ShapeDtypeStruct((s, D_MODEL), F32), jax.ShapeDtypeStruct((s, D_MODEL), BF16), big, big, big,
                   jax.ShapeDtypeStruct((s, D_MODEL), BF16), jax.ShapeDtypeStruct((1, D_MODEL), F32),
                   jax.ShapeDtypeStruct((1, D_MODEL), F32)],
        scratch_shapes=[pltpu.VMEM((tm, D_MODEL), F32), pltpu.VMEM((tm, D_MODEL), BF16)],
        compiler_params=_cp(("arbitrary", "arbitrary")),
    )(dy, x, g, u, f, pre_g, post_g, wg, wu, wd)


def _tn(lhs, rhs, name, nb, lhs_blk, lhs_map, rhs_blk, rhs_map, out_shape, out_blk, out_map, ts):
    s = lhs.shape[-2]
    ns = s // ts
    acc_shape = tuple(d for d in out_blk if d is not None)

    def body(l_ref, r_ref, o_ref, acc_s):
        t = pl.program_id(1)
        _acc(acc_s, _dot_tn(l_ref[...], r_ref[...]), t == 0)

        @pl.when(t == ns - 1)
        def _():
            o_ref[...] = acc_s[...].astype(BF16)

    return pl.pallas_call(
        body, name=name, grid=(nb, ns),
        in_specs=[pl.BlockSpec(lhs_blk, lhs_map), pl.BlockSpec(rhs_blk, rhs_map)],
        out_specs=pl.BlockSpec(out_blk, out_map),
        out_shape=jax.ShapeDtypeStruct(out_shape, BF16),
        scratch_shapes=[pltpu.VMEM(acc_shape, F32)],
        compiler_params=_cp(("parallel", "arbitrary")),
    )(lhs, rhs)


def _tn_plain(lhs, rhs, name, ts):
    m, n = lhs.shape[1], rhs.shape[1]
    return _tn(lhs, rhs, name, 1, (ts, m), lambda b, t: (t, 0), (ts, n), lambda b, t: (t, 0),
               (m, n), (m, n), lambda b, t: (0, 0), ts)


def _ffn_wgrads(h, a, dg, du, df, ts):
    s = h.shape[0]
    sh3 = (N_SHARD, D_MODEL, F_SH)
    lm2 = lambda b, t: (t, 0)
    bm3 = lambda b, t: (b, t, 0)
    om3 = lambda b, t: (b, 0, 0)
    d_wg = _tn(h, dg, "ffn_dwg", N_SHARD, (ts, D_MODEL), lm2, (None, ts, F_SH), bm3, sh3, (None, D_MODEL, F_SH), om3, ts)
    d_wu = _tn(h, du, "ffn_dwu", N_SHARD, (ts, D_MODEL), lm2, (None, ts, F_SH), bm3, sh3, (None, D_MODEL, F_SH), om3, ts)
    d_wd = _tn(a, df, "ffn_dwd", N_SHARD, (None, ts, F_SH), bm3, (ts, D_MODEL), lm2, (N_SHARD, F_SH, D_MODEL),
               (None, F_SH, D_MODEL), om3, ts)
    return d_wg, d_wu, d_wd


def _mix_in_fwd(x, pre_g, w_in, tm):
    s = x.shape[0]
    nt = s // tm
    cw = 512

    def body(x_ref, pg_ref, w_ref, h_ref, za_ref, zm_ref):
        hb = _rms_fwd(x_ref[...], pg_ref[...]).astype(BF16)
        h_ref[...] = hb
        for j in range(ZA_W // cw):
            za_ref[:, j * cw:(j + 1) * cw] = _dot(hb, w_ref[:, j * cw:(j + 1) * cw]).astype(BF16)
        for j in range(ZM_W // cw):
            zm_ref[:, j * cw:(j + 1) * cw] = _dot(hb, w_ref[:, ZA_W + j * cw:ZA_W + (j + 1) * cw]).astype(BF16)

    row = pl.BlockSpec((tm, D_MODEL), lambda i: (i, 0))
    return pl.pallas_call(
        body, name="mix_in_fwd", grid=(nt,),
        in_specs=[row, _full((1, D_MODEL)), _res((D_MODEL, D_IN))],
        out_specs=[row, pl.BlockSpec((tm, ZA_W), lambda i: (i, 0)), pl.BlockSpec((tm, ZM_W), lambda i: (i, 0))],
        out_shape=[jax.ShapeDtypeStruct((s, D_MODEL), BF16), jax.ShapeDtypeStruct((s, ZA_W), BF16),
                   jax.ShapeDtypeStruct((s, ZM_W), BF16)],
        compiler_params=_cp(("parallel",)),
    )(x, pre_g, w_in)


def _lru_conv(xm, prev, nxt, rows, cw_ref, cb_ref):
    acc = cb_ref[...] + cw_ref[2:3, :] * xm
    acc = acc + cw_ref[0:1, :] * _shift(xm, -2, prev, nxt, rows)
    acc = acc + cw_ref[1:2, :] * _shift(xm, -1, prev, nxt, rows)
    acc = acc + cw_ref[3:4, :] * _shift(xm, 1, prev, nxt, rows)
    return acc


def _lru_preact(x_ref, xp_ref, xn_ref, cw_ref, cb_ref, wa_ref, ba_ref, wx_ref, bx_ref, lam_ref, ti, nt, tm,
                xc_s, ra_s, xa_s, c8_s):
    rows = lax.broadcasted_iota(jnp.int32, (tm, 1), 0)
    prev, nxt = _edges(xp_ref, xn_ref, ti, nt)
    xc = _lru_conv(x_ref[...].astype(F32), prev, nxt, rows, cw_ref, cb_ref)
    xc_s[...] = xc
    xcb = xc.astype(BF16)
    for h in range(4):
        cs_ = slice(LRU_HD * h, LRU_HD * (h + 1))
        ra_s[:, cs_] = _dot(xcb[:, cs_], wa_ref[h]) + ba_ref[:, cs_]
        xa_s[:, cs_] = _dot(xcb[:, cs_], wx_ref[h]) + bx_ref[:, cs_]
    lam = lam_ref[...]
    e = jnp.exp(-jnp.abs(lam))
    log1p_e = jnp.where(e < 1e-2, e * (1.0 - e * (0.5 - e * (1.0 / 3.0))), jnp.log(1.0 + e))
    c8_s[...] = jnp.broadcast_to(-LRU_C * (jnp.maximum(-lam, 0.0) + log1p_e), c8_s.shape)
    return xcb


def _lru_decay(cl, r):
    la = cl * r
    a = jnp.exp(la)
    y2 = 2.0 * la
    em = jnp.where(y2 > -0.004, y2 * (-1.0 + y2 * (-0.5 - y2 * (1.0 / 6.0))), 1.0 - a * a)
    return a, jnp.sqrt(em)


def _scan_rows(a_s, b_s, o_s, carry, tm, descending):
    nb = tm // 8

    def blk(j, h):
        jb = (nb - 1 - j) if descending else j
        base = pl.multiple_of(jb * 8, 8)
        for r in range(8):
            t = base + ((7 - r) if descending else r)
            h = a_s[pl.ds(t, 1), :] * h + b_s[pl.ds(t, 1), :]
            o_s[pl.ds(t, 1), :] = h
        return h

    return lax.fori_loop(0, nb, blk, carry)


def _lru_fwd(za, conv_w, conv_b, wa, ba, wx, bx, lam, rev, tm):
    s = za.shape[0]
    nt = s // tm

    def body(x_ref, xp_ref, xn_ref, cw_ref, cb_ref, wa_ref, ba_ref, wx_ref, bx_ref, lam_ref, h_ref,
             xc_s, ra_s, xa_s, o_s, c8_s, c_s):
        i = pl.program_id(0)
        ti = (nt - 1 - i) if rev else i
        _lru_preact(x_ref, xp_ref, xn_ref, cw_ref, cb_ref, wa_ref, ba_ref, wx_ref, bx_ref, lam_ref, ti, nt, tm,
                    xc_s, ra_s, xa_s, c8_s)

        def gate_blk(j, carry):
            rws = pl.ds(pl.multiple_of(j * LRU_RB, LRU_RB), LRU_RB)
            r = _sigmoid(ra_s[rws, :])
            ig = _sigmoid(xa_s[rws, :])
            a, mult = _lru_decay(c8_s[...], r)
            ra_s[rws, :] = a
            xa_s[rws, :] = ig * xc_s[rws, :] * mult
            return carry

        lax.fori_loop(0, tm // LRU_RB, gate_blk, 0)

        @pl.when(i == 0)
        def _():
            c_s[...] = jnp.zeros_like(c_s)

        c_s[...] = _scan_rows(ra_s, xa_s, o_s, c_s[...], tm, rev)
        h_ref[...] = o_s[...].astype(BF16)

    vec = _full((1, LRU_W))
    hd = _full((4, LRU_HD, LRU_HD))
    ti = (lambda i: nt - 1 - i) if rev else (lambda i: i)
    tile = pltpu.VMEM((tm, LRU_W), F32)
    return pl.pallas_call(
        body, name="lru_fwd_rev" if rev else "lru_fwd", grid=(nt,),
        in_specs=_halo(tm, 16, LRU_W, 1, nt, rev) + [_full((4, LRU_W)), vec, hd, vec, hd, vec, vec],
        out_specs=pl.BlockSpec((tm, LRU_W), lambda i: (ti(i), 0)),
        out_shape=jax.ShapeDtypeStruct((s, LRU_W), BF16),
        scratch_shapes=[tile, tile, tile, tile, pltpu.VMEM((LRU_RB, LRU_W), F32), pltpu.VMEM((1, LRU_W), F32)],
        compiler_params=_cp(),
    )(za, za, za, conv_w, conv_b, wa, ba, wx, bx, lam)


def _sc_conv(c_ref, cp_ref, cn_ref, x_ref, xp_ref, xn_ref, w_ref, ti, nt, rows):
    cprev, cnext = _edges(cp_ref, cn_ref, ti, nt)
    xprev, xnext = _edges(xp_ref, xn_ref, ti, nt)
    cv = c_ref[...].astype(F32)
    xv = x_ref[...].astype(F32)
    p = cv * xv
    pm1 = _shift(p, -1, cprev * xprev, cnext * xnext, rows)
    pp1 = _shift(p, 1, cprev * xprev, cnext * xnext, rows)
    conv = w_ref[0:1, :] * pm1 + w_ref[1:2, :] * p + w_ref[2:3, :] * pp1
    return cv, xv, p, pm1, pp1, conv


def _sgu_norm(vz, g_ref, b_ref):
    vg, dvg = _gelu_and_grad(vz)
    mu = jnp.mean(vg, axis=-1, keepdims=True)
    cen = vg - mu
    rs = lax.rsqrt(jnp.mean(cen * cen, axis=-1, keepdims=True) + EPS)
    vn = cen * rs
    return dvg, vn, rs, vn * g_ref[...] + b_ref[...]


def _mix_out_fwd(x, za, zm, hf, hr, lru_w_out, sc_conv_w, sc_w_out, ln_g, ln_b, w_s, b_s_t, sgu_w_out, w_o, post_g, tm):
    s = x.shape[0]
    nt = s // tm
    nc = tm // CHUNK

    def body(x_ref, gate_ref, scb_ref, scc_ref, sccp_ref, sccn_ref, scx_ref, scxp_ref, scxn_ref, su_ref, sv_ref,
             zm_ref, hf_ref, hr_ref, wlo_ref, scw_ref, wso_ref, lg_ref, lb_ref, ws_ref, bs_ref, wgo_ref, wo_ref,
             qg_ref, o_ref, yain_ref, q_ref, ycin_ref, ya_ref, yb_ref, yc_ref, m_ref, mx_ref, mixed_s):
        ti = pl.program_id(0)
        rows = lax.broadcasted_iota(jnp.int32, (tm, 1), 0)
        hs = hf_ref[...].astype(F32) + hr_ref[...].astype(F32)
        yain = (hs * _gelu(gate_ref[...].astype(F32))).astype(BF16)
        yain_ref[...] = yain
        ya = _dot(yain, wlo_ref[...])
        _, _, _, _, _, conv = _sc_conv(scc_ref, sccp_ref, sccn_ref, scx_ref, scxp_ref, scxn_ref, scw_ref, ti, nt, rows)
        qb = (scb_ref[...].astype(F32) * conv).astype(BF16)
        q_ref[...] = qb
        yb = _dot(qb, wso_ref[...])
        _, _, _, v = _sgu_norm(sv_ref[...].astype(F32), lg_ref, lb_ref)
        vb = v.astype(BF16)
        for n in range(nc):
            for g in range(4):
                blk = vb[n * CHUNK:(n + 1) * CHUNK, g * CHUNK:(g + 1) * CHUNK]
                mixed_s[n * CHUNK:(n + 1) * CHUNK, g * CHUNK:(g + 1) * CHUNK] = _dot(ws_ref[g], blk) + bs_ref[:, g:g + 1]
        ycin = (_gelu(su_ref[...].astype(F32)) * mixed_s[...]).astype(BF16)
        ycin_ref[...] = ycin
        yc = _dot(ycin, wgo_ref[...])
        m = (_sigmoid(zm_ref[:, 0:D_MODEL].astype(F32)) * ya + _sigmoid(zm_ref[:, D_MODEL:2 * D_MODEL].astype(F32)) * yb
             + _sigmoid(zm_ref[:, 2 * D_MODEL:3 * D_MODEL].astype(F32)) * yc)
        mb = m.astype(BF16)
        mx = _dot(mb, wo_ref[...])
        ya_ref[...] = ya.astype(BF16)
        yb_ref[...] = yb.astype(BF16)
        yc_ref[...] = yc.astype(BF16)
        m_ref[...] = mb
        mx_ref[...] = mx.astype(BF16)
        o_ref[...] = x_ref[...] + _rms_fwd(mx, qg_ref[...])

    row = pl.BlockSpec((tm, D_MODEL), lambda i: (i, 0))
    half = pl.BlockSpec((tm, 512), lambda i: (i, 0))
    col = lambda c: pl.BlockSpec((tm, 512), lambda i: (i, c))
    in_specs = ([row, pl.BlockSpec((tm, LRU_W), lambda i: (i, 0)), col(4)] + _halo(tm, 16, SC_W, 5, nt) + _halo(tm, 16, SC_W, 6, nt)
                + [col(7), col(8), pl.BlockSpec((tm, ZM_W), lambda i: (i, 0)), row, row,
                   _full((LRU_W, D_MODEL)), _full((3, SC_W)), _full((SC_W, D_MODEL)), _full((1, SGU_W)), _full((1, SGU_W)),
                   _full((4, CHUNK, CHUNK)), _full((CHUNK, 4)), _full((SGU_W, D_MODEL)), _full((D_MODEL, D_MODEL)),
                   _full((1, D_MODEL))])
    bf = lambda w: jax.ShapeDtypeStruct((s, w), BF16)
    return pl.pallas_call(
        body, name="mix_out_fwd", grid=(nt,),
        in_specs=in_specs,
        out_specs=[row, row, half, half, row, row, row, row, row],
        out_shape=[jax.ShapeDtypeStruct((s, D_MODEL), F32), bf(LRU_W), bf(SC_W), bf(SGU_W), bf(D_MODEL), bf(D_MODEL),
                   bf(D_MODEL), bf(D_MODEL), bf(D_MODEL)],
        scratch_shapes=[pltpu.VMEM((tm, SGU_W), F32)],
        compiler_params=_cp(("parallel",)),
    )(x, za, za, za, za, za, za, za, za, za, za, zm, hf, hr, lru_w_out, sc_conv_w, sc_w_out, ln_g, ln_b, w_s, b_s_t,
      sgu_w_out, w_o, post_g)


def _mix_out_bwd(dy, za, zm, hf, hr, ya, yb, yc, mx, lru_w_out, sc_conv_w, sc_w_out, ln_g, ln_b, w_s, w_s_t, b_s_t,
                 sgu_w_out, w_o, post_g, tm):
    s = dy.shape[0]
    nt = s // tm
    nc = tm // CHUNK

    def body(dy_ref, gate_ref, scb_ref, scc_ref, sccp_ref, sccn_ref, scx_ref, scxp_ref, scxn_ref, su_ref, sv_ref,
             zm_ref, hf_ref, hr_ref, ya_ref, yb_ref, yc_ref, mx_ref, wlo_ref, scw_ref, wso_ref, lg_ref, lb_ref,
             ws_ref, wst_ref, bs_ref, wgo_ref, wo_ref, qg_ref,
             dmx_ref, dya_ref, dyb_ref, dyc_ref, dgate_ref, dscb_ref, dsu_ref, dsv_ref, dzm_ref, dhs_ref, dcp_ref,
             dqg_ref, dlg_ref, dlb_ref, dws_ref, dbs_ref, mixed_s, dv_s):
        ti = pl.program_id(0)
        first = ti == 0
        rows = lax.broadcasted_iota(jnp.int32, (tm, 1), 0)
        dmx, dq = _rms_bwd(dy_ref[...], mx_ref[...].astype(F32), qg_ref[...])
        _acc(dqg_ref, dq, first)
        dmxb = dmx.astype(BF16)
        dmx_ref[...] = dmxb
        dm = _dot_nt(dmxb, wo_ref[...])
        dys = []
        for k, (y_ref, d_ref) in enumerate(((ya_ref, dya_ref), (yb_ref, dyb_ref), (yc_ref, dyc_ref))):
            gk = _sigmoid(zm_ref[:, k * D_MODEL:(k + 1) * D_MODEL].astype(F32))
            dyk = (dm * gk).astype(BF16)
            d_ref[...] = dyk
            dys.append(dyk)
            dzm_ref[:, k * D_MODEL:(k + 1) * D_MODEL] = (dm * y_ref[...].astype(F32) * gk * (1.0 - gk)).astype(BF16)
        dyain = _dot_nt(dys[0], wlo_ref[...])
        gg, dgg = _gelu_and_grad(gate_ref[...].astype(F32))
        hs = hf_ref[...].astype(F32) + hr_ref[...].astype(F32)
        dhs_ref[...] = dyain * gg
        dgate_ref[...] = (dyain * hs * dgg).astype(BF16)
        dq_b = _dot_nt(dys[1], wso_ref[...])
        _, _, _, _, _, conv = _sc_conv(scc_ref, sccp_ref, sccn_ref, scx_ref, scxp_ref, scxn_ref, scw_ref, ti, nt, rows)
        dscb_ref[...] = (dq_b * conv).astype(BF16)
        dcp_ref[...] = dq_b * scb_ref[...].astype(F32)
        dycin = _dot_nt(dys[2], wgo_ref[...])
        dvg, vn, rs, v = _sgu_norm(sv_ref[...].astype(F32), lg_ref, lb_ref)
        vb = v.astype(BF16)
        ug, dug = _gelu_and_grad(su_ref[...].astype(F32))
        dmixed = dycin * ug
        dmb = dmixed.astype(BF16)
        dws = [jnp.zeros((CHUNK, CHUNK), F32) for _ in range(4)]
        dbs = [jnp.zeros((CHUNK, CHUNK), F32) for _ in range(4)]
        for n in range(nc):
            for g in range(4):
                rs_, cs_ = slice(n * CHUNK, (n + 1) * CHUNK), slice(g * CHUNK, (g + 1) * CHUNK)
                mixed_s[rs_, cs_] = _dot(ws_ref[g], vb[rs_, cs_]) + bs_ref[:, g:g + 1]
                dv_s[rs_, cs_] = _dot(wst_ref[g], dmb[rs_, cs_])
                dws[g] = dws[g] + _dot_nt(dmb[rs_, cs_], vb[rs_, cs_])
                dbs[g] = dbs[g] + dmixed[rs_, cs_]
        for g in range(4):
            _acc(dws_ref.at[g], dws[g], first)
            _acc(dbs_ref.at[g], dbs[g], first)
        dsu_ref[...] = (dycin * mixed_s[...] * dug).astype(BF16)
        dv = dv_s[...]
        _acc(dlg_ref, jnp.sum(dv * vn, axis=0, keepdims=True), first)
        _acc(dlb_ref, jnp.sum(dv, axis=0, keepdims=True), first)
        dvn = dv * lg_ref[...]
        dcen = rs * (dvn - jnp.mean(dvn, axis=-1, keepdims=True) - vn * jnp.mean(dvn * vn, axis=-1, keepdims=True))
        dsv_ref[...] = (dcen * dvg).astype(BF16)

    row = pl.BlockSpec((tm, D_MODEL), lambda i: (i, 0))
    half = pl.BlockSpec((tm, 512), lambda i: (i, 0))
    col = lambda c: pl.BlockSpec((tm, 512), lambda i: (i, c))
    zmrow = pl.BlockSpec((tm, ZM_W), lambda i: (i, 0))
    sq = _full((4, CHUNK, CHUNK))
    in_specs = ([row, pl.BlockSpec((tm, LRU_W), lambda i: (i, 0)), col(4)] + _halo(tm, 16, SC_W, 5, nt) + _halo(tm, 16, SC_W, 6, nt)
                + [col(7), col(8), zmrow, row, row, row, row, row, row,
                   _full((LRU_W, D_MODEL)), _full((3, SC_W)), _full((SC_W, D_MODEL)), _full((1, SGU_W)), _full((1, SGU_W)),
                   sq, sq, _full((CHUNK, 4)), _full((SGU_W, D_MODEL)), _full((D_MODEL, D_MODEL)), _full((1, D_MODEL))])
    bf = lambda w: jax.ShapeDtypeStruct((s, w), BF16)
    return pl.pallas_call(
        body, name="mix_out_bwd", grid=(nt,),
        in_specs=in_specs,
        out_specs=[row, row, row, row, row, half, half, half, zmrow, row, half,
                   _full((1, D_MODEL)), _full((1, SGU_W)), _full((1, SGU_W)), sq, sq],
        out_shape=[bf(D_MODEL), bf(D_MODEL), bf(D_MODEL), bf(D_MODEL), bf(LRU_W), bf(SC_W), bf(SGU_W), bf(SGU_W), bf(ZM_W),
                   jax.ShapeDtypeStruct((s, LRU_W), F32), jax.ShapeDtypeStruct((s, SC_W), F32),
                   jax.ShapeDtypeStruct((1, D_MODEL), F32), jax.ShapeDtypeStruct((1, SGU_W), F32),
                   jax.ShapeDtypeStruct((1, SGU_W), F32), jax.ShapeDtypeStruct((4, CHUNK, CHUNK), F32),
                   jax.ShapeDtypeStruct((4, CHUNK, CHUNK), F32)],
        scratch_shapes=[pltpu.VMEM((tm, SGU_W), F32), pltpu.VMEM((tm, SGU_W), F32)],
        compiler_params=_cp(),
    )(dy, za, za, za, za, za, za, za, za, za, za, zm, hf, hr, ya, yb, yc, mx, lru_w_out, sc_conv_w, sc_w_out, ln_g, ln_b,
      w_s, w_s_t, b_s_t, sgu_w_out, w_o, post_g)


def _lru_bwd(za, h_dir, dhs, conv_w, conv_b, wa, ba, wx, bx, lam, rev, tm):
    s = za.shape[0]
    nt = s // tm
    back = not rev

    def body(x_ref, xp_ref, xn_ref, h_ref, hp_ref, hn_ref, dh_ref, cw_ref, cb_ref, wa_ref, ba_ref, wx_ref, bx_ref,
             lam_ref, dxc_ref, dwa_ref, dba_ref, dwx_ref, dbx_ref, dlam_ref,
             xc_s, ra_s, xa_s, a_s, m_s, l_s, hsh_s, c8_s, c_s):
        i = pl.program_id(0)
        first = i == 0
        ti = (nt - 1 - i) if back else i
        rows = lax.broadcasted_iota(jnp.int32, (tm, 1), 0)
        xcb = _lru_preact(x_ref, xp_ref, xn_ref, cw_ref, cb_ref, wa_ref, ba_ref, wx_ref, bx_ref, lam_ref, ti, nt, tm,
                          xc_s, ra_s, xa_s, c8_s)
        hprev, hnext = _edges(hp_ref, hn_ref, ti, nt)
        hsh_s[...] = _shift(h_ref[...].astype(F32), 1 if rev else -1, hprev, hnext, rows)

        def gate_blk(j, carry):
            rws = pl.ds(pl.multiple_of(j * LRU_RB, LRU_RB), LRU_RB)
            r = _sigmoid(ra_s[rws, :])
            a, mult = _lru_decay(c8_s[...], r)
            ra_s[rws, :] = r
            xa_s[rws, :] = _sigmoid(xa_s[rws, :])
            a_s[rws, :] = a
            m_s[rws, :] = mult
            return carry

        lax.fori_loop(0, tm // LRU_RB, gate_blk, 0)

        @pl.when(first)
        def _():
            c_s[...] = jnp.zeros_like(c_s)

        nb = tm // 8

        def blk(j, c):
            jb = (nb - 1 - j) if back else j
            base = pl.multiple_of(jb * 8, 8)
            for q in range(8):
                t = base + ((7 - q) if back else q)
                lt = dh_ref[pl.ds(t, 1), :] + c
                l_s[pl.ds(t, 1), :] = lt
                c = a_s[pl.ds(t, 1), :] * lt
            return c

        c_s[...] = lax.fori_loop(0, nb, blk, c_s[...])

        def grad_blk(j, sums):
            s_lam, s_ba, s_bx = sums
            rws = pl.ds(pl.multiple_of(j * LRU_RB, LRU_RB), LRU_RB)
            du = l_s[rws, :]
            a = a_s[rws, :]
            r = ra_s[rws, :]
            ig = xa_s[rws, :]
            mult = m_s[rws, :]
            xc = xc_s[rws, :]
            t1 = du * mult
            dla = du * hsh_s[rws, :] * a - (du * ig * xc) * (a * a) / mult
            dlr = dla * r
            drp = dlr * c8_s[...] * (1.0 - r)
            dip = (t1 * xc) * ig * (1.0 - ig)
            l_s[rws, :] = t1 * ig
            ra_s[rws, :] = drp
            xa_s[rws, :] = dip
            fold = lambda v: sum(v[8 * q:8 * q + 8] for q in range(1, LRU_RB // 8)) + v[0:8]
            return s_lam + fold(dlr), s_ba + fold(drp), s_bx + fold(dip)

        zero8 = jnp.zeros((8, LRU_W), F32)
        s_lam, s_ba, s_bx = lax.fori_loop(0, tm // LRU_RB, grad_blk, (zero8, zero8, zero8))
        _acc(dlam_ref, jnp.sum(s_lam, axis=0, keepdims=True), first)
        _acc(dba_ref, jnp.sum(s_ba, axis=0, keepdims=True), first)
        _acc(dbx_ref, jnp.sum(s_bx, axis=0, keepdims=True), first)
        drb = ra_s[...].astype(BF16)
        dib = xa_s[...].astype(BF16)
        for h in range(4):
            cs_ = slice(LRU_HD * h, LRU_HD * (h + 1))
            dxc_ref[:, cs_] = l_s[:, cs_] + _dot_nt(drb[:, cs_], wa_ref[h]) + _dot_nt(dib[:, cs_], wx_ref[h])
            _acc(dwa_ref.at[h], _dot_tn(xcb[:, cs_], drb[:, cs_]), first)
            _acc(dwx_ref.at[h], _dot_tn(xcb[:, cs_], dib[:, cs_]), first)

        @pl.when(i == nt - 1)
        def _():
            dlam_ref[...] = dlam_ref[...] * (LRU_C * _sigmoid(-lam_ref[...]))

    vec = _full((1, LRU_W))
    hd = _full((4, LRU_HD, LRU_HD))
    tix = (lambda i: nt - 1 - i) if back else (lambda i: i)
    rowspec = pl.BlockSpec((tm, LRU_W), lambda i: (tix(i), 0))
    return pl.pallas_call(
        body, name="lru_bwd_rev" if rev else "lru_bwd", grid=(nt,),
        in_specs=_halo(tm, 16, LRU_W, 1, nt, back) + _halo(tm, 16, LRU_W, 0, nt, back) + [rowspec, _full((4, LRU_W)), vec, hd, vec, hd, vec, vec],
        out_specs=[rowspec, hd, vec, hd, vec, vec],
        out_shape=[jax.ShapeDtypeStruct((s, LRU_W), F32), jax.ShapeDtypeStruct((4, LRU_HD, LRU_HD), F32),
                   jax.ShapeDtypeStruct((1, LRU_W), F32), jax.ShapeDtypeStruct((4, LRU_HD, LRU_HD), F32),
                   jax.ShapeDtypeStruct((1, LRU_W), F32), jax.ShapeDtypeStruct((1, LRU_W), F32)],
        scratch_shapes=[pltpu.VMEM((tm, LRU_W), F32)] * 7 + [pltpu.VMEM((LRU_RB, LRU_W), F32), pltpu.VMEM((1, LRU_W), F32)],
        compiler_params=_cp(),
    )(za, za, za, h_dir, h_dir, h_dir, dhs, conv_w, conv_b, wa, ba, wx, bx, lam)


def _mix_in_bwd(dy, x, za, dxc_f, dxc_r, dcp, dgate, dscb, dsu, dsv, dzm, pre_g, lru_conv_w, sc_conv_w, w_in, tm):
    s = x.shape[0]
    nt = s // tm
    cw = 512

    def body(dy_ref, x_ref, lx_ref, lxp_ref, lxn_ref, scc_ref, sccp_ref, sccn_ref, scx_ref, scxp_ref, scxn_ref,
             df_ref, dfp_ref, dfn_ref, dr_ref, drp_ref, drn_ref, dcp_ref, dcpp_ref, dcpn_ref,
             dgate_ref, dscb_ref, dsu_ref, dsv_ref, dzm_ref, pg_ref, lcw_ref, scw_ref, w_ref,
             dx_ref, dz_ref, dpg_ref, dlcw_ref, dlcb_ref, dscw_ref):
        ti = pl.program_id(0)
        first = ti == 0
        rows = lax.broadcasted_iota(jnp.int32, (tm, 1), 0)
        fp, fn = _edges(dfp_ref, dfn_ref, ti, nt)
        rp, rn = _edges(drp_ref, drn_ref, ti, nt)
        dxc = df_ref[...] + dr_ref[...]
        dprev, dnext = fp + rp, fn + rn
        dlx = lcw_ref[2:3, :] * dxc
        dlx = dlx + lcw_ref[0:1, :] * _shift(dxc, 2, dprev, dnext, rows)
        dlx = dlx + lcw_ref[1:2, :] * _shift(dxc, 1, dprev, dnext, rows)
        dlx = dlx + lcw_ref[3:4, :] * _shift(dxc, -1, dprev, dnext, rows)
        lprev, lnext = _edges(lxp_ref, lxn_ref, ti, nt)
        lx = lx_ref[...].astype(F32)
        _acc(dlcb_ref, jnp.sum(dxc, axis=0, keepdims=True), first)
        for k, d in enumerate((-2, -1, 0, 1)):
            _acc(dlcw_ref.at[pl.ds(k, 1), :], jnp.sum(dxc * _shift(lx, d, lprev, lnext, rows), axis=0, keepdims=True), first)
        cv, xv, p, pm1, pp1, _ = _sc_conv(scc_ref, sccp_ref, sccn_ref, scx_ref, scxp_ref, scxn_ref, scw_ref, ti, nt, rows)
        cprev, cnext = _edges(dcpp_ref, dcpn_ref, ti, nt)
        dcp_v = dcp_ref[...]
        dp = (scw_ref[1:2, :] * dcp_v + scw_ref[0:1, :] * _shift(dcp_v, 1, cprev, cnext, rows)
              + scw_ref[2:3, :] * _shift(dcp_v, -1, cprev, cnext, rows))
        for k, pk in enumerate((pm1, p, pp1)):
            _acc(dscw_ref.at[pl.ds(k, 1), :], jnp.sum(dcp_v * pk, axis=0, keepdims=True), first)
        dz_ref[:, 0:1024] = dgate_ref[...]
        dz_ref[:, 1024:2048] = dlx.astype(BF16)
        dz_ref[:, 2048:2560] = dscb_ref[...]
        dz_ref[:, 2560:3072] = (dp * xv).astype(BF16)
        dz_ref[:, 3072:3584] = (dp * cv).astype(BF16)
        dz_ref[:, 3584:4096] = dsu_ref[...]
        dz_ref[:, 4096:4608] = dsv_ref[...]
        dz_ref[:, 4608:7680] = dzm_ref[...]
        dh = _dot_nt(dz_ref[...], w_ref[...])
        dxn, dpg = _rms_bwd(dh, x_ref[...], pg_ref[...])
        dx_ref[...] = dy_ref[...] + dxn
        _acc(dpg_ref, dpg, first)

    row = pl.BlockSpec((tm, D_MODEL), lambda i: (i, 0))
    half = pl.BlockSpec((tm, 512), lambda i: (i, 0))
    in_specs = ([row, row] + _halo(tm, 16, LRU_W, 1, nt) + _halo(tm, 16, SC_W, 5, nt) + _halo(tm, 16, SC_W, 6, nt)
                + _halo(tm, 8, LRU_W, 0, nt) + _halo(tm, 8, LRU_W, 0, nt) + _halo(tm, 8, SC_W, 0, nt)
                + [row, half, half, half, pl.BlockSpec((tm, ZM_W), lambda i: (i, 0)),
                   _full((1, D_MODEL)), _full((4, LRU_W)), _full((3, SC_W)), _res((D_MODEL, D_IN))])
    return pl.pallas_call(
        body, name="mix_in_bwd", grid=(nt,),
        in_specs=in_specs,
        out_specs=[row, pl.BlockSpec((tm, D_IN), lambda i: (i, 0)), _full((1, D_MODEL)), _full((4, LRU_W)),
                   _full((1, LRU_W)), _full((3, SC_W))],
        out_shape=[jax.ShapeDtypeStruct((s, D_MODEL), F32), jax.ShapeDtypeStruct((s, D_IN), BF16),
                   jax.ShapeDtypeStruct((1, D_MODEL), F32), jax.ShapeDtypeStruct((4, LRU_W), F32),
                   jax.ShapeDtypeStruct((1, LRU_W), F32), jax.ShapeDtypeStruct((3, SC_W), F32)],
        compiler_params=_cp(),
    )(dy, x, za, za, za, za, za, za, za, za, za, dxc_f, dxc_f, dxc_f, dxc_r, dxc_r, dxc_r, dcp, dcp, dcp,
      dgate, dscb, dsu, dsv, dzm, pre_g, lru_conv_w, sc_conv_w, w_in)


def _loss_head(y, target, tm):
    s = y.shape[0]
    nt = s // tm

    def body(y_ref, t_ref, dy_ref, acc_ref):
        err = y_ref[...] - t_ref[...]
        dy_ref[...] = err * (1.0 / D_MODEL)
        _acc(acc_ref, jnp.sum(err * err, axis=0, keepdims=True), pl.program_id(0) == 0)

    row = pl.BlockSpec((tm, D_MODEL), lambda i: (i, 0))
    return pl.pallas_call(
        body, name="loss_head", grid=(nt,), in_specs=[row, row], out_specs=[row, _full((1, D_MODEL))],
        out_shape=[jax.ShapeDtypeStruct((s, D_MODEL), F32), jax.ShapeDtypeStruct((1, D_MODEL), F32)],
        compiler_params=_cp(),
    )(y, target)


def _row_tile(rows, cols):
    cap = max(8, (2 ** 18) // cols)
    best = None
    for t in range(8, min(rows, cap) + 1, 8):
        if rows % t == 0:
            best = t
    return best if best is not None else rows


def _adamw(w, g, m, v, name):
    shape = w.shape
    cols = shape[-1]
    rows = int(np.prod(shape[:-1]))
    tr = _row_tile(rows, cols)
    bc1 = 1.0 - ADAM_B1 ** ADAM_STEP
    bc2 = 1.0 - ADAM_B2 ** ADAM_STEP

    def body(w_ref, g_ref, m_ref, v_ref, d_ref, nm_ref, nv_ref):
        gv = g_ref[...]
        mn = ADAM_B1 * m_ref[...] + (1.0 - ADAM_B1) * gv
        vn = ADAM_B2 * v_ref[...] + (1.0 - ADAM_B2) * (gv * gv)
        nm_ref[...] = mn
        nv_ref[...] = vn
        d_ref[...] = -ADAM_LR * ((mn / bc1) / (jnp.sqrt(vn / bc2) + ADAM_EPS) + ADAM_WD * w_ref[...])

    spec = pl.BlockSpec((tr, cols), lambda i: (i, 0))
    sds = jax.ShapeDtypeStruct((rows, cols), F32)
    outs = pl.pallas_call(
        body, name="adamw_" + name, grid=(rows // tr,), in_specs=[spec] * 4, out_specs=[spec] * 3,
        out_shape=[sds, sds, sds], compiler_params=_cp(("parallel",)),
    )(w.reshape(rows, cols), g.reshape(rows, cols), m.reshape(rows, cols), v.reshape(rows, cols))
    return [o.reshape(shape) for o in outs]


HBM = pl.BlockSpec(memory_space=pl.ANY)


def _place():
    x, y, c = lax.axis_index("x"), lax.axis_index("y"), lax.axis_index("c")
    chips = [(1 - x, y), (x, 1 - y), (1 - x, 1 - y)]
    return x, y, c, chips


def _all_gather(buf, name):
    r, cdim = buf.shape
    r2 = r // 2

    def body(b_ref, o_ref, ssem, rsem):
        x, y, c, chips = _place()
        me = 2 * x + y
        sib = (x, y, 1 - c)
        mine = pl.ds(pl.multiple_of(c * r2, 16), r2)
        other = pl.ds(pl.multiple_of((1 - c) * r2, 16), r2)

        def rc(k, src, dst, to):
            return pltpu.make_async_remote_copy(src_ref=src, dst_ref=dst, send_sem=ssem.at[k], recv_sem=rsem.at[k],
                                                device_id=to, device_id_type=MESH)

        sent = []
        for j, (cx, cy) in enumerate(chips):
            cp = rc(j, b_ref.at[mine], o_ref.at[me, mine], (cx, cy, c))
            cp.start()
            sent.append(cp)
        for j, (cx, cy) in enumerate(chips):
            blk = o_ref.at[2 * cx + cy, mine]
            rc(j, blk, blk, (cx, cy, c)).wait_recv()
            cp = rc(3 + j, blk, blk, sib)
            cp.start()
            sent.append(cp)
        for j, (cx, cy) in enumerate(chips):
            blk = o_ref.at[2 * cx + cy, other]
            rc(3 + j, blk, blk, sib).wait_recv()
        for cp in sent:
            cp.wait_send()

    got = pl.pallas_call(
        body, name=name, in_specs=[HBM], out_specs=HBM,
        out_shape=jax.ShapeDtypeStruct((N_SHARD, r, cdim), buf.dtype),
        scratch_shapes=[pltpu.SemaphoreType.DMA((6,)), pltpu.SemaphoreType.DMA((6,))],
    )(buf)
    me = 2 * lax.axis_index("x") + lax.axis_index("y")
    return lax.dynamic_update_slice(got, buf[None], (me, 0, 0))


def _pair_swap(send, name):
    def body(s_ref, r_ref, ssem, rsem):
        x, y, c, _ = _place()
        cp = pltpu.make_async_remote_copy(src_ref=s_ref, dst_ref=r_ref, send_sem=ssem, recv_sem=rsem,
                                          device_id=(x, y, 1 - c), device_id_type=MESH)
        cp.start()
        cp.wait()

    return pl.pallas_call(
        body, name=name, in_specs=[HBM], out_specs=HBM, out_shape=jax.ShapeDtypeStruct(send.shape, send.dtype),
        scratch_shapes=[pltpu.SemaphoreType.DMA, pltpu.SemaphoreType.DMA],
    )(send)


def _chip_exchange(blocks, name):
    def body(b_ref, o_ref, ssem, rsem):
        x, y, c, chips = _place()
        me = 2 * x + y
        sent = []
        for j, (cx, cy) in enumerate(chips):
            cp = pltpu.make_async_remote_copy(src_ref=b_ref.at[2 * cx + cy], dst_ref=o_ref.at[me], send_sem=ssem.at[j],
                                              recv_sem=rsem.at[j], device_id=(cx, cy, c), device_id_type=MESH)
            cp.start()
            sent.append(cp)
        for j, (cx, cy) in enumerate(chips):
            blk = o_ref.at[2 * cx + cy]
            pltpu.make_async_remote_copy(src_ref=blk, dst_ref=blk, send_sem=ssem.at[j], recv_sem=rsem.at[j],
                                         device_id=(cx, cy, c), device_id_type=MESH).wait_recv()
        for cp in sent:
            cp.wait_send()

    got = pl.pallas_call(
        body, name=name, in_specs=[HBM], out_specs=HBM, out_shape=jax.ShapeDtypeStruct(blocks.shape, blocks.dtype),
        scratch_shapes=[pltpu.SemaphoreType.DMA((3,)), pltpu.SemaphoreType.DMA((3,))],
    )(blocks)
    me = 2 * lax.axis_index("x") + lax.axis_index("y")
    own = lax.dynamic_slice_in_dim(blocks, me, 1, axis=0)
    return lax.dynamic_update_slice_in_dim(got, own, me, axis=0)


def _pair_gather(half, name):
    c = lax.axis_index("c")
    got = _pair_swap(half, name)
    return jnp.stack([jnp.where(c == 0, half, got), jnp.where(c == 0, got, half)])


def _pair_sum(keep, got):
    n, r2, cdim = keep.shape

    def body(a_ref, b_ref, o_ref):
        o_ref[...] = (a_ref[...].astype(F32) + b_ref[...].astype(F32)).astype(BF16)

    spec = pl.BlockSpec((1, RS_TILE, cdim), lambda k, i: (k, i, 0))
    return pl.pallas_call(
        body, name="rs_pair_sum", grid=(n, r2 // RS_TILE), in_specs=[spec, spec], out_specs=spec,
        out_shape=jax.ShapeDtypeStruct(keep.shape, BF16), compiler_params=_cp(("parallel", "parallel")),
    )(keep, got)


def _chip_sum(slots):
    n, r2, cdim = slots.shape

    def body(s_ref, o_ref):
        o_ref[...] = ((s_ref[0].astype(F32) + s_ref[1].astype(F32)) + s_ref[2].astype(F32)) + s_ref[3].astype(F32)

    return pl.pallas_call(
        body, name="rs_chip_sum", grid=(r2 // RS_TILE,),
        in_specs=[pl.BlockSpec((n, RS_TILE, cdim), lambda i: (0, i, 0))],
        out_specs=pl.BlockSpec((RS_TILE, cdim), lambda i: (i, 0)),
        out_shape=jax.ShapeDtypeStruct((r2, cdim), F32), compiler_params=_cp(("parallel",)),
    )(slots)


def _reduce_scatter(packed, tag):
    r2 = packed.shape[1] // 2
    c = lax.axis_index("c")
    keep = lax.dynamic_slice_in_dim(packed, c * r2, r2, axis=1)
    send = lax.dynamic_slice_in_dim(packed, (1 - c) * r2, r2, axis=1)
    got = _pair_swap(send, "rs_pair_swap" + tag)
    slots = _chip_exchange(_pair_sum(keep, got), "rs_chip_exchange" + tag)
    both = _pair_gather(_chip_sum(slots), "rs_pair_gather" + tag)
    return both.reshape(2 * r2, packed.shape[2])


BIG = [("ffn1_w_gate", (D_MODEL, F_SH)), ("ffn1_w_up", (D_MODEL, F_SH)), ("ffn1_w_down", (F_SH, D_MODEL)),
       ("ffn2_w_gate", (D_MODEL, F_SH)), ("ffn2_w_up", (D_MODEL, F_SH)), ("ffn2_w_down", (F_SH, D_MODEL)),
       ("w_in", (D_MODEL, D_IN_SH)), ("lru_wa", (2, 4, 64, LRU_HD)), ("lru_wx", (2, 4, 64, LRU_HD)),
       ("lru_w_out", (256, D_MODEL)), ("sc_w_out", (SC_W, 256)), ("sgu_w_out", (SGU_W, 256)), ("w_o", (256, D_MODEL))]
SMALL = [("lru_conv_w", (4, 256)), ("lru_ba", (2, 256)), ("lru_bx", (2, 256)), ("lru_lambda", (2, 256)),
         ("sc_conv_w", (3, 128))]
REPL = [("ffn1_pre_g", (D_MODEL,)), ("ffn1_post_g", (D_MODEL,)), ("mix_pre_g", (D_MODEL,)), ("lru_conv_b", (LRU_W,)),
        ("sgu_ln_g", (SGU_W,)), ("sgu_ln_b", (SGU_W,)), ("sgu_w_s", (4, CHUNK, CHUNK)), ("sgu_b", (4, CHUNK)),
        ("mix_post_g", (D_MODEL,)), ("ffn2_pre_g", (D_MODEL,)), ("ffn2_post_g", (D_MODEL,))]
WEIGHTS = ['ffn1_pre_g', 'ffn1_w_gate', 'ffn1_w_up', 'ffn1_w_down', 'ffn1_post_g', 'mix_pre_g', 'w_in', 'lru_conv_w',
           'lru_conv_b', 'lru_wa', 'lru_ba', 'lru_wx', 'lru_bx', 'lru_lambda', 'lru_w_out', 'sc_conv_w', 'sc_w_out',
           'sgu_ln_g', 'sgu_ln_b', 'sgu_w_s', 'sgu_b', 'sgu_w_out', 'w_o', 'mix_post_g', 'ffn2_pre_g', 'ffn2_w_gate',
           'ffn2_w_up', 'ffn2_w_down', 'ffn2_post_g']


def _seg_rows(shape):
    return -(-int(np.prod(shape)) // PACK_C)


def _pack(parts, rows_total, dtype):
    lead = parts[0].shape[0]
    segs = []
    used = 0
    for p in parts:
        flat = p.reshape(lead, -1).astype(dtype)
        nr = -(-flat.shape[1] // PACK_C)
        pad = nr * PACK_C - flat.shape[1]
        if pad:
            flat = jnp.pad(flat, ((0, 0), (0, pad)))
        segs.append(flat.reshape(lead, nr, PACK_C))
        used += nr
    if rows_total > used:
        segs.append(jnp.zeros((lead, rows_total - used, PACK_C), dtype))
    return jnp.concatenate(segs, axis=1)


def _unpack(buf, specs):
    lead = buf.shape[0]
    out = {}
    r0 = 0
    for name, shape in specs:
        nr = _seg_rows(shape)
        n = int(np.prod(shape))
        out[name] = buf[:, r0:r0 + nr].reshape(lead, nr * PACK_C)[:, :n].reshape((lead,) + tuple(shape))
        r0 += nr
    return out


def _full_from_shards(name, t):
    if name in ("ffn1_w_gate", "ffn1_w_up", "ffn1_w_down", "ffn2_w_gate", "ffn2_w_up", "ffn2_w_down"):
        return t
    if name in ("w_in", "sc_w_out", "sgu_w_out", "lru_conv_w", "lru_ba", "lru_bx", "lru_lambda", "sc_conv_w"):
        return jnp.moveaxis(t, 0, -2).reshape(t.shape[1:-1] + (N_SHARD * t.shape[-1],))
    if name in ("lru_wa", "lru_wx"):
        return jnp.moveaxis(t, 0, 2).reshape(2, 4, LRU_HD, LRU_HD)
    if name in ("lru_w_out", "w_o"):
        return t.reshape(N_SHARD * t.shape[1], t.shape[2])
    raise ValueError(name)


def _shards_from_full(name, gfull):
    if name in ("ffn1_w_gate", "ffn1_w_up", "ffn1_w_down", "ffn2_w_gate", "ffn2_w_up", "ffn2_w_down"):
        return gfull
    if name in ("w_in", "sc_w_out", "sgu_w_out", "lru_conv_w", "lru_ba", "lru_bx", "lru_lambda", "sc_conv_w"):
        lastdim = gfull.shape[-1] // N_SHARD
        return jnp.moveaxis(gfull.reshape(gfull.shape[:-1] + (N_SHARD, lastdim)), -2, 0)
    if name in ("lru_wa", "lru_wx"):
        return jnp.moveaxis(gfull.reshape(2, 4, N_SHARD, 64, LRU_HD), 2, 0)
    if name in ("lru_w_out", "w_o"):
        return gfull.reshape(N_SHARD, gfull.shape[0] // N_SHARD, gfull.shape[1])
    raise ValueError(name)


def _tiles(s):
    return dict(ffn=min(512, s), ffn_bwd=min(512, s), tn=min(2048, s), mix_in=min(512, s), lru=min(512, s), mix=min(256, s), loss=min(512, s))


def _mixer_fwd(x, w, t):
    hb, za, zm = _mix_in_fwd(x, w["mix_pre_g"], w["w_in"], t["mix_in"])
    hf = _lru_fwd(za, w["lru_conv_w"], w["lru_conv_b"], w["lru_wa"][0], w["lru_ba"][0:1], w["lru_wx"][0],
                  w["lru_bx"][0:1], w["lru_lambda"][0:1], False, t["lru"])
    hr = _lru_fwd(za, w["lru_conv_w"], w["lru_conv_b"], w["lru_wa"][1], w["lru_ba"][1:2], w["lru_wx"][1],
                  w["lru_bx"][1:2], w["lru_lambda"][1:2], True, t["lru"])
    out, yain, q, ycin, ya, yb, yc, mb, mx = _mix_out_fwd(
        x, za, zm, hf, hr, w["lru_w_out"], w["sc_conv_w"], w["sc_w_out"], w["sgu_ln_g"], w["sgu_ln_b"], w["sgu_w_s"],
        w["sgu_b_t"], w["sgu_w_out"], w["w_o"], w["mix_post_g"], t["mix"])
    return out, dict(x=x, hb=hb, za=za, zm=zm, hf=hf, hr=hr, yain=yain, q=q, ycin=ycin, ya=ya, yb=yb, yc=yc, mb=mb, mx=mx)


def _mixer_bwd(dy, sv, w, t):
    g = {}
    (dmx, dya, dyb, dyc, dgate, dscb, dsu, dsv, dzm, dhs, dcp, g["mix_post_g"], g["sgu_ln_g"], g["sgu_ln_b"],
     g["sgu_w_s"], dbs) = _mix_out_bwd(
        dy, sv["za"], sv["zm"], sv["hf"], sv["hr"], sv["ya"], sv["yb"], sv["yc"], sv["mx"], w["lru_w_out"], w["sc_conv_w"],
        w["sc_w_out"], w["sgu_ln_g"], w["sgu_ln_b"], w["sgu_w_s"], w["sgu_w_s_t"], w["sgu_b_t"], w["sgu_w_out"], w["w_o"],
        w["mix_post_g"], t["mix"])
    g["sgu_b"] = jnp.sum(dbs, axis=-1)
    ts = t["tn"]
    g["w_o"] = _tn_plain(sv["mb"], dmx, "dw_o", ts)
    g["lru_w_out"] = _tn_plain(sv["yain"], dya, "dw_lru_out", ts)
    g["sc_w_out"] = _tn_plain(sv["q"], dyb, "dw_sc_out", ts)
    g["sgu_w_out"] = _tn_plain(sv["ycin"], dyc, "dw_sgu_out", ts)
    dxc, dwa, dba, dwx, dbx, dlam = [], [], [], [], [], []
    for d, rev in enumerate((False, True)):
        o = _lru_bwd(sv["za"], sv["hr"] if rev else sv["hf"], dhs, w["lru_conv_w"], w["lru_conv_b"], w["lru_wa"][d],
                     w["lru_ba"][d:d + 1], w["lru_wx"][d], w["lru_bx"][d:d + 1], w["lru_lambda"][d:d + 1], rev, t["lru"])
        for lst, val in zip((dxc, dwa, dba, dwx, dbx, dlam), o):
            lst.append(val)
    g["lru_wa"] = jnp.stack(dwa)
    g["lru_wx"] = jnp.stack(dwx)
    g["lru_ba"] = jnp.concatenate(dba, axis=0)
    g["lru_bx"] = jnp.concatenate(dbx, axis=0)
    g["lru_lambda"] = jnp.concatenate(dlam, axis=0)
    dx, dz, g["mix_pre_g"], g["lru_conv_w"], g["lru_conv_b"], g["sc_conv_w"] = _mix_in_bwd(
        dy, sv["x"], sv["za"], dxc[0], dxc[1], dcp, dgate, dscb, dsu, dsv, dzm, w["mix_pre_g"], w["lru_conv_w"],
        w["sc_conv_w"], w["w_in"], t["mix"])
    g["w_in"] = _tn(sv["hb"], dz, "dw_in", N_SHARD, (ts, D_MODEL), lambda b, s_: (s_, 0), (ts, D_IN_SH), lambda b, s_: (s_, b),
                    (D_MODEL, D_IN), (D_MODEL, D_IN_SH), lambda b, s_: (0, b), ts)
    return dx, g


def kernel(x, ffn1_pre_g, ffn1_w_gate, ffn1_w_up, ffn1_w_down, ffn1_post_g, mix_pre_g, w_in, lru_conv_w, lru_conv_b, lru_wa, lru_ba, lru_wx, lru_bx, lru_lambda, lru_w_out, sc_conv_w, sc_w_out, sgu_ln_g, sgu_ln_b, sgu_w_s, sgu_b, sgu_w_out, w_o, mix_post_g, ffn2_pre_g, ffn2_w_gate, ffn2_w_up, ffn2_w_down, ffn2_post_g, loss_target, m_ffn1_pre_g, m_ffn1_w_gate, m_ffn1_w_up, m_ffn1_w_down, m_ffn1_post_g, m_mix_pre_g, m_w_in, m_lru_conv_w, m_lru_conv_b, m_lru_wa, m_lru_ba, m_lru_wx, m_lru_bx, m_lru_lambda, m_lru_w_out, m_sc_conv_w, m_sc_w_out, m_sgu_ln_g, m_sgu_ln_b, m_sgu_w_s, m_sgu_b, m_sgu_w_out, m_w_o, m_mix_post_g, m_ffn2_pre_g, m_ffn2_w_gate, m_ffn2_w_up, m_ffn2_w_down, m_ffn2_post_g, v_ffn1_pre_g, v_ffn1_w_gate, v_ffn1_w_up, v_ffn1_w_down, v_ffn1_post_g, v_mix_pre_g, v_w_in, v_lru_conv_w, v_lru_conv_b, v_lru_wa, v_lru_ba, v_lru_wx, v_lru_bx, v_lru_lambda, v_lru_w_out, v_sc_conv_w, v_sc_w_out, v_sgu_ln_g, v_sgu_ln_b, v_sgu_w_s, v_sgu_b, v_sgu_w_out, v_w_o, v_mix_post_g, v_ffn2_pre_g, v_ffn2_w_gate, v_ffn2_w_up, v_ffn2_w_down, v_ffn2_post_g):
    args = (ffn1_pre_g, ffn1_w_gate, ffn1_w_up, ffn1_w_down, ffn1_post_g, mix_pre_g, w_in, lru_conv_w, lru_conv_b, lru_wa, lru_ba, lru_wx, lru_bx, lru_lambda, lru_w_out, sc_conv_w, sc_w_out, sgu_ln_g, sgu_ln_b, sgu_w_s, sgu_b, sgu_w_out, w_o, mix_post_g, ffn2_pre_g, ffn2_w_gate, ffn2_w_up, ffn2_w_down, ffn2_post_g)
    margs = (m_ffn1_pre_g, m_ffn1_w_gate, m_ffn1_w_up, m_ffn1_w_down, m_ffn1_post_g, m_mix_pre_g, m_w_in, m_lru_conv_w, m_lru_conv_b, m_lru_wa, m_lru_ba, m_lru_wx, m_lru_bx, m_lru_lambda, m_lru_w_out, m_sc_conv_w, m_sc_w_out, m_sgu_ln_g, m_sgu_ln_b, m_sgu_w_s, m_sgu_b, m_sgu_w_out, m_w_o, m_mix_post_g, m_ffn2_pre_g, m_ffn2_w_gate, m_ffn2_w_up, m_ffn2_w_down, m_ffn2_post_g)
    vargs = (v_ffn1_pre_g, v_ffn1_w_gate, v_ffn1_w_up, v_ffn1_w_down, v_ffn1_post_g, v_mix_pre_g, v_w_in, v_lru_conv_w, v_lru_conv_b, v_lru_wa, v_lru_ba, v_lru_wx, v_lru_bx, v_lru_lambda, v_lru_w_out, v_sc_conv_w, v_sc_w_out, v_sgu_ln_g, v_sgu_ln_b, v_sgu_w_s, v_sgu_b, v_sgu_w_out, v_w_o, v_mix_post_g, v_ffn2_pre_g, v_ffn2_w_gate, v_ffn2_w_up, v_ffn2_w_down, v_ffn2_post_g)
    wsh = dict(zip(WEIGHTS, args))
    msh = dict(zip(WEIGHTS, margs))
    vsh = dict(zip(WEIGHTS, vargs))
    xs = x[0]
    s = xs.shape[0]
    t = _tiles(s)

    small = _pack([wsh[n].reshape(1, -1) for n, _ in SMALL], SMALL_ROWS, F32)[0]
    small_all = _all_gather(small, "ag_small")
    small_un = _unpack(small_all, [(n, (DEPTH,) + sh) for n, sh in SMALL])
    layers = []
    for l in range(DEPTH):
        packed = _pack([wsh[n][l][None] for n, _ in BIG], AG_ROWS, BF16)[0]
        gathered = _all_gather(packed, "ag_weights_l%d" % l)
        un = _unpack(gathered, BIG)
        w = {n: _full_from_shards(n, un[n]) for n, _ in BIG}
        for n, _ in SMALL:
            w[n] = _full_from_shards(n, small_un[n][:, l])
        for n, _ in REPL:
            w[n] = wsh[n][l]
        for n in ("ffn1_pre_g", "ffn1_post_g", "mix_pre_g", "lru_conv_b", "sgu_ln_g", "sgu_ln_b", "mix_post_g", "ffn2_pre_g",
                  "ffn2_post_g"):
            w[n] = w[n].reshape(1, -1)
        w["sgu_w_s_t"] = jnp.swapaxes(w["sgu_w_s"], 1, 2).astype(BF16)
        w["sgu_w_s"] = w["sgu_w_s"].astype(BF16)
        w["sgu_b_t"] = w["sgu_b"].T
        layers.append(w)

    saved = []
    cur = xs
    for l in range(DEPTH):
        w = layers[l]
        sv = {}
        x1, g1, u1, f1 = _ffn_fwd(cur, w["ffn1_pre_g"], w["ffn1_w_gate"], w["ffn1_w_up"], w["ffn1_w_down"], w["ffn1_post_g"], t["ffn"])
        sv["ffn1"] = (cur, g1, u1, f1)
        x2, sv["mix"] = _mixer_fwd(x1, w, t)
        x3, g2, u2, f2 = _ffn_fwd(x2, w["ffn2_pre_g"], w["ffn2_w_gate"], w["ffn2_w_up"], w["ffn2_w_down"], w["ffn2_post_g"], t["ffn"])
        sv["ffn2"] = (x2, g2, u2, f2)
        saved.append(sv)
        cur = x3

    dy, sq = _loss_head(cur, loss_target[0], t["loss"])
    loss = lax.psum(0.5 * jnp.sum(sq) / D_MODEL, ("x", "y", "c"))

    grads = [None] * DEPTH
    for l in reversed(range(DEPTH)):
        w = layers[l]
        sv = saved[l]
        g = {}
        for tag in ("ffn2", "mix", "ffn1"):
            if tag == "mix":
                dy, gm = _mixer_bwd(dy, sv["mix"], w, t)
                g.update(gm)
                continue
            xin, gg, uu, ff = sv[tag]
            dy, hb, ab, dgb, dub, dfb, g[tag + "_pre_g"], g[tag + "_post_g"] = _ffn_bwd(
                dy, xin, gg, uu, ff, w[tag + "_pre_g"], w[tag + "_post_g"], w[tag + "_w_gate"], w[tag + "_w_up"],
                w[tag + "_w_down"], t["ffn_bwd"])
            g[tag + "_w_gate"], g[tag + "_w_up"], g[tag + "_w_down"] = _ffn_wgrads(hb, ab, dgb, dub, dfb, t["tn"])
        parts = [_shards_from_full(n, g[n]) for n, _ in BIG] + [_shards_from_full(n, g[n]) for n, _ in SMALL]
        parts += [jnp.broadcast_to(g[n].reshape((1,) + sh), (N_SHARD,) + sh) for n, sh in REPL]
        total = _reduce_scatter(_pack(parts, RS_ROWS, BF16), "_l%d" % l)
        un = _unpack(total[None], BIG + SMALL + REPL)
        grads[l] = {n: un[n][0] for n in WEIGHTS}
    grad_x = dy[None]

    gw, dw, nm, nv = [], [], [], []
    for n in WEIGHTS:
        gfull = jnp.stack([grads[l][n] for l in range(DEPTH)])
        d_, m_, v_ = _adamw(wsh[n], gfull, msh[n], vsh[n], n)
        gw.append(gfull)
        dw.append(d_)
        nm.append(m_)
        nv.append(v_)
    return (loss, grad_x, *gw, *dw, *nm, *nv)
```

```python
import functools

import jax
import jax.numpy as jnp
import numpy as np
from jax import lax
from jax.experimental import pallas as pl
from jax.experimental.pallas import tpu as pltpu

F32 = jnp.float32
BF16 = jnp.bfloat16
MESH = pl.DeviceIdType.MESH

D_MODEL = 1024
D_FF = 2816
N_SHARD = 4
F_SH = D_FF // N_SHARD
D_IN = 7680
D_IN_SH = D_IN // N_SHARD
ZA_W = 4608
ZM_W = 3072
LRU_W = 1024
LRU_HD = 256
SC_W = 512
SGU_W = 512
CHUNK = 128
DEPTH = 4
EPS = 1e-6
LRU_C = 8.0
LRU_RB = 16

ADAM_LR, ADAM_B1, ADAM_B2, ADAM_EPS, ADAM_WD, ADAM_STEP = 0.001, 0.9, 0.999, 1e-08, 0.01, 10

VMEM_LIMIT = 56 * 2 ** 20
PACK_C = 1024
AG_ROWS = 7168
RS_ROWS = 7296
RS_TILE = 192
SMALL_ROWS = 32


def _cp(sem=("arbitrary",)):
    return pltpu.CompilerParams(dimension_semantics=sem, vmem_limit_bytes=VMEM_LIMIT)


def _full(shape):
    return pl.BlockSpec(shape, lambda *_: (0,) * len(shape))


def _res(shape):
    return pl.BlockSpec(shape, lambda *_: (0,) * len(shape), pipeline_mode=pl.Buffered(1))


def _dot(a, b):
    return jnp.dot(a, b, preferred_element_type=F32)


def _dot_nt(a, b):
    return lax.dot_general(a, b, (((1,), (1,)), ((), ())), preferred_element_type=F32)


def _dot_tn(a, b):
    return lax.dot_general(a, b, (((0,), (0,)), ((), ())), preferred_element_type=F32)


def _sigmoid(x):
    return 0.5 * jnp.tanh(0.5 * x) + 0.5


_GELU_K = 0.7978845608028654
_GELU_C = 0.044715


def _gelu(x):
    return 0.5 * x * (1.0 + jnp.tanh(_GELU_K * (x + _GELU_C * x * x * x)))


def _gelu_and_grad(x):
    t = jnp.tanh(_GELU_K * (x + _GELU_C * x * x * x))
    g = 0.5 * x * (1.0 + t)
    dg = 0.5 * (1.0 + t) + 0.5 * x * (1.0 - t * t) * (_GELU_K * (1.0 + 3.0 * _GELU_C * x * x))
    return g, dg


def _rms_fwd(x, g):
    rs = lax.rsqrt(jnp.mean(x * x, axis=-1, keepdims=True) + EPS)
    return x * rs * g


def _rms_bwd(dy, x, g):
    rs = lax.rsqrt(jnp.mean(x * x, axis=-1, keepdims=True) + EPS)
    n = x * rs
    dn = dy * g
    dx = rs * (dn - n * jnp.mean(dn * n, axis=-1, keepdims=True))
    return dx, jnp.sum(dy * n, axis=0, keepdims=True)


def _acc(ref, val, first):
    @pl.when(first)
    def _():
        ref[...] = val

    @pl.when(jnp.logical_not(first))
    def _():
        ref[...] += val


def _shift(xm, d, prev, nxt, rows):
    tm = xm.shape[0]
    if d == 0:
        return xm
    y = pltpu.roll(xm, (-d) % tm, 0)
    rows8 = rows[0:8]
    if d < 0:
        hb = prev.shape[0]
        top = y[0:8]
        for r in range(-d):
            top = jnp.where(rows8 == r, prev[hb + r + d:hb + r + d + 1, :], top)
        return jnp.concatenate([top, y[8:]], axis=0)
    bot = y[tm - 8:]
    for r in range(d):
        bot = jnp.where(rows8 == 8 - d + r, nxt[r:r + 1, :], bot)
    return jnp.concatenate([y[:tm - 8], bot], axis=0)


def _halo(tm, hb, w, col, nt, rev=False):
    r = tm // hb
    last = nt * r - 1
    ti = (lambda i: nt - 1 - i) if rev else (lambda i: i)
    return [pl.BlockSpec((tm, w), lambda i: (ti(i), col)),
            pl.BlockSpec((hb, w), lambda i: (jnp.maximum(ti(i) * r - 1, 0), col)),
            pl.BlockSpec((hb, w), lambda i: (jnp.minimum((ti(i) + 1) * r, last), col))]


def _edges(prev_ref, next_ref, ti, nt):
    prev = jnp.where(ti > 0, prev_ref[...].astype(F32), 0.0)
    nxt = jnp.where(ti < nt - 1, next_ref[...].astype(F32), 0.0)
    return prev, nxt


def _ffn_fwd(x, pre_g, wg, wu, wd, post_g, tm):
    s = x.shape[0]
    nt = s // tm

    def body(x_ref, pg_ref, wg_ref, wu_ref, wd_ref, qg_ref, o_ref, a_ref, d1_ref, d2_ref, f_ref):
        xv = x_ref[...]
        hb = _rms_fwd(xv, pg_ref[...]).astype(BF16)
        f = jnp.zeros((tm, D_MODEL), F32)
        for k in range(N_SHARD):
            g = _dot_nt(hb, wg_ref[k])
            u = _dot_nt(hb, wu_ref[k])
            sg = _sigmoid(g)
            silu = g * sg
            ab = (silu * u).astype(BF16)
            a_ref[k] = ab
            d1_ref[k] = (u * (sg * (1.0 + g * (1.0 - sg)))).astype(BF16)
            d2_ref[k] = silu.astype(BF16)
            f = f + _dot(ab, wd_ref[k])
        f_ref[...] = f.astype(BF16)
        o_ref[...] = xv + 0.5 * _rms_fwd(f, qg_ref[...])

    row = pl.BlockSpec((tm, D_MODEL), lambda i: (i, 0))
    gu = pl.BlockSpec((N_SHARD, tm, F_SH), lambda i: (0, i, 0))
    return pl.pallas_call(
        body, name="ffn_fwd", grid=(nt,),
        in_specs=[row, _full((1, D_MODEL)), _res((N_SHARD, F_SH, D_MODEL)), _res((N_SHARD, F_SH, D_MODEL)),
                  _res((N_SHARD, F_SH, D_MODEL)), _full((1, D_MODEL))],
        out_specs=[row, gu, gu, gu, row],
        out_shape=[jax.ShapeDtypeStruct((s, D_MODEL), F32), jax.ShapeDtypeStruct((N_SHARD, s, F_SH), BF16),
                   jax.ShapeDtypeStruct((N_SHARD, s, F_SH), BF16), jax.ShapeDtypeStruct((N_SHARD, s, F_SH), BF16),
                   jax.ShapeDtypeStruct((s, D_MODEL), BF16)],
        compiler_params=_cp(("parallel",)),
    )(x, pre_g, wg, wu, wd, post_g)


def _ffn_bwd(dy, x, d1, d2, f, pre_g, post_g, wg, wu, wd, tm):
    s = x.shape[0]
    nt = s // tm

    def body_a(dy_ref, d1_ref, d2_ref, f_ref, qg_ref, wd_ref, dg_ref, du_ref, df_ref, dqg_ref):
        df, dq = _rms_bwd(0.5 * dy_ref[...], f_ref[...].astype(F32), qg_ref[...])
        dfb = df.astype(BF16)
        df_ref[...] = dfb
        for k in range(N_SHARD):
            da = _dot_nt(dfb, wd_ref[k])
            dg_ref[k] = (da * d1_ref[k].astype(F32)).astype(BF16)
            du_ref[k] = (da * d2_ref[k].astype(F32)).astype(BF16)
        _acc(dqg_ref, dq, pl.program_id(0) == 0)

    def body_b(dy_ref, x_ref, dg_ref, du_ref, pg_ref, wg_ref, wu_ref, dx_ref, h_ref, dpg_ref):
        xv = x_ref[...]
        h_ref[...] = _rms_fwd(xv, pg_ref[...]).astype(BF16)
        dh = jnp.zeros((tm, D_MODEL), F32)
        for k in range(N_SHARD):
            dh = dh + _dot(dg_ref[k], wg_ref[k]) + _dot(du_ref[k], wu_ref[k])
        dxn, dp = _rms_bwd(dh, xv, pg_ref[...])
        dx_ref[...] = dy_ref[...] + dxn
        _acc(dpg_ref, dp, pl.program_id(0) == 0)

    row = pl.BlockSpec((tm, D_MODEL), lambda i: (i, 0))
    gu = pl.BlockSpec((N_SHARD, tm, F_SH), lambda i: (0, i, 0))
    vec = _full((1, D_MODEL))
    wsp = _res((N_SHARD, F_SH, D_MODEL))
    big = jax.ShapeDtypeStruct((N_SHARD, s, F_SH), BF16)
    tok_bf = jax.ShapeDtypeStruct((s, D_MODEL), BF16)
    vec_f = jax.ShapeDtypeStruct((1, D_MODEL), F32)
    dg, du, df, dqg = pl.pallas_call(
        body_a, name="ffn_bwd_a", grid=(nt,),
        in_specs=[row, gu, gu, row, vec, wsp],
        out_specs=[gu, gu, row, vec],
        out_shape=[big, big, tok_bf, vec_f],
        compiler_params=_cp(),
    )(dy, d1, d2, f, post_g, wd)
    dx, h, dpg = pl.pallas_call(
        body_b, name="ffn_bwd_b", grid=(nt,),
        in_specs=[row, row, gu, gu, vec, wsp, wsp],
        out_specs=[row, row, vec],
        out_shape=[jax.ShapeDtypeStruct((s, D_MODEL), F32), tok_bf, vec_f],
        compiler_params=_cp(),
    )(dy, x, dg, du, pre_g, wg, wu)
    return dx, h, dg, du, df, dpg, dqg


def _tn(lhs, rhs, name, nb, lhs_blk, lhs_map, rhs_blk, rhs_map, out_shape, out_blk, out_map, ts):
    s = lhs.shape[-2]
    ns = s // ts
    acc_shape = tuple(d for d in out_blk if d is not None)

    def body(l_ref, r_ref, o_ref, acc_s):
        t = pl.program_id(1)
        _acc(acc_s, _dot_tn(l_ref[...], r_ref[...]), t == 0)

        @pl.when(t == ns - 1)
        def _():
            o_ref[...] = acc_s[...].astype(BF16)

    return pl.pallas_call(
        body, name=name, grid=(nb, ns),
        in_specs=[pl.BlockSpec(lhs_blk, lhs_map), pl.BlockSpec(rhs_blk, rhs_map)],
        out_specs=pl.BlockSpec(out_blk, out_map),
        out_shape=jax.ShapeDtypeStruct(out_shape, BF16),
        scratch_shapes=[pltpu.VMEM(acc_shape, F32)],
        compiler_params=_cp(("parallel", "arbitrary")),
    )(lhs, rhs)


def _tn_plain(lhs, rhs, name, ts):
    m, n = lhs.shape[1], rhs.shape[1]
    return _tn(lhs, rhs, name, 1, (ts, m), lambda b, t: (t, 0), (ts, n), lambda b, t: (t, 0),
               (m, n), (m, n), lambda b, t: (0, 0), ts)


def _ffn_wgrads(h, a, dg, du, df, ts):
    s = h.shape[0]
    sh3 = (N_SHARD, F_SH, D_MODEL)
    lm2 = lambda b, t: (t, 0)
    bm3 = lambda b, t: (b, t, 0)
    om3 = lambda b, t: (b, 0, 0)
    one = lambda lhs, rhs, name: _tn(lhs, rhs, name, N_SHARD, (None, ts, F_SH), bm3, (ts, D_MODEL), lm2, sh3,
                                     (None, F_SH, D_MODEL), om3, ts)
    return one(dg, h, "ffn_dwg"), one(du, h, "ffn_dwu"), one(a, df, "ffn_dwd")


def _mix_in_fwd(x, pre_g, w_in, tm):
    s = x.shape[0]
    nt = s // tm
    cw = 512

    def body(x_ref, pg_ref, w_ref, h_ref, za_ref, zm_ref):
        hb = _rms_fwd(x_ref[...], pg_ref[...]).astype(BF16)
        h_ref[...] = hb
        for j in range(ZA_W // cw):
            za_ref[:, j * cw:(j + 1) * cw] = _dot_nt(hb, w_ref[j * cw:(j + 1) * cw, :]).astype(BF16)
        for j in range(ZM_W // cw):
            zm_ref[:, j * cw:(j + 1) * cw] = _dot_nt(hb, w_ref[ZA_W + j * cw:ZA_W + (j + 1) * cw, :]).astype(BF16)

    row = pl.BlockSpec((tm, D_MODEL), lambda i: (i, 0))
    return pl.pallas_call(
        body, name="mix_in_fwd", grid=(nt,),
        in_specs=[row, _full((1, D_MODEL)), _res((D_IN, D_MODEL))],
        out_specs=[row, pl.BlockSpec((tm, ZA_W), lambda i: (i, 0)), pl.BlockSpec((tm, ZM_W), lambda i: (i, 0))],
        out_shape=[jax.ShapeDtypeStruct((s, D_MODEL), BF16), jax.ShapeDtypeStruct((s, ZA_W), BF16),
                   jax.ShapeDtypeStruct((s, ZM_W), BF16)],
        compiler_params=_cp(("parallel",)),
    )(x, pre_g, w_in)


def _lru_conv(xm, prev, nxt, rows, cw_ref, cb_ref):
    acc = cb_ref[...] + cw_ref[2:3, :] * xm
    acc = acc + cw_ref[0:1, :] * _shift(xm, -2, prev, nxt, rows)
    acc = acc + cw_ref[1:2, :] * _shift(xm, -1, prev, nxt, rows)
    acc = acc + cw_ref[3:4, :] * _shift(xm, 1, prev, nxt, rows)
    return acc


def _lru_preact(x_ref, xp_ref, xn_ref, cw_ref, cb_ref, wa_ref, ba_ref, wx_ref, bx_ref, lam_ref, ti, nt, tm,
                xc_s, ra_s, xa_s, c8_s):
    rows = lax.broadcasted_iota(jnp.int32, (tm, 1), 0)
    prev, nxt = _edges(xp_ref, xn_ref, ti, nt)
    xc = _lru_conv(x_ref[...].astype(F32), prev, nxt, rows, cw_ref, cb_ref)
    xc_s[...] = xc
    xcb = xc.astype(BF16)
    for h in range(4):
        cs_ = slice(LRU_HD * h, LRU_HD * (h + 1))
        ra_s[:, cs_] = _dot(xcb[:, cs_], wa_ref[h]) + ba_ref[:, cs_]
        xa_s[:, cs_] = _dot(xcb[:, cs_], wx_ref[h]) + bx_ref[:, cs_]
    lam = lam_ref[...]
    e = jnp.exp(-jnp.abs(lam))
    log1p_e = jnp.where(e < 1e-2, e * (1.0 - e * (0.5 - e * (1.0 / 3.0))), jnp.log(1.0 + e))
    c8_s[...] = jnp.broadcast_to(-LRU_C * (jnp.maximum(-lam, 0.0) + log1p_e), c8_s.shape)
    return xcb


def _lru_decay(cl, r):
    la = cl * r
    a = jnp.exp(la)
    y2 = 2.0 * la
    em = jnp.where(y2 > -0.004, y2 * (-1.0 + y2 * (-0.5 - y2 * (1.0 / 6.0))), 1.0 - a * a)
    return a, jnp.sqrt(em)


def _scan_rows(a_s, b_s, o_s, carry, tm, descending):
    nb = tm // 8

    def blk(j, h):
        jb = (nb - 1 - j) if descending else j
        base = pl.multiple_of(jb * 8, 8)
        for r in range(8):
            t = base + ((7 - r) if descending else r)
            h = a_s[pl.ds(t, 1), :] * h + b_s[pl.ds(t, 1), :]
            o_s[pl.ds(t, 1), :] = h
        return h

    return lax.fori_loop(0, nb, blk, carry)


def _lru_fwd(za, conv_w, conv_b, wa, ba, wx, bx, lam, rev, tm):
    s = za.shape[0]
    nt = s // tm

    def body(x_ref, xp_ref, xn_ref, cw_ref, cb_ref, wa_ref, ba_ref, wx_ref, bx_ref, lam_ref, h_ref,
             xc_s, ra_s, xa_s, o_s, c8_s, c_s):
        i = pl.program_id(0)
        ti = (nt - 1 - i) if rev else i
        _lru_preact(x_ref, xp_ref, xn_ref, cw_ref, cb_ref, wa_ref, ba_ref, wx_ref, bx_ref, lam_ref, ti, nt, tm,
                    xc_s, ra_s, xa_s, c8_s)

        def gate_blk(j, carry):
            rws = pl.ds(pl.multiple_of(j * LRU_RB, LRU_RB), LRU_RB)
            r = _sigmoid(ra_s[rws, :])
            ig = _sigmoid(xa_s[rws, :])
            a, mult = _lru_decay(c8_s[...], r)
            ra_s[rws, :] = a
            xa_s[rws, :] = ig * xc_s[rws, :] * mult
            return carry

        lax.fori_loop(0, tm // LRU_RB, gate_blk, 0)

        @pl.when(i == 0)
        def _():
            c_s[...] = jnp.zeros_like(c_s)

        c_s[...] = _scan_rows(ra_s, xa_s, o_s, c_s[...], tm, rev)
        h_ref[...] = o_s[...].astype(BF16)

    vec = _full((1, LRU_W))
    hd = _full((4, LRU_HD, LRU_HD))
    ti = (lambda i: nt - 1 - i) if rev else (lambda i: i)
    tile = pltpu.VMEM((tm, LRU_W), F32)
    return pl.pallas_call(
        body, name="lru_fwd_rev" if rev else "lru_fwd", grid=(nt,),
        in_specs=_halo(tm, 16, LRU_W, 1, nt, rev) + [_full((4, LRU_W)), vec, hd, vec, hd, vec, vec],
        out_specs=pl.BlockSpec((tm, LRU_W), lambda i: (ti(i), 0)),
        out_shape=jax.ShapeDtypeStruct((s, LRU_W), BF16),
        scratch_shapes=[tile, tile, tile, tile, pltpu.VMEM((LRU_RB, LRU_W), F32), pltpu.VMEM((1, LRU_W), F32)],
        compiler_params=_cp(),
    )(za, za, za, conv_w, conv_b, wa, ba, wx, bx, lam)


def _sc_conv(c_ref, cp_ref, cn_ref, x_ref, xp_ref, xn_ref, w_ref, ti, nt, rows):
    cprev, cnext = _edges(cp_ref, cn_ref, ti, nt)
    xprev, xnext = _edges(xp_ref, xn_ref, ti, nt)
    cv = c_ref[...].astype(F32)
    xv = x_ref[...].astype(F32)
    p = cv * xv
    pm1 = _shift(p, -1, cprev * xprev, cnext * xnext, rows)
    pp1 = _shift(p, 1, cprev * xprev, cnext * xnext, rows)
    conv = w_ref[0:1, :] * pm1 + w_ref[1:2, :] * p + w_ref[2:3, :] * pp1
    return cv, xv, p, pm1, pp1, conv


def _sgu_norm(vz, g_ref, b_ref):
    vg, dvg = _gelu_and_grad(vz)
    mu = jnp.mean(vg, axis=-1, keepdims=True)
    cen = vg - mu
    rs = lax.rsqrt(jnp.mean(cen * cen, axis=-1, keepdims=True) + EPS)
    vn = cen * rs
    return dvg, vn, rs, vn * g_ref[...] + b_ref[...]


def _mix_out_fwd(x, za, zm, hf, hr, lru_w_out, sc_conv_w, sc_w_out, ln_g, ln_b, w_s, b_s_t, sgu_w_out, w_o, post_g, tm):
    s = x.shape[0]
    nt = s // tm
    nc = tm // CHUNK

    def body(x_ref, gate_ref, scb_ref, scc_ref, sccp_ref, sccn_ref, scx_ref, scxp_ref, scxn_ref, su_ref, sv_ref,
             zm_ref, hf_ref, hr_ref, wlo_ref, scw_ref, wso_ref, lg_ref, lb_ref, ws_ref, bs_ref, wgo_ref, wo_ref,
             qg_ref, o_ref, yain_ref, q_ref, ycin_ref, ya_ref, yb_ref, yc_ref, m_ref, mx_ref, mixed_s):
        ti = pl.program_id(0)
        rows = lax.broadcasted_iota(jnp.int32, (tm, 1), 0)
        hs = hf_ref[...].astype(F32) + hr_ref[...].astype(F32)
        yain = (hs * _gelu(gate_ref[...].astype(F32))).astype(BF16)
        yain_ref[...] = yain
        ya = _dot(yain, wlo_ref[...])
        _, _, _, _, _, conv = _sc_conv(scc_ref, sccp_ref, sccn_ref, scx_ref, scxp_ref, scxn_ref, scw_ref, ti, nt, rows)
        qb = (scb_ref[...].astype(F32) * conv).astype(BF16)
        q_ref[...] = qb
        yb = _dot(qb, wso_ref[...])
        _, _, _, v = _sgu_norm(sv_ref[...].astype(F32), lg_ref, lb_ref)
        vb = v.astype(BF16)
        for n in range(nc):
            for g in range(4):
                blk = vb[n * CHUNK:(n + 1) * CHUNK, g * CHUNK:(g + 1) * CHUNK]
                mixed_s[n * CHUNK:(n + 1) * CHUNK, g * CHUNK:(g + 1) * CHUNK] = _dot(ws_ref[g], blk) + bs_ref[:, g:g + 1]
        ycin = (_gelu(su_ref[...].astype(F32)) * mixed_s[...]).astype(BF16)
        ycin_ref[...] = ycin
        yc = _dot(ycin, wgo_ref[...])
        m = (_sigmoid(zm_ref[:, 0:D_MODEL].astype(F32)) * ya + _sigmoid(zm_ref[:, D_MODEL:2 * D_MODEL].astype(F32)) * yb
             + _sigmoid(zm_ref[:, 2 * D_MODEL:3 * D_MODEL].astype(F32)) * yc)
        mb = m.astype(BF16)
        mx = _dot(mb, wo_ref[...])
        ya_ref[...] = ya.astype(BF16)
        yb_ref[...] = yb.astype(BF16)
        yc_ref[...] = yc.astype(BF16)
        m_ref[...] = mb
        mx_ref[...] = mx.astype(BF16)
        o_ref[...] = x_ref[...] + _rms_fwd(mx, qg_ref[...])

    row = pl.BlockSpec((tm, D_MODEL), lambda i: (i, 0))
    half = pl.BlockSpec((tm, 512), lambda i: (i, 0))
    col = lambda c: pl.BlockSpec((tm, 512), lambda i: (i, c))
    in_specs = ([row, pl.BlockSpec((tm, LRU_W), lambda i: (i, 0)), col(4)] + _halo(tm, 16, SC_W, 5, nt) + _halo(tm, 16, SC_W, 6, nt)
                + [col(7), col(8), pl.BlockSpec((tm, ZM_W), lambda i: (i, 0)), row, row,
                   _full((LRU_W, D_MODEL)), _full((3, SC_W)), _full((SC_W, D_MODEL)), _full((1, SGU_W)), _full((1, SGU_W)),
                   _full((4, CHUNK, CHUNK)), _full((CHUNK, 4)), _full((SGU_W, D_MODEL)), _full((D_MODEL, D_MODEL)),
                   _full((1, D_MODEL))])
    bf = lambda w: jax.ShapeDtypeStruct((s, w), BF16)
    return pl.pallas_call(
        body, name="mix_out_fwd", grid=(nt,),
        in_specs=in_specs,
        out_specs=[row, row, half, half, row, row, row, row, row],
        out_shape=[jax.ShapeDtypeStruct((s, D_MODEL), F32), bf(LRU_W), bf(SC_W), bf(SGU_W), bf(D_MODEL), bf(D_MODEL),
                   bf(D_MODEL), bf(D_MODEL), bf(D_MODEL)],
        scratch_shapes=[pltpu.VMEM((tm, SGU_W), F32)],
        compiler_params=_cp(("parallel",)),
    )(x, za, za, za, za, za, za, za, za, za, za, zm, hf, hr, lru_w_out, sc_conv_w, sc_w_out, ln_g, ln_b, w_s, b_s_t,
      sgu_w_out, w_o, post_g)


def _mix_out_bwd(dy, za, zm, hf, hr, ya, yb, yc, mx, lru_w_out, sc_conv_w, sc_w_out, ln_g, ln_b, w_s, w_s_t, b_s_t,
                 sgu_w_out, w_o, post_g, tm):
    s = dy.shape[0]
    nt = s // tm
    nc = tm // CHUNK

    def body(dy_ref, gate_ref, scb_ref, scc_ref, sccp_ref, sccn_ref, scx_ref, scxp_ref, scxn_ref, su_ref, sv_ref,
             zm_ref, hf_ref, hr_ref, ya_ref, yb_ref, yc_ref, mx_ref, wlo_ref, scw_ref, wso_ref, lg_ref, lb_ref,
             ws_ref, wst_ref, bs_ref, wgo_ref, wo_ref, qg_ref,
             dmx_ref, dya_ref, dyb_ref, dyc_ref, dgate_ref, dscb_ref, dsu_ref, dsv_ref, dzm_ref, dhs_ref, dcp_ref,
             dqg_ref, dlg_ref, dlb_ref, dws_ref, dbs_ref, mixed_s, dv_s):
        ti = pl.program_id(0)
        first = ti == 0
        rows = lax.broadcasted_iota(jnp.int32, (tm, 1), 0)
        dmx, dq = _rms_bwd(dy_ref[...], mx_ref[...].astype(F32), qg_ref[...])
        _acc(dqg_ref, dq, first)
        dmxb = dmx.astype(BF16)
        dmx_ref[...] = dmxb
        dm = _dot_nt(dmxb, wo_ref[...])
        dys = []
        for k, (y_ref, d_ref) in enumerate(((ya_ref, dya_ref), (yb_ref, dyb_ref), (yc_ref, dyc_ref))):
            gk = _sigmoid(zm_ref[:, k * D_MODEL:(k + 1) * D_MODEL].astype(F32))
            dyk = (dm * gk).astype(BF16)
            d_ref[...] = dyk
            dys.append(dyk)
            dzm_ref[:, k * D_MODEL:(k + 1) * D_MODEL] = (dm * y_ref[...].astype(F32) * gk * (1.0 - gk)).astype(BF16)
        dyain = _dot_nt(dys[0], wlo_ref[...])
        gg, dgg = _gelu_and_grad(gate_ref[...].astype(F32))
        hs = hf_ref[...].astype(F32) + hr_ref[...].astype(F32)
        dhs_ref[...] = dyain * gg
        dgate_ref[...] = (dyain * hs * dgg).astype(BF16)
        dq_b = _dot_nt(dys[1], wso_ref[...])
        _, _, _, _, _, conv = _sc_conv(scc_ref, sccp_ref, sccn_ref, scx_ref, scxp_ref, scxn_ref, scw_ref, ti, nt, rows)
        dscb_ref[...] = (dq_b * conv).astype(BF16)
        dcp_ref[...] = dq_b * scb_ref[...].astype(F32)
        dycin = _dot_nt(dys[2], wgo_ref[...])
        dvg, vn, rs, v = _sgu_norm(sv_ref[...].astype(F32), lg_ref, lb_ref)
        vb = v.astype(BF16)
        ug, dug = _gelu_and_grad(su_ref[...].astype(F32))
        dmixed = dycin * ug
        dmb = dmixed.astype(BF16)
        dws = [jnp.zeros((CHUNK, CHUNK), F32) for _ in range(4)]
        dbs = [jnp.zeros((CHUNK, CHUNK), F32) for _ in range(4)]
        for n in range(nc):
            for g in range(4):
                rs_, cs_ = slice(n * CHUNK, (n + 1) * CHUNK), slice(g * CHUNK, (g + 1) * CHUNK)
                mixed_s[rs_, cs_] = _dot(ws_ref[g], vb[rs_, cs_]) + bs_ref[:, g:g + 1]
                dv_s[rs_, cs_] = _dot(wst_ref[g], dmb[rs_, cs_])
                dws[g] = dws[g] + _dot_nt(dmb[rs_, cs_], vb[rs_, cs_])
                dbs[g] = dbs[g] + dmixed[rs_, cs_]
        for g in range(4):
            _acc(dws_ref.at[g], dws[g], first)
            _acc(dbs_ref.at[g], dbs[g], first)
        dsu_ref[...] = (dycin * mixed_s[...] * dug).astype(BF16)
        dv = dv_s[...]
        _acc(dlg_ref, jnp.sum(dv * vn, axis=0, keepdims=True), first)
        _acc(dlb_ref, jnp.sum(dv, axis=0, keepdims=True), first)
        dvn = dv * lg_ref[...]
        dcen = rs * (dvn - jnp.mean(dvn, axis=-1, keepdims=True) - vn * jnp.mean(dvn * vn, axis=-1, keepdims=True))
        dsv_ref[...] = (dcen * dvg).astype(BF16)

    row = pl.BlockSpec((tm, D_MODEL), lambda i: (i, 0))
    half = pl.BlockSpec((tm, 512), lambda i: (i, 0))
    col = lambda c: pl.BlockSpec((tm, 512), lambda i: (i, c))
    zmrow = pl.BlockSpec((tm, ZM_W), lambda i: (i, 0))
    sq = _full((4, CHUNK, CHUNK))
    in_specs = ([row, pl.BlockSpec((tm, LRU_W), lambda i: (i, 0)), col(4)] + _halo(tm, 16, SC_W, 5, nt) + _halo(tm, 16, SC_W, 6, nt)
                + [col(7), col(8), zmrow, row, row, row, row, row, row,
                   _full((LRU_W, D_MODEL)), _full((3, SC_W)), _full((SC_W, D_MODEL)), _full((1, SGU_W)), _full((1, SGU_W)),
                   sq, sq, _full((CHUNK, 4)), _full((SGU_W, D_MODEL)), _full((D_MODEL, D_MODEL)), _full((1, D_MODEL))])
    bf = lambda w: jax.ShapeDtypeStruct((s, w), BF16)
    return pl.pallas_call(
        body, name="mix_out_bwd", grid=(nt,),
        in_specs=in_specs,
        out_specs=[row, row, row, row, row, half, half, half, zmrow, row, half,
                   _full((1, D_MODEL)), _full((1, SGU_W)), _full((1, SGU_W)), sq, sq],
        out_shape=[bf(D_MODEL), bf(D_MODEL), bf(D_MODEL), bf(D_MODEL), bf(LRU_W), bf(SC_W), bf(SGU_W), bf(SGU_W), bf(ZM_W),
                   jax.ShapeDtypeStruct((s, LRU_W), F32), jax.ShapeDtypeStruct((s, SC_W), F32),
                   jax.ShapeDtypeStruct((1, D_MODEL), F32), jax.ShapeDtypeStruct((1, SGU_W), F32),
                   jax.ShapeDtypeStruct((1, SGU_W), F32), jax.ShapeDtypeStruct((4, CHUNK, CHUNK), F32),
                   jax.ShapeDtypeStruct((4, CHUNK, CHUNK), F32)],
        scratch_shapes=[pltpu.VMEM((tm, SGU_W), F32), pltpu.VMEM((tm, SGU_W), F32)],
        compiler_params=_cp(),
    )(dy, za, za, za, za, za, za, za, za, za, za, zm, hf, hr, ya, yb, yc, mx, lru_w_out, sc_conv_w, sc_w_out, ln_g, ln_b,
      w_s, w_s_t, b_s_t, sgu_w_out, w_o, post_g)


def _lru_bwd(za, h_dir, dhs, conv_w, conv_b, wa, ba, wx, bx, lam, rev, tm):
    s = za.shape[0]
    nt = s // tm
    back = not rev

    def body(x_ref, xp_ref, xn_ref, h_ref, hp_ref, hn_ref, dh_ref, cw_ref, cb_ref, wa_ref, ba_ref, wx_ref, bx_ref,
             lam_ref, dxc_ref, dwa_ref, dba_ref, dwx_ref, dbx_ref, dlam_ref,
             xc_s, ra_s, xa_s, a_s, m_s, l_s, hsh_s, c8_s, c_s):
        i = pl.program_id(0)
        first = i == 0
        ti = (nt - 1 - i) if back else i
        rows = lax.broadcasted_iota(jnp.int32, (tm, 1), 0)
        xcb = _lru_preact(x_ref, xp_ref, xn_ref, cw_ref, cb_ref, wa_ref, ba_ref, wx_ref, bx_ref, lam_ref, ti, nt, tm,
                          xc_s, ra_s, xa_s, c8_s)
        hprev, hnext = _edges(hp_ref, hn_ref, ti, nt)
        hsh_s[...] = _shift(h_ref[...].astype(F32), 1 if rev else -1, hprev, hnext, rows)

        def gate_blk(j, carry):
            rws = pl.ds(pl.multiple_of(j * LRU_RB, LRU_RB), LRU_RB)
            r = _sigmoid(ra_s[rws, :])
            a, mult = _lru_decay(c8_s[...], r)
            ra_s[rws, :] = r
            xa_s[rws, :] = _sigmoid(xa_s[rws, :])
            a_s[rws, :] = a
            m_s[rws, :] = mult
            return carry

        lax.fori_loop(0, tm // LRU_RB, gate_blk, 0)

        @pl.when(first)
        def _():
            c_s[...] = jnp.zeros_like(c_s)

        nb = tm // 8

        def blk(j, c):
            jb = (nb - 1 - j) if back else j
            base = pl.multiple_of(jb * 8, 8)
            for q in range(8):
                t = base + ((7 - q) if back else q)
                lt = dh_ref[pl.ds(t, 1), :] + c
                l_s[pl.ds(t, 1), :] = lt
                c = a_s[pl.ds(t, 1), :] * lt
            return c

        c_s[...] = lax.fori_loop(0, nb, blk, c_s[...])

        def grad_blk(j, sums):
            s_lam, s_ba, s_bx = sums
            rws = pl.ds(pl.multiple_of(j * LRU_RB, LRU_RB), LRU_RB)
            du = l_s[rws, :]
            a = a_s[rws, :]
            r = ra_s[rws, :]
            ig = xa_s[rws, :]
            mult = m_s[rws, :]
            xc = xc_s[rws, :]
            t1 = du * mult
            dla = du * hsh_s[rws, :] * a - (du * ig * xc) * (a * a) / mult
            dlr = dla * r
            drp = dlr * c8_s[...] * (1.0 - r)
            dip = (t1 * xc) * ig * (1.0 - ig)
            l_s[rws, :] = t1 * ig
            ra_s[rws, :] = drp
            xa_s[rws, :] = dip
            fold = lambda v: sum(v[8 * q:8 * q + 8] for q in range(1, LRU_RB // 8)) + v[0:8]
            return s_lam + fold(dlr), s_ba + fold(drp), s_bx + fold(dip)

        zero8 = jnp.zeros((8, LRU_W), F32)
        s_lam, s_ba, s_bx = lax.fori_loop(0, tm // LRU_RB, grad_blk, (zero8, zero8, zero8))
        _acc(dlam_ref, jnp.sum(s_lam, axis=0, keepdims=True), first)
        _acc(dba_ref, jnp.sum(s_ba, axis=0, keepdims=True), first)
        _acc(dbx_ref, jnp.sum(s_bx, axis=0, keepdims=True), first)
        drb = ra_s[...].astype(BF16)
        dib = xa_s[...].astype(BF16)
        for h in range(4):
            cs_ = slice(LRU_HD * h, LRU_HD * (h + 1))
            dxc_ref[:, cs_] = l_s[:, cs_] + _dot_nt(drb[:, cs_], wa_ref[h]) + _dot_nt(dib[:, cs_], wx_ref[h])
            _acc(dwa_ref.at[h], _dot_tn(xcb[:, cs_], drb[:, cs_]), first)
            _acc(dwx_ref.at[h], _dot_tn(xcb[:, cs_], dib[:, cs_]), first)

        @pl.when(i == nt - 1)
        def _():
            dlam_ref[...] = dlam_ref[...] * (LRU_C * _sigmoid(-lam_ref[...]))

    vec = _full((1, LRU_W))
    hd = _full((4, LRU_HD, LRU_HD))
    tix = (lambda i: nt - 1 - i) if back else (lambda i: i)
    rowspec = pl.BlockSpec((tm, LRU_W), lambda i: (tix(i), 0))
    return pl.pallas_call(
        body, name="lru_bwd_rev" if rev else "lru_bwd", grid=(nt,),
        in_specs=_halo(tm, 16, LRU_W, 1, nt, back) + _halo(tm, 16, LRU_W, 0, nt, back) + [rowspec, _full((4, LRU_W)), vec, hd, vec, hd, vec, vec],
        out_specs=[rowspec, hd, vec, hd, vec, vec],
        out_shape=[jax.ShapeDtypeStruct((s, LRU_W), F32), jax.ShapeDtypeStruct((4, LRU_HD, LRU_HD), F32),
                   jax.ShapeDtypeStruct((1, LRU_W), F32), jax.ShapeDtypeStruct((4, LRU_HD, LRU_HD), F32),
                   jax.ShapeDtypeStruct((1, LRU_W), F32), jax.ShapeDtypeStruct((1, LRU_W), F32)],
        scratch_shapes=[pltpu.VMEM((tm, LRU_W), F32)] * 7 + [pltpu.VMEM((LRU_RB, LRU_W), F32), pltpu.VMEM((1, LRU_W), F32)],
        compiler_params=_cp(),
    )(za, za, za, h_dir, h_dir, h_dir, dhs, conv_w, conv_b, wa, ba, wx, bx, lam)


def _mix_in_bwd(dy, x, za, dxc_f, dxc_r, dcp, dgate, dscb, dsu, dsv, dzm, pre_g, lru_conv_w, sc_conv_w, w_in, tm):
    s = x.shape[0]
    nt = s // tm
    cw = 512

    def body(dy_ref, x_ref, lx_ref, lxp_ref, lxn_ref, scc_ref, sccp_ref, sccn_ref, scx_ref, scxp_ref, scxn_ref,
             df_ref, dfp_ref, dfn_ref, dr_ref, drp_ref, drn_ref, dcp_ref, dcpp_ref, dcpn_ref,
             dgate_ref, dscb_ref, dsu_ref, dsv_ref, dzm_ref, pg_ref, lcw_ref, scw_ref, w_ref,
             dx_ref, dz_ref, dpg_ref, dlcw_ref, dlcb_ref, dscw_ref):
        ti = pl.program_id(0)
        first = ti == 0
        rows = lax.broadcasted_iota(jnp.int32, (tm, 1), 0)
        fp, fn = _edges(dfp_ref, dfn_ref, ti, nt)
        rp, rn = _edges(drp_ref, drn_ref, ti, nt)
        dxc = df_ref[...] + dr_ref[...]
        dprev, dnext = fp + rp, fn + rn
        dlx = lcw_ref[2:3, :] * dxc
        dlx = dlx + lcw_ref[0:1, :] * _shift(dxc, 2, dprev, dnext, rows)
        dlx = dlx + lcw_ref[1:2, :] * _shift(dxc, 1, dprev, dnext, rows)
        dlx = dlx + lcw_ref[3:4, :] * _shift(dxc, -1, dprev, dnext, rows)
        lprev, lnext = _edges(lxp_ref, lxn_ref, ti, nt)
        lx = lx_ref[...].astype(F32)
        _acc(dlcb_ref, jnp.sum(dxc, axis=0, keepdims=True), first)
        for k, d in enumerate((-2, -1, 0, 1)):
            _acc(dlcw_ref.at[pl.ds(k, 1), :], jnp.sum(dxc * _shift(lx, d, lprev, lnext, rows), axis=0, keepdims=True), first)
        cv, xv, p, pm1, pp1, _ = _sc_conv(scc_ref, sccp_ref, sccn_ref, scx_ref, scxp_ref, scxn_ref, scw_ref, ti, nt, rows)
        cprev, cnext = _edges(dcpp_ref, dcpn_ref, ti, nt)
        dcp_v = dcp_ref[...]
        dp = (scw_ref[1:2, :] * dcp_v + scw_ref[0:1, :] * _shift(dcp_v, 1, cprev, cnext, rows)
              + scw_ref[2:3, :] * _shift(dcp_v, -1, cprev, cnext, rows))
        for k, pk in enumerate((pm1, p, pp1)):
            _acc(dscw_ref.at[pl.ds(k, 1), :], jnp.sum(dcp_v * pk, axis=0, keepdims=True), first)
        dz_ref[:, 0:1024] = dgate_ref[...]
        dz_ref[:, 1024:2048] = dlx.astype(BF16)
        dz_ref[:, 2048:2560] = dscb_ref[...]
        dz_ref[:, 2560:3072] = (dp * xv).astype(BF16)
        dz_ref[:, 3072:3584] = (dp * cv).astype(BF16)
        dz_ref[:, 3584:4096] = dsu_ref[...]
        dz_ref[:, 4096:4608] = dsv_ref[...]
        dz_ref[:, 4608:7680] = dzm_ref[...]
        dh = _dot(dz_ref[...], w_ref[...])
        dxn, dpg = _rms_bwd(dh, x_ref[...], pg_ref[...])
        dx_ref[...] = dy_ref[...] + dxn
        _acc(dpg_ref, dpg, first)

    row = pl.BlockSpec((tm, D_MODEL), lambda i: (i, 0))
    half = pl.BlockSpec((tm, 512), lambda i: (i, 0))
    in_specs = ([row, row] + _halo(tm, 16, LRU_W, 1, nt) + _halo(tm, 16, SC_W, 5, nt) + _halo(tm, 16, SC_W, 6, nt)
                + _halo(tm, 8, LRU_W, 0, nt) + _halo(tm, 8, LRU_W, 0, nt) + _halo(tm, 8, SC_W, 0, nt)
                + [row, half, half, half, pl.BlockSpec((tm, ZM_W), lambda i: (i, 0)),
                   _full((1, D_MODEL)), _full((4, LRU_W)), _full((3, SC_W)), _res((D_IN, D_MODEL))])
    return pl.pallas_call(
        body, name="mix_in_bwd", grid=(nt,),
        in_specs=in_specs,
        out_specs=[row, pl.BlockSpec((tm, D_IN), lambda i: (i, 0)), _full((1, D_MODEL)), _full((4, LRU_W)),
                   _full((1, LRU_W)), _full((3, SC_W))],
        out_shape=[jax.ShapeDtypeStruct((s, D_MODEL), F32), jax.ShapeDtypeStruct((s, D_IN), BF16),
                   jax.ShapeDtypeStruct((1, D_MODEL), F32), jax.ShapeDtypeStruct((4, LRU_W), F32),
                   jax.ShapeDtypeStruct((1, LRU_W), F32), jax.ShapeDtypeStruct((3, SC_W), F32)],
        compiler_params=_cp(),
    )(dy, x, za, za, za, za, za, za, za, za, za, dxc_f, dxc_f, dxc_f, dxc_r, dxc_r, dxc_r, dcp, dcp, dcp,
      dgate, dscb, dsu, dsv, dzm, pre_g, lru_conv_w, sc_conv_w, w_in)


def _loss_head(y, target, tm):
    s = y.shape[0]
    nt = s // tm

    def body(y_ref, t_ref, dy_ref, acc_ref):
        err = y_ref[...] - t_ref[...]
        dy_ref[...] = err * (1.0 / D_MODEL)
        _acc(acc_ref, jnp.sum(err * err, axis=0, keepdims=True), pl.program_id(0) == 0)

    row = pl.BlockSpec((tm, D_MODEL), lambda i: (i, 0))
    return pl.pallas_call(
        body, name="loss_head", grid=(nt,), in_specs=[row, row], out_specs=[row, _full((1, D_MODEL))],
        out_shape=[jax.ShapeDtypeStruct((s, D_MODEL), F32), jax.ShapeDtypeStruct((1, D_MODEL), F32)],
        compiler_params=_cp(),
    )(y, target)


def _row_tile(rows, cols):
    cap = max(8, (2 ** 18) // cols)
    best = None
    for t in range(8, min(rows, cap) + 1, 8):
        if rows % t == 0:
            best = t
    return best if best is not None else rows


def _adamw(w, g, m, v, name):
    shape = w.shape
    cols = shape[-1]
    rows = int(np.prod(shape[:-1]))
    tr = _row_tile(rows, cols)
    bc1 = 1.0 - ADAM_B1 ** ADAM_STEP
    bc2 = 1.0 - ADAM_B2 ** ADAM_STEP

    def body(w_ref, g_ref, m_ref, v_ref, d_ref, nm_ref, nv_ref):
        gv = g_ref[...]
        mn = ADAM_B1 * m_ref[...] + (1.0 - ADAM_B1) * gv
        vn = ADAM_B2 * v_ref[...] + (1.0 - ADAM_B2) * (gv * gv)
        nm_ref[...] = mn
        nv_ref[...] = vn
        d_ref[...] = -ADAM_LR * ((mn / bc1) / (jnp.sqrt(vn / bc2) + ADAM_EPS) + ADAM_WD * w_ref[...])

    spec = pl.BlockSpec((tr, cols), lambda i: (i, 0))
    sds = jax.ShapeDtypeStruct((rows, cols), F32)
    outs = pl.pallas_call(
        body, name="adamw_" + name, grid=(rows // tr,), in_specs=[spec] * 4, out_specs=[spec] * 3,
        out_shape=[sds, sds, sds], compiler_params=_cp(("parallel",)),
    )(w.reshape(rows, cols), g.reshape(rows, cols), m.reshape(rows, cols), v.reshape(rows, cols))
    return [o.reshape(shape) for o in outs]


HBM = pl.BlockSpec(memory_space=pl.ANY)


def _place():
    x, y, c = lax.axis_index("x"), lax.axis_index("y"), lax.axis_index("c")
    chips = [(1 - x, y), (x, 1 - y), (1 - x, 1 - y)]
    return x, y, c, chips


def _all_gather(buf, name):
    r, cdim = buf.shape
    r2 = r // 2

    def body(b_ref, o_ref, ssem, rsem):
        x, y, c, chips = _place()
        me = 2 * x + y
        sib = (x, y, 1 - c)
        mine = pl.ds(pl.multiple_of(c * r2, 16), r2)
        other = pl.ds(pl.multiple_of((1 - c) * r2, 16), r2)

        def rc(k, src, dst, to):
            return pltpu.make_async_remote_copy(src_ref=src, dst_ref=dst, send_sem=ssem.at[k], recv_sem=rsem.at[k],
                                                device_id=to, device_id_type=MESH)

        sent = []
        for j, (cx, cy) in enumerate(chips):
            cp = rc(j, b_ref.at[mine], o_ref.at[me, mine], (cx, cy, c))
            cp.start()
            sent.append(cp)
        for j, (cx, cy) in enumerate(chips):
            blk = o_ref.at[2 * cx + cy, mine]
            rc(j, blk, blk, (cx, cy, c)).wait_recv()
            cp = rc(3 + j, blk, blk, sib)
            cp.start()
            sent.append(cp)
        for j, (cx, cy) in enumerate(chips):
            blk = o_ref.at[2 * cx + cy, other]
            rc(3 + j, blk, blk, sib).wait_recv()
        for cp in sent:
            cp.wait_send()

    got = pl.pallas_call(
        body, name=name, in_specs=[HBM], out_specs=HBM,
        out_shape=jax.ShapeDtypeStruct((N_SHARD, r, cdim), buf.dtype),
        scratch_shapes=[pltpu.SemaphoreType.DMA((6,)), pltpu.SemaphoreType.DMA((6,))],
    )(buf)
    me = 2 * lax.axis_index("x") + lax.axis_index("y")
    return lax.dynamic_update_slice(got, buf[None], (me, 0, 0))


def _pair_swap(send, name):
    def body(s_ref, r_ref, ssem, rsem):
        x, y, c, _ = _place()
        cp = pltpu.make_async_remote_copy(src_ref=s_ref, dst_ref=r_ref, send_sem=ssem, recv_sem=rsem,
                                          device_id=(x, y, 1 - c), device_id_type=MESH)
        cp.start()
        cp.wait()

    return pl.pallas_call(
        body, name=name, in_specs=[HBM], out_specs=HBM, out_shape=jax.ShapeDtypeStruct(send.shape, send.dtype),
        scratch_shapes=[pltpu.SemaphoreType.DMA, pltpu.SemaphoreType.DMA],
    )(send)


def _chip_exchange(blocks, name):
    def body(b_ref, o_ref, ssem, rsem):
        x, y, c, chips = _place()
        me = 2 * x + y
        sent = []
        for j, (cx, cy) in enumerate(chips):
            cp = pltpu.make_async_remote_copy(src_ref=b_ref.at[2 * cx + cy], dst_ref=o_ref.at[me], send_sem=ssem.at[j],
                                              recv_sem=rsem.at[j], device_id=(cx, cy, c), device_id_type=MESH)
            cp.start()
            sent.append(cp)
        for j, (cx, cy) in enumerate(chips):
            blk = o_ref.at[2 * cx + cy]
            pltpu.make_async_remote_copy(src_ref=blk, dst_ref=blk, send_sem=ssem.at[j], recv_sem=rsem.at[j],
                                         device_id=(cx, cy, c), device_id_type=MESH).wait_recv()
        for cp in sent:
            cp.wait_send()

    got = pl.pallas_call(
        body, name=name, in_specs=[HBM], out_specs=HBM, out_shape=jax.ShapeDtypeStruct(blocks.shape, blocks.dtype),
        scratch_shapes=[pltpu.SemaphoreType.DMA((3,)), pltpu.SemaphoreType.DMA((3,))],
    )(blocks)
    me = 2 * lax.axis_index("x") + lax.axis_index("y")
    own = lax.dynamic_slice_in_dim(blocks, me, 1, axis=0)
    return lax.dynamic_update_slice_in_dim(got, own, me, axis=0)


def _pair_gather(half, name):
    c = lax.axis_index("c")
    got = _pair_swap(half, name)
    return jnp.stack([jnp.where(c == 0, half, got), jnp.where(c == 0, got, half)])


def _pair_sum(keep, got):
    n, r2, cdim = keep.shape

    def body(a_ref, b_ref, o_ref):
        o_ref[...] = (a_ref[...].astype(F32) + b_ref[...].astype(F32)).astype(BF16)

    spec = pl.BlockSpec((1, RS_TILE, cdim), lambda k, i: (k, i, 0))
    return pl.pallas_call(
        body, name="rs_pair_sum", grid=(n, r2 // RS_TILE), in_specs=[spec, spec], out_specs=spec,
        out_shape=jax.ShapeDtypeStruct(keep.shape, BF16), compiler_params=_cp(("parallel", "parallel")),
    )(keep, got)


def _chip_sum(slots):
    n, r2, cdim = slots.shape

    def body(s_ref, o_ref):
        o_ref[...] = ((s_ref[0].astype(F32) + s_ref[1].astype(F32)) + s_ref[2].astype(F32)) + s_ref[3].astype(F32)

    return pl.pallas_call(
        body, name="rs_chip_sum", grid=(r2 // RS_TILE,),
        in_specs=[pl.BlockSpec((n, RS_TILE, cdim), lambda i: (0, i, 0))],
        out_specs=pl.BlockSpec((RS_TILE, cdim), lambda i: (i, 0)),
        out_shape=jax.ShapeDtypeStruct((r2, cdim), F32), compiler_params=_cp(("parallel",)),
    )(slots)


def _reduce_scatter(packed, tag):
    r2 = packed.shape[1] // 2
    c = lax.axis_index("c")
    keep = lax.dynamic_slice_in_dim(packed, c * r2, r2, axis=1)
    send = lax.dynamic_slice_in_dim(packed, (1 - c) * r2, r2, axis=1)
    got = _pair_swap(send, "rs_pair_swap" + tag)
    slots = _chip_exchange(_pair_sum(keep, got), "rs_chip_exchange" + tag)
    both = _pair_gather(_chip_sum(slots), "rs_pair_gather" + tag)
    return both.reshape(2 * r2, packed.shape[2])


TRANSPOSED = ("ffn1_w_gate", "ffn1_w_up", "ffn2_w_gate", "ffn2_w_up", "w_in")
BIG = [("ffn1_w_gate", (F_SH, D_MODEL)), ("ffn1_w_up", (F_SH, D_MODEL)), ("ffn1_w_down", (F_SH, D_MODEL)),
       ("ffn2_w_gate", (F_SH, D_MODEL)), ("ffn2_w_up", (F_SH, D_MODEL)), ("ffn2_w_down", (F_SH, D_MODEL)),
       ("w_in", (D_IN_SH, D_MODEL)), ("lru_wa", (2, 4, 64, LRU_HD)), ("lru_wx", (2, 4, 64, LRU_HD)),
       ("lru_w_out", (256, D_MODEL)), ("sc_w_out", (SC_W, 256)), ("sgu_w_out", (SGU_W, 256)), ("w_o", (256, D_MODEL))]
SMALL = [("lru_conv_w", (4, 256)), ("lru_ba", (2, 256)), ("lru_bx", (2, 256)), ("lru_lambda", (2, 256)),
         ("sc_conv_w", (3, 128))]
REPL = [("ffn1_pre_g", (D_MODEL,)), ("ffn1_post_g", (D_MODEL,)), ("mix_pre_g", (D_MODEL,)), ("lru_conv_b", (LRU_W,)),
        ("sgu_ln_g", (SGU_W,)), ("sgu_ln_b", (SGU_W,)), ("sgu_w_s", (4, CHUNK, CHUNK)), ("sgu_b", (4, CHUNK)),
        ("mix_post_g", (D_MODEL,)), ("ffn2_pre_g", (D_MODEL,)), ("ffn2_post_g", (D_MODEL,))]
WEIGHTS = ['ffn1_pre_g', 'ffn1_w_gate', 'ffn1_w_up', 'ffn1_w_down', 'ffn1_post_g', 'mix_pre_g', 'w_in', 'lru_conv_w',
           'lru_conv_b', 'lru_wa', 'lru_ba', 'lru_wx', 'lru_bx', 'lru_lambda', 'lru_w_out', 'sc_conv_w', 'sc_w_out',
           'sgu_ln_g', 'sgu_ln_b', 'sgu_w_s', 'sgu_b', 'sgu_w_out', 'w_o', 'mix_post_g', 'ffn2_pre_g', 'ffn2_w_gate',
           'ffn2_w_up', 'ffn2_w_down', 'ffn2_post_g']


def _seg_rows(shape):
    return -(-int(np.prod(shape)) // PACK_C)


def _pack(parts, rows_total, dtype):
    lead = parts[0].shape[0]
    segs = []
    used = 0
    for p in parts:
        flat = p.reshape(lead, -1).astype(dtype)
        nr = -(-flat.shape[1] // PACK_C)
        pad = nr * PACK_C - flat.shape[1]
        if pad:
            flat = jnp.pad(flat, ((0, 0), (0, pad)))
        segs.append(flat.reshape(lead, nr, PACK_C))
        used += nr
    if rows_total > used:
        segs.append(jnp.zeros((lead, rows_total - used, PACK_C), dtype))
    return jnp.concatenate(segs, axis=1)


def _unpack(buf, specs):
    lead = buf.shape[0]
    out = {}
    r0 = 0
    for name, shape in specs:
        nr = _seg_rows(shape)
        n = int(np.prod(shape))
        out[name] = buf[:, r0:r0 + nr].reshape(lead, nr * PACK_C)[:, :n].reshape((lead,) + tuple(shape))
        r0 += nr
    return out


def _full_from_shards(name, t):
    if name in ("ffn1_w_gate", "ffn1_w_up", "ffn1_w_down", "ffn2_w_gate", "ffn2_w_up", "ffn2_w_down"):
        return t
    if name == "w_in":
        return t.reshape(D_IN, D_MODEL)
    if name in ("sc_w_out", "sgu_w_out", "lru_conv_w", "lru_ba", "lru_bx", "lru_lambda", "sc_conv_w"):
        return jnp.moveaxis(t, 0, -2).reshape(t.shape[1:-1] + (N_SHARD * t.shape[-1],))
    if name in ("lru_wa", "lru_wx"):
        return jnp.moveaxis(t, 0, 2).reshape(2, 4, LRU_HD, LRU_HD)
    if name in ("lru_w_out", "w_o"):
        return t.reshape(N_SHARD * t.shape[1], t.shape[2])
    raise ValueError(name)


def _shards_from_full(name, gfull):
    if name in ("ffn1_w_gate", "ffn1_w_up", "ffn1_w_down", "ffn2_w_gate", "ffn2_w_up", "ffn2_w_down"):
        return gfull
    if name == "w_in":
        return gfull.reshape(N_SHARD, D_IN_SH, D_MODEL)
    if name in ("sc_w_out", "sgu_w_out", "lru_conv_w", "lru_ba", "lru_bx", "lru_lambda", "sc_conv_w"):
        lastdim = gfull.shape[-1] // N_SHARD
        return jnp.moveaxis(gfull.reshape(gfull.shape[:-1] + (N_SHARD, lastdim)), -2, 0)
    if name in ("lru_wa", "lru_wx"):
        return jnp.moveaxis(gfull.reshape(2, 4, N_SHARD, 64, LRU_HD), 2, 0)
    if name in ("lru_w_out", "w_o"):
        return gfull.reshape(N_SHARD, gfull.shape[0] // N_SHARD, gfull.shape[1])
    raise ValueError(name)


def _tiles(s):
    return dict(ffn=min(512, s), ffn_bwd=min(512, s), tn=min(2048, s), mix_in=min(512, s), lru=min(512, s), mix=min(256, s), loss=min(512, s))


def _mixer_fwd(x, w, t):
    hb, za, zm = _mix_in_fwd(x, w["mix_pre_g"], w["w_in"], t["mix_in"])
    hf = _lru_fwd(za, w["lru_conv_w"], w["lru_conv_b"], w["lru_wa"][0], w["lru_ba"][0:1], w["lru_wx"][0],
                  w["lru_bx"][0:1], w["lru_lambda"][0:1], False, t["lru"])
    hr = _lru_fwd(za, w["lru_conv_w"], w["lru_conv_b"], w["lru_wa"][1], w["lru_ba"][1:2], w["lru_wx"][1],
                  w["lru_bx"][1:2], w["lru_lambda"][1:2], True, t["lru"])
    out, yain, q, ycin, ya, yb, yc, mb, mx = _mix_out_fwd(
        x, za, zm, hf, hr, w["lru_w_out"], w["sc_conv_w"], w["sc_w_out"], w["sgu_ln_g"], w["sgu_ln_b"], w["sgu_w_s"],
        w["sgu_b_t"], w["sgu_w_out"], w["w_o"], w["mix_post_g"], t["mix"])
    return out, dict(x=x, hb=hb, za=za, zm=zm, hf=hf, hr=hr, yain=yain, q=q, ycin=ycin, ya=ya, yb=yb, yc=yc, mb=mb, mx=mx)


def _mixer_bwd(dy, sv, w, t):
    g = {}
    (dmx, dya, dyb, dyc, dgate, dscb, dsu, dsv, dzm, dhs, dcp, g["mix_post_g"], g["sgu_ln_g"], g["sgu_ln_b"],
     g["sgu_w_s"], dbs) = _mix_out_bwd(
        dy, sv["za"], sv["zm"], sv["hf"], sv["hr"], sv["ya"], sv["yb"], sv["yc"], sv["mx"], w["lru_w_out"], w["sc_conv_w"],
        w["sc_w_out"], w["sgu_ln_g"], w["sgu_ln_b"], w["sgu_w_s"], w["sgu_w_s_t"], w["sgu_b_t"], w["sgu_w_out"], w["w_o"],
        w["mix_post_g"], t["mix"])
    g["sgu_b"] = jnp.sum(dbs, axis=-1)
    ts = t["tn"]
    g["w_o"] = _tn_plain(sv["mb"], dmx, "dw_o", ts)
    g["lru_w_out"] = _tn_plain(sv["yain"], dya, "dw_lru_out", ts)
    g["sc_w_out"] = _tn_plain(sv["q"], dyb, "dw_sc_out", ts)
    g["sgu_w_out"] = _tn_plain(sv["ycin"], dyc, "dw_sgu_out", ts)
    dxc, dwa, dba, dwx, dbx, dlam = [], [], [], [], [], []
    for d, rev in enumerate((False, True)):
        o = _lru_bwd(sv["za"], sv["hr"] if rev else sv["hf"], dhs, w["lru_conv_w"], w["lru_conv_b"], w["lru_wa"][d],
                     w["lru_ba"][d:d + 1], w["lru_wx"][d], w["lru_bx"][d:d + 1], w["lru_lambda"][d:d + 1], rev, t["lru"])
        for lst, val in zip((dxc, dwa, dba, dwx, dbx, dlam), o):
            lst.append(val)
    g["lru_wa"] = jnp.stack(dwa)
    g["lru_wx"] = jnp.stack(dwx)
    g["lru_ba"] = jnp.concatenate(dba, axis=0)
    g["lru_bx"] = jnp.concatenate(dbx, axis=0)
    g["lru_lambda"] = jnp.concatenate(dlam, axis=0)
    dx, dz, g["mix_pre_g"], g["lru_conv_w"], g["lru_conv_b"], g["sc_conv_w"] = _mix_in_bwd(
        dy, sv["x"], sv["za"], dxc[0], dxc[1], dcp, dgate, dscb, dsu, dsv, dzm, w["mix_pre_g"], w["lru_conv_w"],
        w["sc_conv_w"], w["w_in"], t["mix"])
    g["w_in"] = _tn(dz, sv["hb"], "dw_in", N_SHARD, (ts, D_IN_SH), lambda b, s_: (s_, b), (ts, D_MODEL), lambda b, s_: (s_, 0),
                    (D_IN, D_MODEL), (D_IN_SH, D_MODEL), lambda b, s_: (b, 0), ts)
    return dx, g


def kernel(x, ffn1_pre_g, ffn1_w_gate, ffn1_w_up, ffn1_w_down, ffn1_post_g, mix_pre_g, w_in, lru_conv_w, lru_conv_b, lru_wa, lru_ba, lru_wx, lru_bx, lru_lambda, lru_w_out, sc_conv_w, sc_w_out, sgu_ln_g, sgu_ln_b, sgu_w_s, sgu_b, sgu_w_out, w_o, mix_post_g, ffn2_pre_g, ffn2_w_gate, ffn2_w_up, ffn2_w_down, ffn2_post_g, loss_target, m_ffn1_pre_g, m_ffn1_w_gate, m_ffn1_w_up, m_ffn1_w_down, m_ffn1_post_g, m_mix_pre_g, m_w_in, m_lru_conv_w, m_lru_conv_b, m_lru_wa, m_lru_ba, m_lru_wx, m_lru_bx, m_lru_lambda, m_lru_w_out, m_sc_conv_w, m_sc_w_out, m_sgu_ln_g, m_sgu_ln_b, m_sgu_w_s, m_sgu_b, m_sgu_w_out, m_w_o, m_mix_post_g, m_ffn2_pre_g, m_ffn2_w_gate, m_ffn2_w_up, m_ffn2_w_down, m_ffn2_post_g, v_ffn1_pre_g, v_ffn1_w_gate, v_ffn1_w_up, v_ffn1_w_down, v_ffn1_post_g, v_mix_pre_g, v_w_in, v_lru_conv_w, v_lru_conv_b, v_lru_wa, v_lru_ba, v_lru_wx, v_lru_bx, v_lru_lambda, v_lru_w_out, v_sc_conv_w, v_sc_w_out, v_sgu_ln_g, v_sgu_ln_b, v_sgu_w_s, v_sgu_b, v_sgu_w_out, v_w_o, v_mix_post_g, v_ffn2_pre_g, v_ffn2_w_gate, v_ffn2_w_up, v_ffn2_w_down, v_ffn2_post_g):
    args = (ffn1_pre_g, ffn1_w_gate, ffn1_w_up, ffn1_w_down, ffn1_post_g, mix_pre_g, w_in, lru_conv_w, lru_conv_b, lru_wa, lru_ba, lru_wx, lru_bx, lru_lambda, lru_w_out, sc_conv_w, sc_w_out, sgu_ln_g, sgu_ln_b, sgu_w_s, sgu_b, sgu_w_out, w_o, mix_post_g, ffn2_pre_g, ffn2_w_gate, ffn2_w_up, ffn2_w_down, ffn2_post_g)
    margs = (m_ffn1_pre_g, m_ffn1_w_gate, m_ffn1_w_up, m_ffn1_w_down, m_ffn1_post_g, m_mix_pre_g, m_w_in, m_lru_conv_w, m_lru_conv_b, m_lru_wa, m_lru_ba, m_lru_wx, m_lru_bx, m_lru_lambda, m_lru_w_out, m_sc_conv_w, m_sc_w_out, m_sgu_ln_g, m_sgu_ln_b, m_sgu_w_s, m_sgu_b, m_sgu_w_out, m_w_o, m_mix_post_g, m_ffn2_pre_g, m_ffn2_w_gate, m_ffn2_w_up, m_ffn2_w_down, m_ffn2_post_g)
    vargs = (v_ffn1_pre_g, v_ffn1_w_gate, v_ffn1_w_up, v_ffn1_w_down, v_ffn1_post_g, v_mix_pre_g, v_w_in, v_lru_conv_w, v_lru_conv_b, v_lru_wa, v_lru_ba, v_lru_wx, v_lru_bx, v_lru_lambda, v_lru_w_out, v_sc_conv_w, v_sc_w_out, v_sgu_ln_g, v_sgu_ln_b, v_sgu_w_s, v_sgu_b, v_sgu_w_out, v_w_o, v_mix_post_g, v_ffn2_pre_g, v_ffn2_w_gate, v_ffn2_w_up, v_ffn2_w_down, v_ffn2_post_g)
    wsh = dict(zip(WEIGHTS, args))
    msh = dict(zip(WEIGHTS, margs))
    vsh = dict(zip(WEIGHTS, vargs))
    xs = x[0]
    s = xs.shape[0]
    t = _tiles(s)

    small = _pack([wsh[n].reshape(1, -1) for n, _ in SMALL], SMALL_ROWS, F32)[0]
    small_all = _all_gather(small, "ag_small")
    small_un = _unpack(small_all, [(n, (DEPTH,) + sh) for n, sh in SMALL])
    layers = []
    for l in range(DEPTH):
        packed = _pack([(wsh[n][l].T if n in TRANSPOSED else wsh[n][l])[None] for n, _ in BIG], AG_ROWS, BF16)[0]
        gathered = _all_gather(packed, "ag_weights_l%d" % l)
        un = _unpack(gathered, BIG)
        w = {n: _full_from_shards(n, un[n]) for n, _ in BIG}
        for n, _ in SMALL:
            w[n] = _full_from_shards(n, small_un[n][:, l])
        for n, _ in REPL:
            w[n] = wsh[n][l]
        for n in ("ffn1_pre_g", "ffn1_post_g", "mix_pre_g", "lru_conv_b", "sgu_ln_g", "sgu_ln_b", "mix_post_g", "ffn2_pre_g",
                  "ffn2_post_g"):
            w[n] = w[n].reshape(1, -1)
        w["sgu_w_s_t"] = jnp.swapaxes(w["sgu_w_s"], 1, 2).astype(BF16)
        w["sgu_w_s"] = w["sgu_w_s"].astype(BF16)
        w["sgu_b_t"] = w["sgu_b"].T
        layers.append(w)

    saved = []
    cur = xs
    for l in range(DEPTH):
        w = layers[l]
        sv = {}
        x1, *rest = _ffn_fwd(cur, w["ffn1_pre_g"], w["ffn1_w_gate"], w["ffn1_w_up"], w["ffn1_w_down"], w["ffn1_post_g"], t["ffn"])
        sv["ffn1"] = (cur, *rest)
        x2, sv["mix"] = _mixer_fwd(x1, w, t)
        x3, *rest = _ffn_fwd(x2, w["ffn2_pre_g"], w["ffn2_w_gate"], w["ffn2_w_up"], w["ffn2_w_down"], w["ffn2_post_g"], t["ffn"])
        sv["ffn2"] = (x2, *rest)
        saved.append(sv)
        cur = x3

    dy, sq = _loss_head(cur, loss_target[0], t["loss"])
    loss = lax.psum(0.5 * jnp.sum(sq) / D_MODEL, ("x", "y", "c"))

    grads = [None] * DEPTH
    for l in reversed(range(DEPTH)):
        w = layers[l]
        sv = saved[l]
        g = {}
        for tag in ("ffn2", "mix", "ffn1"):
            if tag == "mix":
                dy, gm = _mixer_bwd(dy, sv["mix"], w, t)
                g.update(gm)
                continue
            xin, ab, d1, d2, ff = sv[tag]
            dy, hb, dgb, dub, dfb, g[tag + "_pre_g"], g[tag + "_post_g"] = _ffn_bwd(
                dy, xin, d1, d2, ff, w[tag + "_pre_g"], w[tag + "_post_g"], w[tag + "_w_gate"], w[tag + "_w_up"],
                w[tag + "_w_down"], t["ffn_bwd"])
            g[tag + "_w_gate"], g[tag + "_w_up"], g[tag + "_w_down"] = _ffn_wgrads(hb, ab, dgb, dub, dfb, t["tn"])
        parts = [_shards_from_full(n, g[n]) for n, _ in BIG] + [_shards_from_full(n, g[n]) for n, _ in SMALL]
        parts += [jnp.broadcast_to(g[n].reshape((1,) + sh), (N_SHARD,) + sh) for n, sh in REPL]
        total = _reduce_scatter(_pack(parts, RS_ROWS, BF16), "_l%d" % l)
        un = _unpack(total[None], BIG + SMALL + REPL)
        grads[l] = {n: (un[n][0].T if n in TRANSPOSED else un[n][0]) for n in WEIGHTS}
    grad_x = dy[None]

    gw, dw, nm, nv = [], [], [], []
    for n in WEIGHTS:
        gfull = jnp.stack([grads[l][n] for l in range(DEPTH)])
        d_, m_, v_ = _adamw(wsh[n], gfull, msh[n], vsh[n], n)
        gw.append(gfull)
        dw.append(d_)
        nm.append(m_)
        nv.append(v_)
    return (loss, grad_x, *gw, *dw, *nm, *nv)
```

```python
import functools

import jax
import jax.numpy as jnp
import numpy as np
from jax import lax
from jax.experimental import pallas as pl
from jax.experimental.pallas import tpu as pltpu

F32 = jnp.float32
BF16 = jnp.bfloat16
MESH = pl.DeviceIdType.MESH

D_MODEL = 1024
D_FF = 2816
N_SHARD = 4
F_SH = D_FF // N_SHARD
D_IN = 7680
D_IN_SH = D_IN // N_SHARD
ZA_W = 4608
ZM_W = 3072
LRU_W = 1024
LRU_HD = 256
SC_W = 512
SGU_W = 512
CHUNK = 128
DEPTH = 4
EPS = 1e-6
LRU_C = 8.0
LRU_RB = 16

ADAM_LR, ADAM_B1, ADAM_B2, ADAM_EPS, ADAM_WD, ADAM_STEP = 0.001, 0.9, 0.999, 1e-08, 0.01, 10

VMEM_LIMIT = 56 * 2 ** 20
PACK_C = 1024
AG_ROWS = 7168
RS_ROWS = 7680
RS_TILE = 192
SMALL_ROWS = 32


def _cp(sem=("arbitrary",)):
    return pltpu.CompilerParams(dimension_semantics=sem, vmem_limit_bytes=VMEM_LIMIT)


def _full(shape):
    return pl.BlockSpec(shape, lambda *_: (0,) * len(shape))


def _res(shape):
    return pl.BlockSpec(shape, lambda *_: (0,) * len(shape), pipeline_mode=pl.Buffered(1))


def _dot(a, b):
    return jnp.dot(a, b, preferred_element_type=F32)


def _dot_nt(a, b):
    return lax.dot_general(a, b, (((1,), (1,)), ((), ())), preferred_element_type=F32)


def _dot_tn(a, b):
    return lax.dot_general(a, b, (((0,), (0,)), ((), ())), preferred_element_type=F32)


def _sigmoid(x):
    return 0.5 * jnp.tanh(0.5 * x) + 0.5


_GELU_K = 0.7978845608028654
_GELU_C = 0.044715


def _gelu(x):
    return 0.5 * x * (1.0 + jnp.tanh(_GELU_K * (x + _GELU_C * x * x * x)))


def _gelu_and_grad(x):
    t = jnp.tanh(_GELU_K * (x + _GELU_C * x * x * x))
    g = 0.5 * x * (1.0 + t)
    dg = 0.5 * (1.0 + t) + 0.5 * x * (1.0 - t * t) * (_GELU_K * (1.0 + 3.0 * _GELU_C * x * x))
    return g, dg


def _rms_fwd(x, g):
    rs = lax.rsqrt(jnp.mean(x * x, axis=-1, keepdims=True) + EPS)
    return x * rs * g


def _rms_bwd(dy, x, g):
    rs = lax.rsqrt(jnp.mean(x * x, axis=-1, keepdims=True) + EPS)
    n = x * rs
    dn = dy * g
    dx = rs * (dn - n * jnp.mean(dn * n, axis=-1, keepdims=True))
    return dx, jnp.sum(dy * n, axis=0, keepdims=True)


def _acc(ref, val, first):
    @pl.when(first)
    def _():
        ref[...] = val

    @pl.when(jnp.logical_not(first))
    def _():
        ref[...] += val


def _shift(xm, d, prev, nxt, rows):
    tm = xm.shape[0]
    if d == 0:
        return xm
    y = pltpu.roll(xm, (-d) % tm, 0)
    rows8 = rows[0:8]
    if d < 0:
        hb = prev.shape[0]
        top = y[0:8]
        for r in range(-d):
            top = jnp.where(rows8 == r, prev[hb + r + d:hb + r + d + 1, :], top)
        return jnp.concatenate([top, y[8:]], axis=0)
    bot = y[tm - 8:]
    for r in range(d):
        bot = jnp.where(rows8 == 8 - d + r, nxt[r:r + 1, :], bot)
    return jnp.concatenate([y[:tm - 8], bot], axis=0)


def _halo(tm, hb, w, col, nt, rev=False):
    r = tm // hb
    last = nt * r - 1
    ti = (lambda i: nt - 1 - i) if rev else (lambda i: i)
    return [pl.BlockSpec((tm, w), lambda i: (ti(i), col)),
            pl.BlockSpec((hb, w), lambda i: (jnp.maximum(ti(i) * r - 1, 0), col)),
            pl.BlockSpec((hb, w), lambda i: (jnp.minimum((ti(i) + 1) * r, last), col))]


def _edges(prev_ref, next_ref, ti, nt):
    prev = jnp.where(ti > 0, prev_ref[...].astype(F32), 0.0)
    nxt = jnp.where(ti < nt - 1, next_ref[...].astype(F32), 0.0)
    return prev, nxt


def _ffn_fwd(x, pre_g, wg, wu, wd, post_g, tm):
    s = x.shape[0]
    nt = s // tm

    def body(x_ref, pg_ref, wg_ref, wu_ref, wd_ref, qg_ref, o_ref, a_ref, d1_ref, d2_ref, f_ref):
        xv = x_ref[...]
        hb = _rms_fwd(xv, pg_ref[...]).astype(BF16)
        f = jnp.zeros((tm, D_MODEL), F32)
        for k in range(N_SHARD):
            g = _dot_nt(hb, wg_ref[k])
            u = _dot_nt(hb, wu_ref[k])
            sg = _sigmoid(g)
            silu = g * sg
            ab = (silu * u).astype(BF16)
            a_ref[k] = ab
            d1_ref[k] = (u * (sg * (1.0 + g * (1.0 - sg)))).astype(BF16)
            d2_ref[k] = silu.astype(BF16)
            f = f + _dot(ab, wd_ref[k])
        f_ref[...] = f.astype(BF16)
        o_ref[...] = xv + 0.5 * _rms_fwd(f, qg_ref[...])

    row = pl.BlockSpec((tm, D_MODEL), lambda i: (i, 0))
    gu = pl.BlockSpec((N_SHARD, tm, F_SH), lambda i: (0, i, 0))
    return pl.pallas_call(
        body, name="ffn_fwd", grid=(nt,),
        in_specs=[row, _full((1, D_MODEL)), _res((N_SHARD, F_SH, D_MODEL)), _res((N_SHARD, F_SH, D_MODEL)),
                  _res((N_SHARD, F_SH, D_MODEL)), _full((1, D_MODEL))],
        out_specs=[row, gu, gu, gu, row],
        out_shape=[jax.ShapeDtypeStruct((s, D_MODEL), F32), jax.ShapeDtypeStruct((N_SHARD, s, F_SH), BF16),
                   jax.ShapeDtypeStruct((N_SHARD, s, F_SH), BF16), jax.ShapeDtypeStruct((N_SHARD, s, F_SH), BF16),
                   jax.ShapeDtypeStruct((s, D_MODEL), BF16)],
        compiler_params=_cp(("parallel",)),
    )(x, pre_g, wg, wu, wd, post_g)


def _ffn_bwd(dy, x, d1, d2, f, pre_g, post_g, wg, wu, wd, tm):
    s = x.shape[0]
    nt = s // tm

    def body_a(dy_ref, d1_ref, d2_ref, f_ref, qg_ref, wd_ref, dg_ref, du_ref, df_ref, dqg_ref):
        df, dq = _rms_bwd(0.5 * dy_ref[...], f_ref[...].astype(F32), qg_ref[...])
        dfb = df.astype(BF16)
        df_ref[...] = dfb
        for k in range(N_SHARD):
            da = _dot_nt(dfb, wd_ref[k])
            dg_ref[k] = (da * d1_ref[k].astype(F32)).astype(BF16)
            du_ref[k] = (da * d2_ref[k].astype(F32)).astype(BF16)
        _acc(dqg_ref, dq, pl.program_id(0) == 0)

    def body_b(dy_ref, x_ref, dg_ref, du_ref, pg_ref, wg_ref, wu_ref, dx_ref, h_ref, dpg_ref):
        xv = x_ref[...]
        h_ref[...] = _rms_fwd(xv, pg_ref[...]).astype(BF16)
        dh = jnp.zeros((tm, D_MODEL), F32)
        for k in range(N_SHARD):
            dh = dh + _dot(dg_ref[k], wg_ref[k]) + _dot(du_ref[k], wu_ref[k])
        dxn, dp = _rms_bwd(dh, xv, pg_ref[...])
        dx_ref[...] = dy_ref[...] + dxn
        _acc(dpg_ref, dp, pl.program_id(0) == 0)

    row = pl.BlockSpec((tm, D_MODEL), lambda i: (i, 0))
    gu = pl.BlockSpec((N_SHARD, tm, F_SH), lambda i: (0, i, 0))
    vec = _full((1, D_MODEL))
    wsp = _res((N_SHARD, F_SH, D_MODEL))
    big = jax.ShapeDtypeStruct((N_SHARD, s, F_SH), BF16)
    tok_bf = jax.ShapeDtypeStruct((s, D_MODEL), BF16)
    vec_f = jax.ShapeDtypeStruct((1, D_MODEL), F32)
    dg, du, df, dqg = pl.pallas_call(
        body_a, name="ffn_bwd_a", grid=(nt,),
        in_specs=[row, gu, gu, row, vec, wsp],
        out_specs=[gu, gu, row, vec],
        out_shape=[big, big, tok_bf, vec_f],
        compiler_params=_cp(),
    )(dy, d1, d2, f, post_g, wd)
    dx, h, dpg = pl.pallas_call(
        body_b, name="ffn_bwd_b", grid=(nt,),
        in_specs=[row, row, gu, gu, vec, wsp, wsp],
        out_specs=[row, row, vec],
        out_shape=[jax.ShapeDtypeStruct((s, D_MODEL), F32), tok_bf, vec_f],
        compiler_params=_cp(),
    )(dy, x, dg, du, pre_g, wg, wu)
    return dx, h, dg, du, df, dpg, dqg


def _tn(lhs, rhs, name, nb, lhs_blk, lhs_map, rhs_blk, rhs_map, out_shape, out_blk, out_map, ts, into=None):
    s = lhs.shape[-2]
    ns = s // ts
    acc_shape = tuple(d for d in out_blk if d is not None)

    def body(l_ref, r_ref, *rest):
        o_ref, acc_s = rest[-2:]
        t = pl.program_id(1)
        _acc(acc_s, _dot_tn(l_ref[...], r_ref[...]), t == 0)

        @pl.when(t == ns - 1)
        def _():
            o_ref[...] = acc_s[...].astype(BF16)

    in_specs = [pl.BlockSpec(lhs_blk, lhs_map), pl.BlockSpec(rhs_blk, rhs_map)]
    args = (lhs, rhs)
    aliases = {}
    if into is not None:
        buf, blk_row = into
        in_specs.append(HBM)
        args = (lhs, rhs, buf)
        aliases = {2: 0}
        out_shape = buf.shape
        out_map = lambda b, t: (b, blk_row, 0)
    return pl.pallas_call(
        body, name=name, grid=(nb, ns),
        in_specs=in_specs,
        out_specs=pl.BlockSpec(out_blk, out_map),
        out_shape=jax.ShapeDtypeStruct(out_shape, BF16),
        scratch_shapes=[pltpu.VMEM(acc_shape, F32)],
        input_output_aliases=aliases,
        compiler_params=_cp(("parallel", "arbitrary")),
    )(*args)


def _tn_plain(lhs, rhs, name, ts):
    m, n = lhs.shape[1], rhs.shape[1]
    return _tn(lhs, rhs, name, 1, (ts, m), lambda b, t: (t, 0), (ts, n), lambda b, t: (t, 0),
               (m, n), (m, n), lambda b, t: (0, 0), ts)


def _ffn_wgrads(h, a, dg, du, df, ts, buf, blk0):
    lm2 = lambda b, t: (t, 0)
    bm3 = lambda b, t: (b, t, 0)
    for j, (lhs, rhs, name) in enumerate(((dg, h, "ffn_dwg"), (du, h, "ffn_dwu"), (a, df, "ffn_dwd"))):
        buf = _tn(lhs, rhs, name, N_SHARD, (None, ts, F_SH), bm3, (ts, D_MODEL), lm2, None, (None, F_SH, D_MODEL), None, ts,
                  into=(buf, blk0 + j))
    return buf


def _mix_in_fwd(x, pre_g, w_in, tm):
    s = x.shape[0]
    nt = s // tm
    cw = 512

    def body(x_ref, pg_ref, w_ref, h_ref, za_ref, zm_ref):
        hb = _rms_fwd(x_ref[...], pg_ref[...]).astype(BF16)
        h_ref[...] = hb
        for j in range(ZA_W // cw):
            za_ref[:, j * cw:(j + 1) * cw] = _dot_nt(hb, w_ref[j * cw:(j + 1) * cw, :]).astype(BF16)
        for j in range(ZM_W // cw):
            zm_ref[:, j * cw:(j + 1) * cw] = _dot_nt(hb, w_ref[ZA_W + j * cw:ZA_W + (j + 1) * cw, :]).astype(BF16)

    row = pl.BlockSpec((tm, D_MODEL), lambda i: (i, 0))
    return pl.pallas_call(
        body, name="mix_in_fwd", grid=(nt,),
        in_specs=[row, _full((1, D_MODEL)), _res((D_IN, D_MODEL))],
        out_specs=[row, pl.BlockSpec((tm, ZA_W), lambda i: (i, 0)), pl.BlockSpec((tm, ZM_W), lambda i: (i, 0))],
        out_shape=[jax.ShapeDtypeStruct((s, D_MODEL), BF16), jax.ShapeDtypeStruct((s, ZA_W), BF16),
                   jax.ShapeDtypeStruct((s, ZM_W), BF16)],
        compiler_params=_cp(("parallel",)),
    )(x, pre_g, w_in)


def _lru_conv(xm, prev, nxt, rows, cw_ref, cb_ref):
    acc = cb_ref[...] + cw_ref[2:3, :] * xm
    acc = acc + cw_ref[0:1, :] * _shift(xm, -2, prev, nxt, rows)
    acc = acc + cw_ref[1:2, :] * _shift(xm, -1, prev, nxt, rows)
    acc = acc + cw_ref[3:4, :] * _shift(xm, 1, prev, nxt, rows)
    return acc


def _lru_preact(x_ref, xp_ref, xn_ref, cw_ref, cb_ref, wa_ref, ba_ref, wx_ref, bx_ref, lam_ref, ti, nt, tm,
                xc_s, ra_s, xa_s, c8_s):
    rows = lax.broadcasted_iota(jnp.int32, (tm, 1), 0)
    prev, nxt = _edges(xp_ref, xn_ref, ti, nt)
    xc = _lru_conv(x_ref[...].astype(F32), prev, nxt, rows, cw_ref, cb_ref)
    xc_s[...] = xc
    xcb = xc.astype(BF16)
    for h in range(4):
        cs_ = slice(LRU_HD * h, LRU_HD * (h + 1))
        ra_s[:, cs_] = _dot(xcb[:, cs_], wa_ref[h]) + ba_ref[:, cs_]
        xa_s[:, cs_] = _dot(xcb[:, cs_], wx_ref[h]) + bx_ref[:, cs_]
    lam = lam_ref[...]
    e = jnp.exp(-jnp.abs(lam))
    log1p_e = jnp.where(e < 1e-2, e * (1.0 - e * (0.5 - e * (1.0 / 3.0))), jnp.log(1.0 + e))
    c8_s[...] = jnp.broadcast_to(-LRU_C * (jnp.maximum(-lam, 0.0) + log1p_e), c8_s.shape)
    return xcb


def _lru_decay(cl, r):
    la = cl * r
    a = jnp.exp(la)
    y2 = 2.0 * la
    em = jnp.where(y2 > -0.004, y2 * (-1.0 + y2 * (-0.5 - y2 * (1.0 / 6.0))), 1.0 - a * a)
    return a, jnp.sqrt(em)


def _scan_rows(a_s, b_s, o_s, carry, tm, descending):
    nb = tm // 8

    def blk(j, h):
        jb = (nb - 1 - j) if descending else j
        base = pl.multiple_of(jb * 8, 8)
        for r in range(8):
            t = base + ((7 - r) if descending else r)
            h = a_s[pl.ds(t, 1), :] * h + b_s[pl.ds(t, 1), :]
            o_s[pl.ds(t, 1), :] = h
        return h

    return lax.fori_loop(0, nb, blk, carry)


def _lru_fwd(za, conv_w, conv_b, wa, ba, wx, bx, lam, rev, tm):
    s = za.shape[0]
    nt = s // tm

    def body(x_ref, xp_ref, xn_ref, cw_ref, cb_ref, wa_ref, ba_ref, wx_ref, bx_ref, lam_ref, h_ref,
             xc_s, ra_s, xa_s, o_s, c8_s, c_s):
        i = pl.program_id(0)
        ti = (nt - 1 - i) if rev else i
        _lru_preact(x_ref, xp_ref, xn_ref, cw_ref, cb_ref, wa_ref, ba_ref, wx_ref, bx_ref, lam_ref, ti, nt, tm,
                    xc_s, ra_s, xa_s, c8_s)

        def gate_blk(j, carry):
            rws = pl.ds(pl.multiple_of(j * LRU_RB, LRU_RB), LRU_RB)
            r = _sigmoid(ra_s[rws, :])
            ig = _sigmoid(xa_s[rws, :])
            a, mult = _lru_decay(c8_s[...], r)
            ra_s[rws, :] = a
            xa_s[rws, :] = ig * xc_s[rws, :] * mult
            return carry

        lax.fori_loop(0, tm // LRU_RB, gate_blk, 0)

        @pl.when(i == 0)
        def _():
            c_s[...] = jnp.zeros_like(c_s)

        c_s[...] = _scan_rows(ra_s, xa_s, o_s, c_s[...], tm, rev)
        h_ref[...] = o_s[...].astype(BF16)

    vec = _full((1, LRU_W))
    hd = _full((4, LRU_HD, LRU_HD))
    ti = (lambda i: nt - 1 - i) if rev else (lambda i: i)
    tile = pltpu.VMEM((tm, LRU_W), F32)
    return pl.pallas_call(
        body, name="lru_fwd_rev" if rev else "lru_fwd", grid=(nt,),
        in_specs=_halo(tm, 16, LRU_W, 1, nt, rev) + [_full((4, LRU_W)), vec, hd, vec, hd, vec, vec],
        out_specs=pl.BlockSpec((tm, LRU_W), lambda i: (ti(i), 0)),
        out_shape=jax.ShapeDtypeStruct((s, LRU_W), BF16),
        scratch_shapes=[tile, tile, tile, tile, pltpu.VMEM((LRU_RB, LRU_W), F32), pltpu.VMEM((1, LRU_W), F32)],
        compiler_params=_cp(),
    )(za, za, za, conv_w, conv_b, wa, ba, wx, bx, lam)


def _sc_conv(c_ref, cp_ref, cn_ref, x_ref, xp_ref, xn_ref, w_ref, ti, nt, rows):
    cprev, cnext = _edges(cp_ref, cn_ref, ti, nt)
    xprev, xnext = _edges(xp_ref, xn_ref, ti, nt)
    cv = c_ref[...].astype(F32)
    xv = x_ref[...].astype(F32)
    p = cv * xv
    pm1 = _shift(p, -1, cprev * xprev, cnext * xnext, rows)
    pp1 = _shift(p, 1, cprev * xprev, cnext * xnext, rows)
    conv = w_ref[0:1, :] * pm1 + w_ref[1:2, :] * p + w_ref[2:3, :] * pp1
    return cv, xv, p, pm1, pp1, conv


def _sgu_norm(vz, g_ref, b_ref):
    vg, dvg = _gelu_and_grad(vz)
    mu = jnp.mean(vg, axis=-1, keepdims=True)
    cen = vg - mu
    rs = lax.rsqrt(jnp.mean(cen * cen, axis=-1, keepdims=True) + EPS)
    vn = cen * rs
    return dvg, vn, rs, vn * g_ref[...] + b_ref[...]


def _mix_out_fwd(x, za, zm, hf, hr, lru_w_out, sc_conv_w, sc_w_out, ln_g, ln_b, w_s, b_s_t, sgu_w_out, w_o, post_g, tm):
    s = x.shape[0]
    nt = s // tm
    nc = tm // CHUNK

    def body(x_ref, gate_ref, scb_ref, scc_ref, sccp_ref, sccn_ref, scx_ref, scxp_ref, scxn_ref, su_ref, sv_ref,
             zm_ref, hf_ref, hr_ref, wlo_ref, scw_ref, wso_ref, lg_ref, lb_ref, ws_ref, bs_ref, wgo_ref, wo_ref,
             qg_ref, o_ref, yain_ref, q_ref, ycin_ref, ya_ref, yb_ref, yc_ref, m_ref, mx_ref, mixed_s):
        ti = pl.program_id(0)
        rows = lax.broadcasted_iota(jnp.int32, (tm, 1), 0)
        hs = hf_ref[...].astype(F32) + hr_ref[...].astype(F32)
        yain = (hs * _gelu(gate_ref[...].astype(F32))).astype(BF16)
        yain_ref[...] = yain
        ya = _dot(yain, wlo_ref[...])
        _, _, _, _, _, conv = _sc_conv(scc_ref, sccp_ref, sccn_ref, scx_ref, scxp_ref, scxn_ref, scw_ref, ti, nt, rows)
        qb = (scb_ref[...].astype(F32) * conv).astype(BF16)
        q_ref[...] = qb
        yb = _dot(qb, wso_ref[...])
        _, _, _, v = _sgu_norm(sv_ref[...].astype(F32), lg_ref, lb_ref)
        vb = v.astype(BF16)
        for n in range(nc):
            for g in range(4):
                blk = vb[n * CHUNK:(n + 1) * CHUNK, g * CHUNK:(g + 1) * CHUNK]
                mixed_s[n * CHUNK:(n + 1) * CHUNK, g * CHUNK:(g + 1) * CHUNK] = _dot(ws_ref[g], blk) + bs_ref[:, g:g + 1]
        ycin = (_gelu(su_ref[...].astype(F32)) * mixed_s[...]).astype(BF16)
        ycin_ref[...] = ycin
        yc = _dot(ycin, wgo_ref[...])
        m = (_sigmoid(zm_ref[:, 0:D_MODEL].astype(F32)) * ya + _sigmoid(zm_ref[:, D_MODEL:2 * D_MODEL].astype(F32)) * yb
             + _sigmoid(zm_ref[:, 2 * D_MODEL:3 * D_MODEL].astype(F32)) * yc)
        mb = m.astype(BF16)
        mx = _dot(mb, wo_ref[...])
        ya_ref[...] = ya.astype(BF16)
        yb_ref[...] = yb.astype(BF16)
        yc_ref[...] = yc.astype(BF16)
        m_ref[...] = mb
        mx_ref[...] = mx.astype(BF16)
        o_ref[...] = x_ref[...] + _rms_fwd(mx, qg_ref[...])

    row = pl.BlockSpec((tm, D_MODEL), lambda i: (i, 0))
    half = pl.BlockSpec((tm, 512), lambda i: (i, 0))
    col = lambda c: pl.BlockSpec((tm, 512), lambda i: (i, c))
    in_specs = ([row, pl.BlockSpec((tm, LRU_W), lambda i: (i, 0)), col(4)] + _halo(tm, 16, SC_W, 5, nt) + _halo(tm, 16, SC_W, 6, nt)
                + [col(7), col(8), pl.BlockSpec((tm, ZM_W), lambda i: (i, 0)), row, row,
                   _full((LRU_W, D_MODEL)), _full((3, SC_W)), _full((SC_W, D_MODEL)), _full((1, SGU_W)), _full((1, SGU_W)),
                   _full((4, CHUNK, CHUNK)), _full((CHUNK, 4)), _full((SGU_W, D_MODEL)), _full((D_MODEL, D_MODEL)),
                   _full((1, D_MODEL))])
    bf = lambda w: jax.ShapeDtypeStruct((s, w), BF16)
    return pl.pallas_call(
        body, name="mix_out_fwd", grid=(nt,),
        in_specs=in_specs,
        out_specs=[row, row, half, half, row, row, row, row, row],
        out_shape=[jax.ShapeDtypeStruct((s, D_MODEL), F32), bf(LRU_W), bf(SC_W), bf(SGU_W), bf(D_MODEL), bf(D_MODEL),
                   bf(D_MODEL), bf(D_MODEL), bf(D_MODEL)],
        scratch_shapes=[pltpu.VMEM((tm, SGU_W), F32)],
        compiler_params=_cp(("parallel",)),
    )(x, za, za, za, za, za, za, za, za, za, za, zm, hf, hr, lru_w_out, sc_conv_w, sc_w_out, ln_g, ln_b, w_s, b_s_t,
      sgu_w_out, w_o, post_g)


def _mix_out_bwd(dy, za, zm, hf, hr, ya, yb, yc, mx, lru_w_out, sc_conv_w, sc_w_out, ln_g, ln_b, w_s, w_s_t, b_s_t,
                 sgu_w_out, w_o, post_g, tm):
    s = dy.shape[0]
    nt = s // tm
    nc = tm // CHUNK

    def body(dy_ref, gate_ref, scb_ref, scc_ref, sccp_ref, sccn_ref, scx_ref, scxp_ref, scxn_ref, su_ref, sv_ref,
             zm_ref, hf_ref, hr_ref, ya_ref, yb_ref, yc_ref, mx_ref, wlo_ref, scw_ref, wso_ref, lg_ref, lb_ref,
             ws_ref, wst_ref, bs_ref, wgo_ref, wo_ref, qg_ref,
             dmx_ref, dya_ref, dyb_ref, dyc_ref, dgate_ref, dscb_ref, dsu_ref, dsv_ref, dzm_ref, dhs_ref, dcp_ref,
             dqg_ref, dlg_ref, dlb_ref, dws_ref, dbs_ref, mixed_s, dv_s):
        ti = pl.program_id(0)
        first = ti == 0
        rows = lax.broadcasted_iota(jnp.int32, (tm, 1), 0)
        dmx, dq = _rms_bwd(dy_ref[...], mx_ref[...].astype(F32), qg_ref[...])
        _acc(dqg_ref, dq, first)
        dmxb = dmx.astype(BF16)
        dmx_ref[...] = dmxb
        dm = _dot_nt(dmxb, wo_ref[...])
        dys = []
        for k, (y_ref, d_ref) in enumerate(((ya_ref, dya_ref), (yb_ref, dyb_ref), (yc_ref, dyc_ref))):
            gk = _sigmoid(zm_ref[:, k * D_MODEL:(k + 1) * D_MODEL].astype(F32))
            dyk = (dm * gk).astype(BF16)
            d_ref[...] = dyk
            dys.append(dyk)
            dzm_ref[:, k * D_MODEL:(k + 1) * D_MODEL] = (dm * y_ref[...].astype(F32) * gk * (1.0 - gk)).astype(BF16)
        dyain = _dot_nt(dys[0], wlo_ref[...])
        gg, dgg = _gelu_and_grad(gate_ref[...].astype(F32))
        hs = hf_ref[...].astype(F32) + hr_ref[...].astype(F32)
        dhs_ref[...] = dyain * gg
        dgate_ref[...] = (dyain * hs * dgg).astype(BF16)
        dq_b = _dot_nt(dys[1], wso_ref[...])
        _, _, _, _, _, conv = _sc_conv(scc_ref, sccp_ref, sccn_ref, scx_ref, scxp_ref, scxn_ref, scw_ref, ti, nt, rows)
        dscb_ref[...] = (dq_b * conv).astype(BF16)
        dcp_ref[...] = dq_b * scb_ref[...].astype(F32)
        dycin = _dot_nt(dys[2], wgo_ref[...])
        dvg, vn, rs, v = _sgu_norm(sv_ref[...].astype(F32), lg_ref, lb_ref)
        vb = v.astype(BF16)
        ug, dug = _gelu_and_grad(su_ref[...].astype(F32))
        dmixed = dycin * ug
        dmb = dmixed.astype(BF16)
        dws = [jnp.zeros((CHUNK, CHUNK), F32) for _ in range(4)]
        dbs = [jnp.zeros((CHUNK, CHUNK), F32) for _ in range(4)]
        for n in range(nc):
            for g in range(4):
                rs_, cs_ = slice(n * CHUNK, (n + 1) * CHUNK), slice(g * CHUNK, (g + 1) * CHUNK)
                mixed_s[rs_, cs_] = _dot(ws_ref[g], vb[rs_, cs_]) + bs_ref[:, g:g + 1]
                dv_s[rs_, cs_] = _dot(wst_ref[g], dmb[rs_, cs_])
                dws[g] = dws[g] + _dot_nt(dmb[rs_, cs_], vb[rs_, cs_])
                dbs[g] = dbs[g] + dmixed[rs_, cs_]
        for g in range(4):
            _acc(dws_ref.at[g], dws[g], first)
            _acc(dbs_ref.at[g], dbs[g], first)
        dsu_ref[...] = (dycin * mixed_s[...] * dug).astype(BF16)
        dv = dv_s[...]
        _acc(dlg_ref, jnp.sum(dv * vn, axis=0, keepdims=True), first)
        _acc(dlb_ref, jnp.sum(dv, axis=0, keepdims=True), first)
        dvn = dv * lg_ref[...]
        dcen = rs * (dvn - jnp.mean(dvn, axis=-1, keepdims=True) - vn * jnp.mean(dvn * vn, axis=-1, keepdims=True))
        dsv_ref[...] = (dcen * dvg).astype(BF16)

    row = pl.BlockSpec((tm, D_MODEL), lambda i: (i, 0))
    half = pl.BlockSpec((tm, 512), lambda i: (i, 0))
    col = lambda c: pl.BlockSpec((tm, 512), lambda i: (i, c))
    zmrow = pl.BlockSpec((tm, ZM_W), lambda i: (i, 0))
    sq = _full((4, CHUNK, CHUNK))
    in_specs = ([row, pl.BlockSpec((tm, LRU_W), lambda i: (i, 0)), col(4)] + _halo(tm, 16, SC_W, 5, nt) + _halo(tm, 16, SC_W, 6, nt)
                + [col(7), col(8), zmrow, row, row, row, row, row, row,
                   _full((LRU_W, D_MODEL)), _full((3, SC_W)), _full((SC_W, D_MODEL)), _full((1, SGU_W)), _full((1, SGU_W)),
                   sq, sq, _full((CHUNK, 4)), _full((SGU_W, D_MODEL)), _full((D_MODEL, D_MODEL)), _full((1, D_MODEL))])
    bf = lambda w: jax.ShapeDtypeStruct((s, w), BF16)
    return pl.pallas_call(
        body, name="mix_out_bwd", grid=(nt,),
        in_specs=in_specs,
        out_specs=[row, row, row, row, row, half, half, half, zmrow, row, half,
                   _full((1, D_MODEL)), _full((1, SGU_W)), _full((1, SGU_W)), sq, sq],
        out_shape=[bf(D_MODEL), bf(D_MODEL), bf(D_MODEL), bf(D_MODEL), bf(LRU_W), bf(SC_W), bf(SGU_W), bf(SGU_W), bf(ZM_W),
                   jax.ShapeDtypeStruct((s, LRU_W), F32), jax.ShapeDtypeStruct((s, SC_W), F32),
                   jax.ShapeDtypeStruct((1, D_MODEL), F32), jax.ShapeDtypeStruct((1, SGU_W), F32),
                   jax.ShapeDtypeStruct((1, SGU_W), F32), jax.ShapeDtypeStruct((4, CHUNK, CHUNK), F32),
                   jax.ShapeDtypeStruct((4, CHUNK, CHUNK), F32)],
        scratch_shapes=[pltpu.VMEM((tm, SGU_W), F32), pltpu.VMEM((tm, SGU_W), F32)],
        compiler_params=_cp(),
    )(dy, za, za, za, za, za, za, za, za, za, za, zm, hf, hr, ya, yb, yc, mx, lru_w_out, sc_conv_w, sc_w_out, ln_g, ln_b,
      w_s, w_s_t, b_s_t, sgu_w_out, w_o, post_g)


def _lru_bwd(za, h_dir, dhs, conv_w, conv_b, wa, ba, wx, bx, lam, rev, tm):
    s = za.shape[0]
    nt = s // tm
    back = not rev

    def body(x_ref, xp_ref, xn_ref, h_ref, hp_ref, hn_ref, dh_ref, cw_ref, cb_ref, wa_ref, ba_ref, wx_ref, bx_ref,
             lam_ref, dxc_ref, dwa_ref, dba_ref, dwx_ref, dbx_ref, dlam_ref,
             xc_s, ra_s, xa_s, a_s, m_s, l_s, hsh_s, c8_s, c_s):
        i = pl.program_id(0)
        first = i == 0
        ti = (nt - 1 - i) if back else i
        rows = lax.broadcasted_iota(jnp.int32, (tm, 1), 0)
        xcb = _lru_preact(x_ref, xp_ref, xn_ref, cw_ref, cb_ref, wa_ref, ba_ref, wx_ref, bx_ref, lam_ref, ti, nt, tm,
                          xc_s, ra_s, xa_s, c8_s)
        hprev, hnext = _edges(hp_ref, hn_ref, ti, nt)
        hsh_s[...] = _shift(h_ref[...].astype(F32), 1 if rev else -1, hprev, hnext, rows)

        def gate_blk(j, carry):
            rws = pl.ds(pl.multiple_of(j * LRU_RB, LRU_RB), LRU_RB)
            r = _sigmoid(ra_s[rws, :])
            a, mult = _lru_decay(c8_s[...], r)
            ra_s[rws, :] = r
            xa_s[rws, :] = _sigmoid(xa_s[rws, :])
            a_s[rws, :] = a
            m_s[rws, :] = mult
            return carry

        lax.fori_loop(0, tm // LRU_RB, gate_blk, 0)

        @pl.when(first)
        def _():
            c_s[...] = jnp.zeros_like(c_s)

        nb = tm // 8

        def blk(j, c):
            jb = (nb - 1 - j) if back else j
            base = pl.multiple_of(jb * 8, 8)
            for q in range(8):
                t = base + ((7 - q) if back else q)
                lt = dh_ref[pl.ds(t, 1), :] + c
                l_s[pl.ds(t, 1), :] = lt
                c = a_s[pl.ds(t, 1), :] * lt
            return c

        c_s[...] = lax.fori_loop(0, nb, blk, c_s[...])

        def grad_blk(j, sums):
            s_lam, s_ba, s_bx = sums
            rws = pl.ds(pl.multiple_of(j * LRU_RB, LRU_RB), LRU_RB)
            du = l_s[rws, :]
            a = a_s[rws, :]
            r = ra_s[rws, :]
            ig = xa_s[rws, :]
            mult = m_s[rws, :]
            xc = xc_s[rws, :]
            t1 = du * mult
            dla = du * hsh_s[rws, :] * a - (du * ig * xc) * (a * a) / mult
            dlr = dla * r
            drp = dlr * c8_s[...] * (1.0 - r)
            dip = (t1 * xc) * ig * (1.0 - ig)
            l_s[rws, :] = t1 * ig
            ra_s[rws, :] = drp
            xa_s[rws, :] = dip
            fold = lambda v: sum(v[8 * q:8 * q + 8] for q in range(1, LRU_RB // 8)) + v[0:8]
            return s_lam + fold(dlr), s_ba + fold(drp), s_bx + fold(dip)

        zero8 = jnp.zeros((8, LRU_W), F32)
        s_lam, s_ba, s_bx = lax.fori_loop(0, tm // LRU_RB, grad_blk, (zero8, zero8, zero8))
        _acc(dlam_ref, jnp.sum(s_lam, axis=0, keepdims=True), first)
        _acc(dba_ref, jnp.sum(s_ba, axis=0, keepdims=True), first)
        _acc(dbx_ref, jnp.sum(s_bx, axis=0, keepdims=True), first)
        drb = ra_s[...].astype(BF16)
        dib = xa_s[...].astype(BF16)
        for h in range(4):
            cs_ = slice(LRU_HD * h, LRU_HD * (h + 1))
            dxc_ref[:, cs_] = l_s[:, cs_] + _dot_nt(drb[:, cs_], wa_ref[h]) + _dot_nt(dib[:, cs_], wx_ref[h])
            _acc(dwa_ref.at[h], _dot_tn(xcb[:, cs_], drb[:, cs_]), first)
            _acc(dwx_ref.at[h], _dot_tn(xcb[:, cs_], dib[:, cs_]), first)

        @pl.when(i == nt - 1)
        def _():
            dlam_ref[...] = dlam_ref[...] * (LRU_C * _sigmoid(-lam_ref[...]))

    vec = _full((1, LRU_W))
    hd = _full((4, LRU_HD, LRU_HD))
    tix = (lambda i: nt - 1 - i) if back else (lambda i: i)
    rowspec = pl.BlockSpec((tm, LRU_W), lambda i: (tix(i), 0))
    return pl.pallas_call(
        body, name="lru_bwd_rev" if rev else "lru_bwd", grid=(nt,),
        in_specs=_halo(tm, 16, LRU_W, 1, nt, back) + _halo(tm, 16, LRU_W, 0, nt, back) + [rowspec, _full((4, LRU_W)), vec, hd, vec, hd, vec, vec],
        out_specs=[rowspec, hd, vec, hd, vec, vec],
        out_shape=[jax.ShapeDtypeStruct((s, LRU_W), F32), jax.ShapeDtypeStruct((4, LRU_HD, LRU_HD), F32),
                   jax.ShapeDtypeStruct((1, LRU_W), F32), jax.ShapeDtypeStruct((4, LRU_HD, LRU_HD), F32),
                   jax.ShapeDtypeStruct((1, LRU_W), F32), jax.ShapeDtypeStruct((1, LRU_W), F32)],
        scratch_shapes=[pltpu.VMEM((tm, LRU_W), F32)] * 7 + [pltpu.VMEM((LRU_RB, LRU_W), F32), pltpu.VMEM((1, LRU_W), F32)],
        compiler_params=_cp(),
    )(za, za, za, h_dir, h_dir, h_dir, dhs, conv_w, conv_b, wa, ba, wx, bx, lam)


def _mix_in_bwd(dy, x, za, dxc_f, dxc_r, dcp, dgate, dscb, dsu, dsv, dzm, pre_g, lru_conv_w, sc_conv_w, w_in, tm):
    s = x.shape[0]
    nt = s // tm
    cw = 512

    def body(dy_ref, x_ref, lx_ref, lxp_ref, lxn_ref, scc_ref, sccp_ref, sccn_ref, scx_ref, scxp_ref, scxn_ref,
             df_ref, dfp_ref, dfn_ref, dr_ref, drp_ref, drn_ref, dcp_ref, dcpp_ref, dcpn_ref,
             dgate_ref, dscb_ref, dsu_ref, dsv_ref, dzm_ref, pg_ref, lcw_ref, scw_ref, w_ref,
             dx_ref, dz_ref, dpg_ref, dlcw_ref, dlcb_ref, dscw_ref):
        ti = pl.program_id(0)
        first = ti == 0
        rows = lax.broadcasted_iota(jnp.int32, (tm, 1), 0)
        fp, fn = _edges(dfp_ref, dfn_ref, ti, nt)
        rp, rn = _edges(drp_ref, drn_ref, ti, nt)
        dxc = df_ref[...] + dr_ref[...]
        dprev, dnext = fp + rp, fn + rn
        dlx = lcw_ref[2:3, :] * dxc
        dlx = dlx + lcw_ref[0:1, :] * _shift(dxc, 2, dprev, dnext, rows)
        dlx = dlx + lcw_ref[1:2, :] * _shift(dxc, 1, dprev, dnext, rows)
        dlx = dlx + lcw_ref[3:4, :] * _shift(dxc, -1, dprev, dnext, rows)
        lprev, lnext = _edges(lxp_ref, lxn_ref, ti, nt)
        lx = lx_ref[...].astype(F32)
        _acc(dlcb_ref, jnp.sum(dxc, axis=0, keepdims=True), first)
        for k, d in enumerate((-2, -1, 0, 1)):
            _acc(dlcw_ref.at[pl.ds(k, 1), :], jnp.sum(dxc * _shift(lx, d, lprev, lnext, rows), axis=0, keepdims=True), first)
        cv, xv, p, pm1, pp1, _ = _sc_conv(scc_ref, sccp_ref, sccn_ref, scx_ref, scxp_ref, scxn_ref, scw_ref, ti, nt, rows)
        cprev, cnext = _edges(dcpp_ref, dcpn_ref, ti, nt)
        dcp_v = dcp_ref[...]
        dp = (scw_ref[1:2, :] * dcp_v + scw_ref[0:1, :] * _shift(dcp_v, 1, cprev, cnext, rows)
              + scw_ref[2:3, :] * _shift(dcp_v, -1, cprev, cnext, rows))
        for k, pk in enumerate((pm1, p, pp1)):
            _acc(dscw_ref.at[pl.ds(k, 1), :], jnp.sum(dcp_v * pk, axis=0, keepdims=True), first)
        dz_ref[:, 0:1024] = dgate_ref[...]
        dz_ref[:, 1024:2048] = dlx.astype(BF16)
        dz_ref[:, 2048:2560] = dscb_ref[...]
        dz_ref[:, 2560:3072] = (dp * xv).astype(BF16)
        dz_ref[:, 3072:3584] = (dp * cv).astype(BF16)
        dz_ref[:, 3584:4096] = dsu_ref[...]
        dz_ref[:, 4096:4608] = dsv_ref[...]
        dz_ref[:, 4608:7680] = dzm_ref[...]
        dh = _dot(dz_ref[...], w_ref[...])
        dxn, dpg = _rms_bwd(dh, x_ref[...], pg_ref[...])
        dx_ref[...] = dy_ref[...] + dxn
        _acc(dpg_ref, dpg, first)

    row = pl.BlockSpec((tm, D_MODEL), lambda i: (i, 0))
    half = pl.BlockSpec((tm, 512), lambda i: (i, 0))
    in_specs = ([row, row] + _halo(tm, 16, LRU_W, 1, nt) + _halo(tm, 16, SC_W, 5, nt) + _halo(tm, 16, SC_W, 6, nt)
                + _halo(tm, 8, LRU_W, 0, nt) + _halo(tm, 8, LRU_W, 0, nt) + _halo(tm, 8, SC_W, 0, nt)
                + [row, half, half, half, pl.BlockSpec((tm, ZM_W), lambda i: (i, 0)),
                   _full((1, D_MODEL)), _full((4, LRU_W)), _full((3, SC_W)), _res((D_IN, D_MODEL))])
    return pl.pallas_call(
        body, name="mix_in_bwd", grid=(nt,),
        in_specs=in_specs,
        out_specs=[row, pl.BlockSpec((tm, D_IN), lambda i: (i, 0)), _full((1, D_MODEL)), _full((4, LRU_W)),
                   _full((1, LRU_W)), _full((3, SC_W))],
        out_shape=[jax.ShapeDtypeStruct((s, D_MODEL), F32), jax.ShapeDtypeStruct((s, D_IN), BF16),
                   jax.ShapeDtypeStruct((1, D_MODEL), F32), jax.ShapeDtypeStruct((4, LRU_W), F32),
                   jax.ShapeDtypeStruct((1, LRU_W), F32), jax.ShapeDtypeStruct((3, SC_W), F32)],
        compiler_params=_cp(),
    )(dy, x, za, za, za, za, za, za, za, za, za, dxc_f, dxc_f, dxc_f, dxc_r, dxc_r, dxc_r, dcp, dcp, dcp,
      dgate, dscb, dsu, dsv, dzm, pre_g, lru_conv_w, sc_conv_w, w_in)


def _loss_head(y, target, tm):
    s = y.shape[0]
    nt = s // tm

    def body(y_ref, t_ref, dy_ref, acc_ref):
        err = y_ref[...] - t_ref[...]
        dy_ref[...] = err * (1.0 / D_MODEL)
        _acc(acc_ref, jnp.sum(err * err, axis=0, keepdims=True), pl.program_id(0) == 0)

    row = pl.BlockSpec((tm, D_MODEL), lambda i: (i, 0))
    return pl.pallas_call(
        body, name="loss_head", grid=(nt,), in_specs=[row, row], out_specs=[row, _full((1, D_MODEL))],
        out_shape=[jax.ShapeDtypeStruct((s, D_MODEL), F32), jax.ShapeDtypeStruct((1, D_MODEL), F32)],
        compiler_params=_cp(),
    )(y, target)


def _row_tile(rows, cols):
    cap = max(8, (2 ** 18) // cols)
    best = None
    for t in range(8, min(rows, cap) + 1, 8):
        if rows % t == 0:
            best = t
    return best if best is not None else rows


def _adamw(w, g, m, v, name):
    shape = w.shape
    cols = shape[-1]
    rows = int(np.prod(shape[:-1]))
    tr = _row_tile(rows, cols)
    bc1 = 1.0 - ADAM_B1 ** ADAM_STEP
    bc2 = 1.0 - ADAM_B2 ** ADAM_STEP

    def body(w_ref, g_ref, m_ref, v_ref, d_ref, nm_ref, nv_ref):
        gv = g_ref[...]
        mn = ADAM_B1 * m_ref[...] + (1.0 - ADAM_B1) * gv
        vn = ADAM_B2 * v_ref[...] + (1.0 - ADAM_B2) * (gv * gv)
        nm_ref[...] = mn
        nv_ref[...] = vn
        d_ref[...] = -ADAM_LR * ((mn / bc1) / (jnp.sqrt(vn / bc2) + ADAM_EPS) + ADAM_WD * w_ref[...])

    spec = pl.BlockSpec((tr, cols), lambda i: (i, 0))
    sds = jax.ShapeDtypeStruct((rows, cols), F32)
    outs = pl.pallas_call(
        body, name="adamw_" + name, grid=(rows // tr,), in_specs=[spec] * 4, out_specs=[spec] * 3,
        out_shape=[sds, sds, sds], compiler_params=_cp(("parallel",)),
    )(w.reshape(rows, cols), g.reshape(rows, cols), m.reshape(rows, cols), v.reshape(rows, cols))
    return [o.reshape(shape) for o in outs]


HBM = pl.BlockSpec(memory_space=pl.ANY)


def _place():
    x, y, c = lax.axis_index("x"), lax.axis_index("y"), lax.axis_index("c")
    chips = [(1 - x, y), (x, 1 - y), (1 - x, 1 - y)]
    return x, y, c, chips


def _all_gather(buf, name):
    r, cdim = buf.shape
    r2 = r // 2

    def body(b_ref, o_ref, ssem, rsem):
        x, y, c, chips = _place()
        me = 2 * x + y
        sib = (x, y, 1 - c)
        mine = pl.ds(pl.multiple_of(c * r2, 16), r2)
        other = pl.ds(pl.multiple_of((1 - c) * r2, 16), r2)

        def rc(k, src, dst, to):
            return pltpu.make_async_remote_copy(src_ref=src, dst_ref=dst, send_sem=ssem.at[k], recv_sem=rsem.at[k],
                                                device_id=to, device_id_type=MESH)

        sent = []
        for j, (cx, cy) in enumerate(chips):
            cp = rc(j, b_ref.at[mine], o_ref.at[me, mine], (cx, cy, c))
            cp.start()
            sent.append(cp)
        for j, (cx, cy) in enumerate(chips):
            blk = o_ref.at[2 * cx + cy, mine]
            rc(j, blk, blk, (cx, cy, c)).wait_recv()
            cp = rc(3 + j, blk, blk, sib)
            cp.start()
            sent.append(cp)
        for j, (cx, cy) in enumerate(chips):
            blk = o_ref.at[2 * cx + cy, other]
            rc(3 + j, blk, blk, sib).wait_recv()
        for cp in sent:
            cp.wait_send()

    got = pl.pallas_call(
        body, name=name, in_specs=[HBM], out_specs=HBM,
        out_shape=jax.ShapeDtypeStruct((N_SHARD, r, cdim), buf.dtype),
        scratch_shapes=[pltpu.SemaphoreType.DMA((6,)), pltpu.SemaphoreType.DMA((6,))],
    )(buf)
    me = 2 * lax.axis_index("x") + lax.axis_index("y")
    return lax.dynamic_update_slice(got, buf[None], (me, 0, 0))


def _pair_swap(send, name):
    def body(s_ref, r_ref, ssem, rsem):
        x, y, c, _ = _place()
        cp = pltpu.make_async_remote_copy(src_ref=s_ref, dst_ref=r_ref, send_sem=ssem, recv_sem=rsem,
                                          device_id=(x, y, 1 - c), device_id_type=MESH)
        cp.start()
        cp.wait()

    return pl.pallas_call(
        body, name=name, in_specs=[HBM], out_specs=HBM, out_shape=jax.ShapeDtypeStruct(send.shape, send.dtype),
        scratch_shapes=[pltpu.SemaphoreType.DMA, pltpu.SemaphoreType.DMA],
    )(send)


def _chip_exchange(blocks, name):
    def body(b_ref, o_ref, ssem, rsem):
        x, y, c, chips = _place()
        me = 2 * x + y
        sent = []
        for j, (cx, cy) in enumerate(chips):
            cp = pltpu.make_async_remote_copy(src_ref=b_ref.at[2 * cx + cy], dst_ref=o_ref.at[me], send_sem=ssem.at[j],
                                              recv_sem=rsem.at[j], device_id=(cx, cy, c), device_id_type=MESH)
            cp.start()
            sent.append(cp)
        for j, (cx, cy) in enumerate(chips):
            blk = o_ref.at[2 * cx + cy]
            pltpu.make_async_remote_copy(src_ref=blk, dst_ref=blk, send_sem=ssem.at[j], recv_sem=rsem.at[j],
                                         device_id=(cx, cy, c), device_id_type=MESH).wait_recv()
        for cp in sent:
            cp.wait_send()

    got = pl.pallas_call(
        body, name=name, in_specs=[HBM], out_specs=HBM, out_shape=jax.ShapeDtypeStruct(blocks.shape, blocks.dtype),
        scratch_shapes=[pltpu.SemaphoreType.DMA((3,)), pltpu.SemaphoreType.DMA((3,))],
    )(blocks)
    me = 2 * lax.axis_index("x") + lax.axis_index("y")
    own = lax.dynamic_slice_in_dim(blocks, me, 1, axis=0)
    return lax.dynamic_update_slice_in_dim(got, own, me, axis=0)


def _pair_gather(half, name):
    c = lax.axis_index("c")
    got = _pair_swap(half, name)
    return jnp.stack([jnp.where(c == 0, half, got), jnp.where(c == 0, got, half)])


def _pair_sum(keep, got):
    n, r2, cdim = keep.shape

    def body(a_ref, b_ref, o_ref):
        o_ref[...] = (a_ref[...].astype(F32) + b_ref[...].astype(F32)).astype(BF16)

    spec = pl.BlockSpec((1, RS_TILE, cdim), lambda k, i: (k, i, 0))
    return pl.pallas_call(
        body, name="rs_pair_sum", grid=(n, r2 // RS_TILE), in_specs=[spec, spec], out_specs=spec,
        out_shape=jax.ShapeDtypeStruct(keep.shape, BF16), compiler_params=_cp(("parallel", "parallel")),
    )(keep, got)


def _chip_sum(slots):
    n, r2, cdim = slots.shape

    def body(s_ref, o_ref):
        o_ref[...] = ((s_ref[0].astype(F32) + s_ref[1].astype(F32)) + s_ref[2].astype(F32)) + s_ref[3].astype(F32)

    return pl.pallas_call(
        body, name="rs_chip_sum", grid=(r2 // RS_TILE,),
        in_specs=[pl.BlockSpec((n, RS_TILE, cdim), lambda i: (0, i, 0))],
        out_specs=pl.BlockSpec((RS_TILE, cdim), lambda i: (i, 0)),
        out_shape=jax.ShapeDtypeStruct((r2, cdim), F32), compiler_params=_cp(("parallel",)),
    )(slots)


def _reduce_scatter(packed, tag):
    r2 = packed.shape[1] // 2
    c = lax.axis_index("c")
    keep = lax.dynamic_slice_in_dim(packed, c * r2, r2, axis=1)
    send = lax.dynamic_slice_in_dim(packed, (1 - c) * r2, r2, axis=1)
    got = _pair_swap(send, "rs_pair_swap" + tag)
    slots = _chip_exchange(_pair_sum(keep, got), "rs_chip_exchange" + tag)
    both = _pair_gather(_chip_sum(slots), "rs_pair_gather" + tag)
    return both.reshape(2 * r2, packed.shape[2])


TRANSPOSED = ("ffn1_w_gate", "ffn1_w_up", "ffn2_w_gate", "ffn2_w_up", "w_in")
BIG = [("ffn1_w_gate", (F_SH, D_MODEL)), ("ffn1_w_up", (F_SH, D_MODEL)), ("ffn1_w_down", (F_SH, D_MODEL)),
       ("ffn2_w_gate", (F_SH, D_MODEL)), ("ffn2_w_up", (F_SH, D_MODEL)), ("ffn2_w_down", (F_SH, D_MODEL)),
       ("w_in", (D_IN_SH, D_MODEL)), ("lru_wa", (2, 4, 64, LRU_HD)), ("lru_wx", (2, 4, 64, LRU_HD)),
       ("lru_w_out", (256, D_MODEL)), ("sc_w_out", (SC_W, 256)), ("sgu_w_out", (SGU_W, 256)), ("w_o", (256, D_MODEL))]
FFN_NAMES = [n for n, _ in BIG[:6]]
TAIL_BIG = BIG[7:]
FFN_BLK0 = 3
TAIL_ROW0 = (FFN_BLK0 + 6) * F_SH
TAIL_ROWS = 1152
SMALL = [("lru_conv_w", (4, 256)), ("lru_ba", (2, 256)), ("lru_bx", (2, 256)), ("lru_lambda", (2, 256)),
         ("sc_conv_w", (3, 128))]
REPL = [("ffn1_pre_g", (D_MODEL,)), ("ffn1_post_g", (D_MODEL,)), ("mix_pre_g", (D_MODEL,)), ("lru_conv_b", (LRU_W,)),
        ("sgu_ln_g", (SGU_W,)), ("sgu_ln_b", (SGU_W,)), ("sgu_w_s", (4, CHUNK, CHUNK)), ("sgu_b", (4, CHUNK)),
        ("mix_post_g", (D_MODEL,)), ("ffn2_pre_g", (D_MODEL,)), ("ffn2_post_g", (D_MODEL,))]
WEIGHTS = ['ffn1_pre_g', 'ffn1_w_gate', 'ffn1_w_up', 'ffn1_w_down', 'ffn1_post_g', 'mix_pre_g', 'w_in', 'lru_conv_w',
           'lru_conv_b', 'lru_wa', 'lru_ba', 'lru_wx', 'lru_bx', 'lru_lambda', 'lru_w_out', 'sc_conv_w', 'sc_w_out',
           'sgu_ln_g', 'sgu_ln_b', 'sgu_w_s', 'sgu_b', 'sgu_w_out', 'w_o', 'mix_post_g', 'ffn2_pre_g', 'ffn2_w_gate',
           'ffn2_w_up', 'ffn2_w_down', 'ffn2_post_g']


def _seg_rows(shape):
    return -(-int(np.prod(shape)) // PACK_C)


def _pack(parts, rows_total, dtype):
    lead = parts[0].shape[0]
    segs = []
    used = 0
    for p in parts:
        flat = p.reshape(lead, -1).astype(dtype)
        nr = -(-flat.shape[1] // PACK_C)
        pad = nr * PACK_C - flat.shape[1]
        if pad:
            flat = jnp.pad(flat, ((0, 0), (0, pad)))
        segs.append(flat.reshape(lead, nr, PACK_C))
        used += nr
    if rows_total > used:
        segs.append(jnp.zeros((lead, rows_total - used, PACK_C), dtype))
    return jnp.concatenate(segs, axis=1)


def _unpack(buf, specs):
    lead = buf.shape[0]
    out = {}
    r0 = 0
    for name, shape in specs:
        nr = _seg_rows(shape)
        n = int(np.prod(shape))
        out[name] = buf[:, r0:r0 + nr].reshape(lead, nr * PACK_C)[:, :n].reshape((lead,) + tuple(shape))
        r0 += nr
    return out


def _full_from_shards(name, t):
    if name in ("ffn1_w_gate", "ffn1_w_up", "ffn1_w_down", "ffn2_w_gate", "ffn2_w_up", "ffn2_w_down"):
        return t
    if name == "w_in":
        return t.reshape(D_IN, D_MODEL)
    if name in ("sc_w_out", "sgu_w_out", "lru_conv_w", "lru_ba", "lru_bx", "lru_lambda", "sc_conv_w"):
        return jnp.moveaxis(t, 0, -2).reshape(t.shape[1:-1] + (N_SHARD * t.shape[-1],))
    if name in ("lru_wa", "lru_wx"):
        return jnp.moveaxis(t, 0, 2).reshape(2, 4, LRU_HD, LRU_HD)
    if name in ("lru_w_out", "w_o"):
        return t.reshape(N_SHARD * t.shape[1], t.shape[2])
    raise ValueError(name)


def _shards_from_full(name, gfull):
    if name in ("ffn1_w_gate", "ffn1_w_up", "ffn1_w_down", "ffn2_w_gate", "ffn2_w_up", "ffn2_w_down"):
        return gfull
    if name == "w_in":
        return gfull.reshape(N_SHARD, D_IN_SH, D_MODEL)
    if name in ("sc_w_out", "sgu_w_out", "lru_conv_w", "lru_ba", "lru_bx", "lru_lambda", "sc_conv_w"):
        lastdim = gfull.shape[-1] // N_SHARD
        return jnp.moveaxis(gfull.reshape(gfull.shape[:-1] + (N_SHARD, lastdim)), -2, 0)
    if name in ("lru_wa", "lru_wx"):
        return jnp.moveaxis(gfull.reshape(2, 4, N_SHARD, 64, LRU_HD), 2, 0)
    if name in ("lru_w_out", "w_o"):
        return gfull.reshape(N_SHARD, gfull.shape[0] // N_SHARD, gfull.shape[1])
    raise ValueError(name)


def _tiles(s):
    return dict(ffn=min(512, s), ffn_bwd=min(512, s), tn=min(2048, s), mix_in=min(512, s), lru=min(512, s), mix=min(256, s), loss=min(512, s))


def _mixer_fwd(x, w, t):
    hb, za, zm = _mix_in_fwd(x, w["mix_pre_g"], w["w_in"], t["mix_in"])
    hf = _lru_fwd(za, w["lru_conv_w"], w["lru_conv_b"], w["lru_wa"][0], w["lru_ba"][0:1], w["lru_wx"][0],
                  w["lru_bx"][0:1], w["lru_lambda"][0:1], False, t["lru"])
    hr = _lru_fwd(za, w["lru_conv_w"], w["lru_conv_b"], w["lru_wa"][1], w["lru_ba"][1:2], w["lru_wx"][1],
                  w["lru_bx"][1:2], w["lru_lambda"][1:2], True, t["lru"])
    out, yain, q, ycin, ya, yb, yc, mb, mx = _mix_out_fwd(
        x, za, zm, hf, hr, w["lru_w_out"], w["sc_conv_w"], w["sc_w_out"], w["sgu_ln_g"], w["sgu_ln_b"], w["sgu_w_s"],
        w["sgu_b_t"], w["sgu_w_out"], w["w_o"], w["mix_post_g"], t["mix"])
    return out, dict(x=x, hb=hb, za=za, zm=zm, hf=hf, hr=hr, yain=yain, q=q, ycin=ycin, ya=ya, yb=yb, yc=yc, mb=mb, mx=mx)


def _mixer_bwd(dy, sv, w, t, buf):
    g = {}
    (dmx, dya, dyb, dyc, dgate, dscb, dsu, dsv, dzm, dhs, dcp, g["mix_post_g"], g["sgu_ln_g"], g["sgu_ln_b"],
     g["sgu_w_s"], dbs) = _mix_out_bwd(
        dy, sv["za"], sv["zm"], sv["hf"], sv["hr"], sv["ya"], sv["yb"], sv["yc"], sv["mx"], w["lru_w_out"], w["sc_conv_w"],
        w["sc_w_out"], w["sgu_ln_g"], w["sgu_ln_b"], w["sgu_w_s"], w["sgu_w_s_t"], w["sgu_b_t"], w["sgu_w_out"], w["w_o"],
        w["mix_post_g"], t["mix"])
    g["sgu_b"] = jnp.sum(dbs, axis=-1)
    ts = t["tn"]
    g["w_o"] = _tn_plain(sv["mb"], dmx, "dw_o", ts)
    g["lru_w_out"] = _tn_plain(sv["yain"], dya, "dw_lru_out", ts)
    g["sc_w_out"] = _tn_plain(sv["q"], dyb, "dw_sc_out", ts)
    g["sgu_w_out"] = _tn_plain(sv["ycin"], dyc, "dw_sgu_out", ts)
    dxc, dwa, dba, dwx, dbx, dlam = [], [], [], [], [], []
    for d, rev in enumerate((False, True)):
        o = _lru_bwd(sv["za"], sv["hr"] if rev else sv["hf"], dhs, w["lru_conv_w"], w["lru_conv_b"], w["lru_wa"][d],
                     w["lru_ba"][d:d + 1], w["lru_wx"][d], w["lru_bx"][d:d + 1], w["lru_lambda"][d:d + 1], rev, t["lru"])
        for lst, val in zip((dxc, dwa, dba, dwx, dbx, dlam), o):
            lst.append(val)
    g["lru_wa"] = jnp.stack(dwa)
    g["lru_wx"] = jnp.stack(dwx)
    g["lru_ba"] = jnp.concatenate(dba, axis=0)
    g["lru_bx"] = jnp.concatenate(dbx, axis=0)
    g["lru_lambda"] = jnp.concatenate(dlam, axis=0)
    dx, dz, g["mix_pre_g"], g["lru_conv_w"], g["lru_conv_b"], g["sc_conv_w"] = _mix_in_bwd(
        dy, sv["x"], sv["za"], dxc[0], dxc[1], dcp, dgate, dscb, dsu, dsv, dzm, w["mix_pre_g"], w["lru_conv_w"],
        w["sc_conv_w"], w["w_in"], t["mix"])
    buf = _tn(dz, sv["hb"], "dw_in", N_SHARD, (ts, D_IN_SH), lambda b, s_: (s_, b), (ts, D_MODEL), lambda b, s_: (s_, 0),
              None, (None, D_IN_SH, D_MODEL), None, ts, into=(buf, 0))
    return dx, g, buf


def kernel(x, ffn1_pre_g, ffn1_w_gate, ffn1_w_up, ffn1_w_down, ffn1_post_g, mix_pre_g, w_in, lru_conv_w, lru_conv_b, lru_wa, lru_ba, lru_wx, lru_bx, lru_lambda, lru_w_out, sc_conv_w, sc_w_out, sgu_ln_g, sgu_ln_b, sgu_w_s, sgu_b, sgu_w_out, w_o, mix_post_g, ffn2_pre_g, ffn2_w_gate, ffn2_w_up, ffn2_w_down, ffn2_post_g, loss_target, m_ffn1_pre_g, m_ffn1_w_gate, m_ffn1_w_up, m_ffn1_w_down, m_ffn1_post_g, m_mix_pre_g, m_w_in, m_lru_conv_w, m_lru_conv_b, m_lru_wa, m_lru_ba, m_lru_wx, m_lru_bx, m_lru_lambda, m_lru_w_out, m_sc_conv_w, m_sc_w_out, m_sgu_ln_g, m_sgu_ln_b, m_sgu_w_s, m_sgu_b, m_sgu_w_out, m_w_o, m_mix_post_g, m_ffn2_pre_g, m_ffn2_w_gate, m_ffn2_w_up, m_ffn2_w_down, m_ffn2_post_g, v_ffn1_pre_g, v_ffn1_w_gate, v_ffn1_w_up, v_ffn1_w_down, v_ffn1_post_g, v_mix_pre_g, v_w_in, v_lru_conv_w, v_lru_conv_b, v_lru_wa, v_lru_ba, v_lru_wx, v_lru_bx, v_lru_lambda, v_lru_w_out, v_sc_conv_w, v_sc_w_out, v_sgu_ln_g, v_sgu_ln_b, v_sgu_w_s, v_sgu_b, v_sgu_w_out, v_w_o, v_mix_post_g, v_ffn2_pre_g, v_ffn2_w_gate, v_ffn2_w_up, v_ffn2_w_down, v_ffn2_post_g):
    args = (ffn1_pre_g, ffn1_w_gate, ffn1_w_up, ffn1_w_down, ffn1_post_g, mix_pre_g, w_in, lru_conv_w, lru_conv_b, lru_wa, lru_ba, lru_wx, lru_bx, lru_lambda, lru_w_out, sc_conv_w, sc_w_out, sgu_ln_g, sgu_ln_b, sgu_w_s, sgu_b, sgu_w_out, w_o, mix_post_g, ffn2_pre_g, ffn2_w_gate, ffn2_w_up, ffn2_w_down, ffn2_post_g)
    margs = (m_ffn1_pre_g, m_ffn1_w_gate, m_ffn1_w_up, m_ffn1_w_down, m_ffn1_post_g, m_mix_pre_g, m_w_in, m_lru_conv_w, m_lru_conv_b, m_lru_wa, m_lru_ba, m_lru_wx, m_lru_bx, m_lru_lambda, m_lru_w_out, m_sc_conv_w, m_sc_w_out, m_sgu_ln_g, m_sgu_ln_b, m_sgu_w_s, m_sgu_b, m_sgu_w_out, m_w_o, m_mix_post_g, m_ffn2_pre_g, m_ffn2_w_gate, m_ffn2_w_up, m_ffn2_w_down, m_ffn2_post_g)
    vargs = (v_ffn1_pre_g, v_ffn1_w_gate, v_ffn1_w_up, v_ffn1_w_down, v_ffn1_post_g, v_mix_pre_g, v_w_in, v_lru_conv_w, v_lru_conv_b, v_lru_wa, v_lru_ba, v_lru_wx, v_lru_bx, v_lru_lambda, v_lru_w_out, v_sc_conv_w, v_sc_w_out, v_sgu_ln_g, v_sgu_ln_b, v_sgu_w_s, v_sgu_b, v_sgu_w_out, v_w_o, v_mix_post_g, v_ffn2_pre_g, v_ffn2_w_gate, v_ffn2_w_up, v_ffn2_w_down, v_ffn2_post_g)
    wsh = dict(zip(WEIGHTS, args))
    msh = dict(zip(WEIGHTS, margs))
    vsh = dict(zip(WEIGHTS, vargs))
    xs = x[0]
    s = xs.shape[0]
    t = _tiles(s)

    small = _pack([wsh[n].reshape(1, -1) for n, _ in SMALL], SMALL_ROWS, F32)[0]
    small_all = _all_gather(small, "ag_small")
    small_un = _unpack(small_all, [(n, (DEPTH,) + sh) for n, sh in SMALL])
    layers = []
    for l in range(DEPTH):
        packed = _pack([(wsh[n][l].T if n in TRANSPOSED else wsh[n][l])[None] for n, _ in BIG], AG_ROWS, BF16)[0]
        gathered = _all_gather(packed, "ag_weights_l%d" % l)
        un = _unpack(gathered, BIG)
        w = {n: _full_from_shards(n, un[n]) for n, _ in BIG}
        for n, _ in SMALL:
            w[n] = _full_from_shards(n, small_un[n][:, l])
        for n, _ in REPL:
            w[n] = wsh[n][l]
        for n in ("ffn1_pre_g", "ffn1_post_g", "mix_pre_g", "lru_conv_b", "sgu_ln_g", "sgu_ln_b", "mix_post_g", "ffn2_pre_g",
                  "ffn2_post_g"):
            w[n] = w[n].reshape(1, -1)
        w["sgu_w_s_t"] = jnp.swapaxes(w["sgu_w_s"], 1, 2).astype(BF16)
        w["sgu_w_s"] = w["sgu_w_s"].astype(BF16)
        w["sgu_b_t"] = w["sgu_b"].T
        layers.append(w)

    saved = []
    cur = xs
    for l in range(DEPTH):
        w = layers[l]
        sv = {}
        x1, *rest = _ffn_fwd(cur, w["ffn1_pre_g"], w["ffn1_w_gate"], w["ffn1_w_up"], w["ffn1_w_down"], w["ffn1_post_g"], t["ffn"])
        sv["ffn1"] = (cur, *rest)
        x2, sv["mix"] = _mixer_fwd(x1, w, t)
        x3, *rest = _ffn_fwd(x2, w["ffn2_pre_g"], w["ffn2_w_gate"], w["ffn2_w_up"], w["ffn2_w_down"], w["ffn2_post_g"], t["ffn"])
        sv["ffn2"] = (x2, *rest)
        saved.append(sv)
        cur = x3

    dy, sq = _loss_head(cur, loss_target[0], t["loss"])
    loss = lax.psum(0.5 * jnp.sum(sq) / D_MODEL, ("x", "y", "c"))

    grads = [None] * DEPTH
    for l in reversed(range(DEPTH)):
        w = layers[l]
        sv = saved[l]
        g = {}
        buf = jnp.zeros((N_SHARD, RS_ROWS, PACK_C), BF16)
        for tag in ("ffn2", "mix", "ffn1"):
            if tag == "mix":
                dy, gm, buf = _mixer_bwd(dy, sv["mix"], w, t, buf)
                g.update(gm)
                continue
            xin, ab, d1, d2, ff = sv[tag]
            dy, hb, dgb, dub, dfb, g[tag + "_pre_g"], g[tag + "_post_g"] = _ffn_bwd(
                dy, xin, d1, d2, ff, w[tag + "_pre_g"], w[tag + "_post_g"], w[tag + "_w_gate"], w[tag + "_w_up"],
                w[tag + "_w_down"], t["ffn_bwd"])
            buf = _ffn_wgrads(hb, ab, dgb, dub, dfb, t["tn"], buf, FFN_BLK0 + (3 if tag == "ffn2" else 0))
        parts = [_shards_from_full(n, g[n]) for n, _ in TAIL_BIG] + [_shards_from_full(n, g[n]) for n, _ in SMALL]
        parts += [jnp.broadcast_to(g[n].reshape((1,) + sh), (N_SHARD,) + sh) for n, sh in REPL]
        buf = lax.dynamic_update_slice(buf, _pack(parts, TAIL_ROWS, BF16), (0, TAIL_ROW0, 0))
        total = _reduce_scatter(buf, "_l%d" % l)
        un = _unpack(total[None, TAIL_ROW0:TAIL_ROW0 + TAIL_ROWS], TAIL_BIG + SMALL + REPL)
        un["w_in"] = total[None, 0:D_IN_SH]
        for j, n in enumerate(FFN_NAMES):
            un[n] = total[None, (FFN_BLK0 + j) * F_SH:(FFN_BLK0 + j + 1) * F_SH]
        grads[l] = {n: (un[n][0].T if n in TRANSPOSED else un[n][0]) for n in WEIGHTS}
    grad_x = dy[None]

    gw, dw, nm, nv = [], [], [], []
    for n in WEIGHTS:
        gfull = jnp.stack([grads[l][n] for l in range(DEPTH)])
        d_, m_, v_ = _adamw(wsh[n], gfull, msh[n], vsh[n], n)
        gw.append(gfull)
        dw.append(d_)
        nm.append(m_)
        nv.append(v_)
    return (loss, grad_x, *gw, *dw, *nm, *nv)
```

```python
import functools

import jax
import jax.numpy as jnp
import numpy as np
from jax import lax
from jax.experimental import pallas as pl
from jax.experimental.pallas import tpu as pltpu

F32 = jnp.float32
BF16 = jnp.bfloat16
MESH = pl.DeviceIdType.MESH

D_MODEL = 1024
D_FF = 2816
N_SHARD = 4
F_SH = D_FF // N_SHARD
D_IN = 7680
D_IN_SH = D_IN // N_SHARD
ZA_W = 4608
ZM_W = 3072
LRU_W = 1024
LRU_HD = 256
SC_W = 512
SGU_W = 512
CHUNK = 128
DEPTH = 4
EPS = 1e-6
LRU_C = 8.0
LRU_RB = 16

ADAM_LR, ADAM_B1, ADAM_B2, ADAM_EPS, ADAM_WD, ADAM_STEP = 0.001, 0.9, 0.999, 1e-08, 0.01, 10

VMEM_LIMIT = 56 * 2 ** 20
PACK_C = 1024
AG_ROWS = 7168
RS_ROWS = 7680
RS_TILE = 192
SMALL_ROWS = 32


def _cp(sem=("arbitrary",)):
    return pltpu.CompilerParams(dimension_semantics=sem, vmem_limit_bytes=VMEM_LIMIT)


def _full(shape):
    return pl.BlockSpec(shape, lambda *_: (0,) * len(shape))


def _res(shape):
    return pl.BlockSpec(shape, lambda *_: (0,) * len(shape), pipeline_mode=pl.Buffered(1))


def _dot(a, b):
    return jnp.dot(a, b, preferred_element_type=F32)


def _dot_nt(a, b):
    return lax.dot_general(a, b, (((1,), (1,)), ((), ())), preferred_element_type=F32)


def _dot_tn(a, b):
    return lax.dot_general(a, b, (((0,), (0,)), ((), ())), preferred_element_type=F32)


def _sigmoid(x):
    return 0.5 * jnp.tanh(0.5 * x) + 0.5


_GELU_K = 0.7978845608028654
_GELU_C = 0.044715


def _gelu(x):
    return 0.5 * x * (1.0 + jnp.tanh(_GELU_K * (x + _GELU_C * x * x * x)))


def _gelu_and_grad(x):
    t = jnp.tanh(_GELU_K * (x + _GELU_C * x * x * x))
    g = 0.5 * x * (1.0 + t)
    dg = 0.5 * (1.0 + t) + 0.5 * x * (1.0 - t * t) * (_GELU_K * (1.0 + 3.0 * _GELU_C * x * x))
    return g, dg


def _rms_fwd(x, g):
    rs = lax.rsqrt(jnp.mean(x * x, axis=-1, keepdims=True) + EPS)
    return x * rs * g


def _rms_bwd(dy, x, g):
    rs = lax.rsqrt(jnp.mean(x * x, axis=-1, keepdims=True) + EPS)
    n = x * rs
    dn = dy * g
    dx = rs * (dn - n * jnp.mean(dn * n, axis=-1, keepdims=True))
    return dx, jnp.sum(dy * n, axis=0, keepdims=True)


def _acc(ref, val, first):
    @pl.when(first)
    def _():
        ref[...] = val

    @pl.when(jnp.logical_not(first))
    def _():
        ref[...] += val


def _shift(xm, d, prev, nxt, rows):
    tm = xm.shape[0]
    if d == 0:
        return xm
    y = pltpu.roll(xm, (-d) % tm, 0)
    rows8 = rows[0:8]
    if d < 0:
        hb = prev.shape[0]
        top = y[0:8]
        for r in range(-d):
            top = jnp.where(rows8 == r, prev[hb + r + d:hb + r + d + 1, :], top)
        return jnp.concatenate([top, y[8:]], axis=0)
    bot = y[tm - 8:]
    for r in range(d):
        bot = jnp.where(rows8 == 8 - d + r, nxt[r:r + 1, :], bot)
    return jnp.concatenate([y[:tm - 8], bot], axis=0)


def _halo(tm, hb, w, col, nt, rev=False):
    r = tm // hb
    last = nt * r - 1
    ti = (lambda i: nt - 1 - i) if rev else (lambda i: i)
    return [pl.BlockSpec((tm, w), lambda i: (ti(i), col)),
            pl.BlockSpec((hb, w), lambda i: (jnp.maximum(ti(i) * r - 1, 0), col)),
            pl.BlockSpec((hb, w), lambda i: (jnp.minimum((ti(i) + 1) * r, last), col))]


def _edges(prev_ref, next_ref, ti, nt):
    prev = jnp.where(ti > 0, prev_ref[...].astype(F32), 0.0)
    nxt = jnp.where(ti < nt - 1, next_ref[...].astype(F32), 0.0)
    return prev, nxt


def _ffn_fwd(x, pre_g, wg, wu, wd, post_g, tm, gather=None):
    s = x.shape[0]
    nt = s // tm

    def body(x_ref, pg_ref, wg_ref, wu_ref, wd_ref, qg_ref, *rest):
        if gather is None:
            o_ref, a_ref, d1_ref, d2_ref, f_ref = rest
        else:
            pk_ref, o_ref, a_ref, d1_ref, d2_ref, f_ref, gat_ref, ssem, rsem = rest
            _gather_steps(pl.program_id(0), nt, pk_ref, gat_ref, ssem, rsem)
        xv = x_ref[...]
        hb = _rms_fwd(xv, pg_ref[...]).astype(BF16)
        f = jnp.zeros((tm, D_MODEL), F32)
        for k in range(N_SHARD):
            g = _dot_nt(hb, wg_ref[k])
            u = _dot_nt(hb, wu_ref[k])
            sg = _sigmoid(g)
            silu = g * sg
            ab = (silu * u).astype(BF16)
            a_ref[k] = ab
            d1_ref[k] = (u * (sg * (1.0 + g * (1.0 - sg)))).astype(BF16)
            d2_ref[k] = silu.astype(BF16)
            f = f + _dot(ab, wd_ref[k])
        f_ref[...] = f.astype(BF16)
        o_ref[...] = xv + 0.5 * _rms_fwd(f, qg_ref[...])

    row = pl.BlockSpec((tm, D_MODEL), lambda i: (i, 0))
    gu = pl.BlockSpec((N_SHARD, tm, F_SH), lambda i: (0, i, 0))
    in_specs = [row, _full((1, D_MODEL)), _res((N_SHARD, F_SH, D_MODEL)), _res((N_SHARD, F_SH, D_MODEL)),
                _res((N_SHARD, F_SH, D_MODEL)), _full((1, D_MODEL))]
    out_specs = [row, gu, gu, gu, row]
    out_shape = [jax.ShapeDtypeStruct((s, D_MODEL), F32), jax.ShapeDtypeStruct((N_SHARD, s, F_SH), BF16),
                 jax.ShapeDtypeStruct((N_SHARD, s, F_SH), BF16), jax.ShapeDtypeStruct((N_SHARD, s, F_SH), BF16),
                 jax.ShapeDtypeStruct((s, D_MODEL), BF16)]
    args = [x, pre_g, wg, wu, wd, post_g]
    scratch = []
    if gather is not None:
        in_specs.append(HBM)
        out_specs.append(HBM)
        out_shape.append(jax.ShapeDtypeStruct((N_SHARD,) + gather.shape, gather.dtype))
        args.append(gather)
        scratch = [pltpu.SemaphoreType.DMA((6,)), pltpu.SemaphoreType.DMA((6,))]
    outs = list(pl.pallas_call(
        body, name="ffn_fwd" if gather is None else "ffn_fwd_gather", grid=(nt,),
        in_specs=in_specs, out_specs=out_specs, out_shape=out_shape, scratch_shapes=scratch,
        compiler_params=_cp(("parallel",) if gather is None else ("arbitrary",)),
    )(*args))
    if gather is not None:
        outs[-1] = _own_slot(outs[-1], gather)
    return outs


def _ffn_bwd(dy, x, d1, d2, f, pre_g, post_g, wg, wu, wd, tm):
    s = x.shape[0]
    nt = s // tm

    def body_a(dy_ref, d1_ref, d2_ref, f_ref, qg_ref, wd_ref, dg_ref, du_ref, df_ref, dqg_ref):
        df, dq = _rms_bwd(0.5 * dy_ref[...], f_ref[...].astype(F32), qg_ref[...])
        dfb = df.astype(BF16)
        df_ref[...] = dfb
        for k in range(N_SHARD):
            da = _dot_nt(dfb, wd_ref[k])
            dg_ref[k] = (da * d1_ref[k].astype(F32)).astype(BF16)
            du_ref[k] = (da * d2_ref[k].astype(F32)).astype(BF16)
        _acc(dqg_ref, dq, pl.program_id(0) == 0)

    def body_b(dy_ref, x_ref, dg_ref, du_ref, pg_ref, wg_ref, wu_ref, dx_ref, h_ref, dpg_ref):
        xv = x_ref[...]
        h_ref[...] = _rms_fwd(xv, pg_ref[...]).astype(BF16)
        dh = jnp.zeros((tm, D_MODEL), F32)
        for k in range(N_SHARD):
            dh = dh + _dot(dg_ref[k], wg_ref[k]) + _dot(du_ref[k], wu_ref[k])
        dxn, dp = _rms_bwd(dh, xv, pg_ref[...])
        dx_ref[...] = dy_ref[...] + dxn
        _acc(dpg_ref, dp, pl.program_id(0) == 0)

    row = pl.BlockSpec((tm, D_MODEL), lambda i: (i, 0))
    gu = pl.BlockSpec((N_SHARD, tm, F_SH), lambda i: (0, i, 0))
    vec = _full((1, D_MODEL))
    wsp = _res((N_SHARD, F_SH, D_MODEL))
    big = jax.ShapeDtypeStruct((N_SHARD, s, F_SH), BF16)
    tok_bf = jax.ShapeDtypeStruct((s, D_MODEL), BF16)
    vec_f = jax.ShapeDtypeStruct((1, D_MODEL), F32)
    dg, du, df, dqg = pl.pallas_call(
        body_a, name="ffn_bwd_a", grid=(nt,),
        in_specs=[row, gu, gu, row, vec, wsp],
        out_specs=[gu, gu, row, vec],
        out_shape=[big, big, tok_bf, vec_f],
        compiler_params=_cp(),
    )(dy, d1, d2, f, post_g, wd)
    dx, h, dpg = pl.pallas_call(
        body_b, name="ffn_bwd_b", grid=(nt,),
        in_specs=[row, row, gu, gu, vec, wsp, wsp],
        out_specs=[row, row, vec],
        out_shape=[jax.ShapeDtypeStruct((s, D_MODEL), F32), tok_bf, vec_f],
        compiler_params=_cp(),
    )(dy, x, dg, du, pre_g, wg, wu)
    return dx, h, dg, du, df, dpg, dqg


def _tn(lhs, rhs, name, nb, lhs_blk, lhs_map, rhs_blk, rhs_map, out_shape, out_blk, out_map, ts, into=None):
    s = lhs.shape[-2]
    ns = s // ts
    acc_shape = tuple(d for d in out_blk if d is not None)

    def body(l_ref, r_ref, *rest):
        o_ref, acc_s = rest[-2:]
        t = pl.program_id(1)
        _acc(acc_s, _dot_tn(l_ref[...], r_ref[...]), t == 0)

        @pl.when(t == ns - 1)
        def _():
            o_ref[...] = acc_s[...].astype(BF16)

    in_specs = [pl.BlockSpec(lhs_blk, lhs_map), pl.BlockSpec(rhs_blk, rhs_map)]
    args = (lhs, rhs)
    aliases = {}
    if into is not None:
        buf, blk_row = into
        in_specs.append(HBM)
        args = (lhs, rhs, buf)
        aliases = {2: 0}
        out_shape = buf.shape
        out_map = lambda b, t: (b, blk_row, 0)
    return pl.pallas_call(
        body, name=name, grid=(nb, ns),
        in_specs=in_specs,
        out_specs=pl.BlockSpec(out_blk, out_map),
        out_shape=jax.ShapeDtypeStruct(out_shape, BF16),
        scratch_shapes=[pltpu.VMEM(acc_shape, F32)],
        input_output_aliases=aliases,
        compiler_params=_cp(("parallel", "arbitrary")),
    )(*args)


def _tn_plain(lhs, rhs, name, ts):
    m, n = lhs.shape[1], rhs.shape[1]
    return _tn(lhs, rhs, name, 1, (ts, m), lambda b, t: (t, 0), (ts, n), lambda b, t: (t, 0),
               (m, n), (m, n), lambda b, t: (0, 0), ts)


def _ffn_wgrads(h, a, dg, du, df, ts, buf, blk0):
    lm2 = lambda b, t: (t, 0)
    bm3 = lambda b, t: (b, t, 0)
    for j, (lhs, rhs, name) in enumerate(((dg, h, "ffn_dwg"), (du, h, "ffn_dwu"), (a, df, "ffn_dwd"))):
        buf = _tn(lhs, rhs, name, N_SHARD, (None, ts, F_SH), bm3, (ts, D_MODEL), lm2, None, (None, F_SH, D_MODEL), None, ts,
                  into=(buf, blk0 + j))
    return buf


def _mix_in_fwd(x, pre_g, w_in, tm):
    s = x.shape[0]
    nt = s // tm
    cw = 512

    def body(x_ref, pg_ref, w_ref, h_ref, za_ref, zm_ref):
        hb = _rms_fwd(x_ref[...], pg_ref[...]).astype(BF16)
        h_ref[...] = hb
        for j in range(ZA_W // cw):
            za_ref[:, j * cw:(j + 1) * cw] = _dot_nt(hb, w_ref[j * cw:(j + 1) * cw, :]).astype(BF16)
        for j in range(ZM_W // cw):
            zm_ref[:, j * cw:(j + 1) * cw] = _dot_nt(hb, w_ref[ZA_W + j * cw:ZA_W + (j + 1) * cw, :]).astype(BF16)

    row = pl.BlockSpec((tm, D_MODEL), lambda i: (i, 0))
    return pl.pallas_call(
        body, name="mix_in_fwd", grid=(nt,),
        in_specs=[row, _full((1, D_MODEL)), _res((D_IN, D_MODEL))],
        out_specs=[row, pl.BlockSpec((tm, ZA_W), lambda i: (i, 0)), pl.BlockSpec((tm, ZM_W), lambda i: (i, 0))],
        out_shape=[jax.ShapeDtypeStruct((s, D_MODEL), BF16), jax.ShapeDtypeStruct((s, ZA_W), BF16),
                   jax.ShapeDtypeStruct((s, ZM_W), BF16)],
        compiler_params=_cp(("parallel",)),
    )(x, pre_g, w_in)


def _lru_conv(xm, prev, nxt, rows, cw_ref, cb_ref):
    acc = cb_ref[...] + cw_ref[2:3, :] * xm
    acc = acc + cw_ref[0:1, :] * _shift(xm, -2, prev, nxt, rows)
    acc = acc + cw_ref[1:2, :] * _shift(xm, -1, prev, nxt, rows)
    acc = acc + cw_ref[3:4, :] * _shift(xm, 1, prev, nxt, rows)
    return acc


def _lru_preact(x_ref, xp_ref, xn_ref, cw_ref, cb_ref, wa_ref, ba_ref, wx_ref, bx_ref, lam_ref, ti, nt, tm,
                xc_s, ra_s, xa_s, c8_s):
    rows = lax.broadcasted_iota(jnp.int32, (tm, 1), 0)
    prev, nxt = _edges(xp_ref, xn_ref, ti, nt)
    xc = _lru_conv(x_ref[...].astype(F32), prev, nxt, rows, cw_ref, cb_ref)
    xc_s[...] = xc
    xcb = xc.astype(BF16)
    for h in range(4):
        cs_ = slice(LRU_HD * h, LRU_HD * (h + 1))
        ra_s[:, cs_] = _dot(xcb[:, cs_], wa_ref[h]) + ba_ref[:, cs_]
        xa_s[:, cs_] = _dot(xcb[:, cs_], wx_ref[h]) + bx_ref[:, cs_]
    lam = lam_ref[...]
    e = jnp.exp(-jnp.abs(lam))
    log1p_e = jnp.where(e < 1e-2, e * (1.0 - e * (0.5 - e * (1.0 / 3.0))), jnp.log(1.0 + e))
    c8_s[...] = jnp.broadcast_to(-LRU_C * (jnp.maximum(-lam, 0.0) + log1p_e), c8_s.shape)
    return xcb


def _lru_decay(cl, r):
    la = cl * r
    a = jnp.exp(la)
    y2 = 2.0 * la
    em = jnp.where(y2 > -0.004, y2 * (-1.0 + y2 * (-0.5 - y2 * (1.0 / 6.0))), 1.0 - a * a)
    return a, jnp.sqrt(em)


def _scan_rows(a_s, b_s, o_s, carry, tm, descending):
    nb = tm // 8

    def blk(j, h):
        jb = (nb - 1 - j) if descending else j
        base = pl.multiple_of(jb * 8, 8)
        for r in range(8):
            t = base + ((7 - r) if descending else r)
            h = a_s[pl.ds(t, 1), :] * h + b_s[pl.ds(t, 1), :]
            o_s[pl.ds(t, 1), :] = h
        return h

    return lax.fori_loop(0, nb, blk, carry)


def _lru_fwd(za, conv_w, conv_b, wa, ba, wx, bx, lam, rev, tm):
    s = za.shape[0]
    nt = s // tm

    def body(x_ref, xp_ref, xn_ref, cw_ref, cb_ref, wa_ref, ba_ref, wx_ref, bx_ref, lam_ref, h_ref,
             xc_s, ra_s, xa_s, o_s, c8_s, c_s):
        i = pl.program_id(0)
        ti = (nt - 1 - i) if rev else i
        _lru_preact(x_ref, xp_ref, xn_ref, cw_ref, cb_ref, wa_ref, ba_ref, wx_ref, bx_ref, lam_ref, ti, nt, tm,
                    xc_s, ra_s, xa_s, c8_s)

        def gate_blk(j, carry):
            rws = pl.ds(pl.multiple_of(j * LRU_RB, LRU_RB), LRU_RB)
            r = _sigmoid(ra_s[rws, :])
            ig = _sigmoid(xa_s[rws, :])
            a, mult = _lru_decay(c8_s[...], r)
            ra_s[rws, :] = a
            xa_s[rws, :] = ig * xc_s[rws, :] * mult
            return carry

        lax.fori_loop(0, tm // LRU_RB, gate_blk, 0)

        @pl.when(i == 0)
        def _():
            c_s[...] = jnp.zeros_like(c_s)

        c_s[...] = _scan_rows(ra_s, xa_s, o_s, c_s[...], tm, rev)
        h_ref[...] = o_s[...].astype(BF16)

    vec = _full((1, LRU_W))
    hd = _full((4, LRU_HD, LRU_HD))
    ti = (lambda i: nt - 1 - i) if rev else (lambda i: i)
    tile = pltpu.VMEM((tm, LRU_W), F32)
    return pl.pallas_call(
        body, name="lru_fwd_rev" if rev else "lru_fwd", grid=(nt,),
        in_specs=_halo(tm, 16, LRU_W, 1, nt, rev) + [_full((4, LRU_W)), vec, hd, vec, hd, vec, vec],
        out_specs=pl.BlockSpec((tm, LRU_W), lambda i: (ti(i), 0)),
        out_shape=jax.ShapeDtypeStruct((s, LRU_W), BF16),
        scratch_shapes=[tile, tile, tile, tile, pltpu.VMEM((LRU_RB, LRU_W), F32), pltpu.VMEM((1, LRU_W), F32)],
        compiler_params=_cp(),
    )(za, za, za, conv_w, conv_b, wa, ba, wx, bx, lam)


def _sc_conv(c_ref, cp_ref, cn_ref, x_ref, xp_ref, xn_ref, w_ref, ti, nt, rows):
    cprev, cnext = _edges(cp_ref, cn_ref, ti, nt)
    xprev, xnext = _edges(xp_ref, xn_ref, ti, nt)
    cv = c_ref[...].astype(F32)
    xv = x_ref[...].astype(F32)
    p = cv * xv
    pm1 = _shift(p, -1, cprev * xprev, cnext * xnext, rows)
    pp1 = _shift(p, 1, cprev * xprev, cnext * xnext, rows)
    conv = w_ref[0:1, :] * pm1 + w_ref[1:2, :] * p + w_ref[2:3, :] * pp1
    return cv, xv, p, pm1, pp1, conv


def _sgu_norm(vz, g_ref, b_ref):
    vg, dvg = _gelu_and_grad(vz)
    mu = jnp.mean(vg, axis=-1, keepdims=True)
    cen = vg - mu
    rs = lax.rsqrt(jnp.mean(cen * cen, axis=-1, keepdims=True) + EPS)
    vn = cen * rs
    return dvg, vn, rs, vn * g_ref[...] + b_ref[...]


def _mix_out_fwd(x, za, zm, hf, hr, lru_w_out, sc_conv_w, sc_w_out, ln_g, ln_b, w_s, b_s_t, sgu_w_out, w_o, post_g, tm):
    s = x.shape[0]
    nt = s // tm
    nc = tm // CHUNK

    def body(x_ref, gate_ref, scb_ref, scc_ref, sccp_ref, sccn_ref, scx_ref, scxp_ref, scxn_ref, su_ref, sv_ref,
             zm_ref, hf_ref, hr_ref, wlo_ref, scw_ref, wso_ref, lg_ref, lb_ref, ws_ref, bs_ref, wgo_ref, wo_ref,
             qg_ref, o_ref, yain_ref, q_ref, ycin_ref, ya_ref, yb_ref, yc_ref, m_ref, mx_ref, mixed_s):
        ti = pl.program_id(0)
        rows = lax.broadcasted_iota(jnp.int32, (tm, 1), 0)
        hs = hf_ref[...].astype(F32) + hr_ref[...].astype(F32)
        yain = (hs * _gelu(gate_ref[...].astype(F32))).astype(BF16)
        yain_ref[...] = yain
        ya = _dot(yain, wlo_ref[...])
        _, _, _, _, _, conv = _sc_conv(scc_ref, sccp_ref, sccn_ref, scx_ref, scxp_ref, scxn_ref, scw_ref, ti, nt, rows)
        qb = (scb_ref[...].astype(F32) * conv).astype(BF16)
        q_ref[...] = qb
        yb = _dot(qb, wso_ref[...])
        _, _, _, v = _sgu_norm(sv_ref[...].astype(F32), lg_ref, lb_ref)
        vb = v.astype(BF16)
        for n in range(nc):
            for g in range(4):
                blk = vb[n * CHUNK:(n + 1) * CHUNK, g * CHUNK:(g + 1) * CHUNK]
                mixed_s[n * CHUNK:(n + 1) * CHUNK, g * CHUNK:(g + 1) * CHUNK] = _dot(ws_ref[g], blk) + bs_ref[:, g:g + 1]
        ycin = (_gelu(su_ref[...].astype(F32)) * mixed_s[...]).astype(BF16)
        ycin_ref[...] = ycin
        yc = _dot(ycin, wgo_ref[...])
        m = (_sigmoid(zm_ref[:, 0:D_MODEL].astype(F32)) * ya + _sigmoid(zm_ref[:, D_MODEL:2 * D_MODEL].astype(F32)) * yb
             + _sigmoid(zm_ref[:, 2 * D_MODEL:3 * D_MODEL].astype(F32)) * yc)
        mb = m.astype(BF16)
        mx = _dot(mb, wo_ref[...])
        ya_ref[...] = ya.astype(BF16)
        yb_ref[...] = yb.astype(BF16)
        yc_ref[...] = yc.astype(BF16)
        m_ref[...] = mb
        mx_ref[...] = mx.astype(BF16)
        o_ref[...] = x_ref[...] + _rms_fwd(mx, qg_ref[...])

    row = pl.BlockSpec((tm, D_MODEL), lambda i: (i, 0))
    half = pl.BlockSpec((tm, 512), lambda i: (i, 0))
    col = lambda c: pl.BlockSpec((tm, 512), lambda i: (i, c))
    in_specs = ([row, pl.BlockSpec((tm, LRU_W), lambda i: (i, 0)), col(4)] + _halo(tm, 16, SC_W, 5, nt) + _halo(tm, 16, SC_W, 6, nt)
                + [col(7), col(8), pl.BlockSpec((tm, ZM_W), lambda i: (i, 0)), row, row,
                   _full((LRU_W, D_MODEL)), _full((3, SC_W)), _full((SC_W, D_MODEL)), _full((1, SGU_W)), _full((1, SGU_W)),
                   _full((4, CHUNK, CHUNK)), _full((CHUNK, 4)), _full((SGU_W, D_MODEL)), _full((D_MODEL, D_MODEL)),
                   _full((1, D_MODEL))])
    bf = lambda w: jax.ShapeDtypeStruct((s, w), BF16)
    return pl.pallas_call(
        body, name="mix_out_fwd", grid=(nt,),
        in_specs=in_specs,
        out_specs=[row, row, half, half, row, row, row, row, row],
        out_shape=[jax.ShapeDtypeStruct((s, D_MODEL), F32), bf(LRU_W), bf(SC_W), bf(SGU_W), bf(D_MODEL), bf(D_MODEL),
                   bf(D_MODEL), bf(D_MODEL), bf(D_MODEL)],
        scratch_shapes=[pltpu.VMEM((tm, SGU_W), F32)],
        compiler_params=_cp(("parallel",)),
    )(x, za, za, za, za, za, za, za, za, za, za, zm, hf, hr, lru_w_out, sc_conv_w, sc_w_out, ln_g, ln_b, w_s, b_s_t,
      sgu_w_out, w_o, post_g)


def _mix_out_bwd(dy, za, zm, hf, hr, ya, yb, yc, mx, lru_w_out, sc_conv_w, sc_w_out, ln_g, ln_b, w_s, w_s_t, b_s_t,
                 sgu_w_out, w_o, post_g, tm, exchange=None):
    s = dy.shape[0]
    nt = s // tm
    nc = tm // CHUNK

    n_in, n_out = 29, 16

    def body(*refs):
        if exchange is not None:
            _exchange_steps(pl.program_id(0), nt, refs[n_in], refs[n_in + 1 + n_out], refs[-2], refs[-1])
            refs = refs[:n_in] + refs[n_in + 1:n_in + 1 + n_out] + refs[n_in + 2 + n_out:-2]
        compute(*refs)

    def compute(dy_ref, gate_ref, scb_ref, scc_ref, sccp_ref, sccn_ref, scx_ref, scxp_ref, scxn_ref, su_ref, sv_ref,
                zm_ref, hf_ref, hr_ref, ya_ref, yb_ref, yc_ref, mx_ref, wlo_ref, scw_ref, wso_ref, lg_ref, lb_ref,
                ws_ref, wst_ref, bs_ref, wgo_ref, wo_ref, qg_ref,
                dmx_ref, dya_ref, dyb_ref, dyc_ref, dgate_ref, dscb_ref, dsu_ref, dsv_ref, dzm_ref, dhs_ref, dcp_ref,
                dqg_ref, dlg_ref, dlb_ref, dws_ref, dbs_ref, mixed_s, dv_s):
        ti = pl.program_id(0)
        first = ti == 0
        rows = lax.broadcasted_iota(jnp.int32, (tm, 1), 0)
        dmx, dq = _rms_bwd(dy_ref[...], mx_ref[...].astype(F32), qg_ref[...])
        _acc(dqg_ref, dq, first)
        dmxb = dmx.astype(BF16)
        dmx_ref[...] = dmxb
        dm = _dot_nt(dmxb, wo_ref[...])
        dys = []
        for k, (y_ref, d_ref) in enumerate(((ya_ref, dya_ref), (yb_ref, dyb_ref), (yc_ref, dyc_ref))):
            gk = _sigmoid(zm_ref[:, k * D_MODEL:(k + 1) * D_MODEL].astype(F32))
            dyk = (dm * gk).astype(BF16)
            d_ref[...] = dyk
            dys.append(dyk)
            dzm_ref[:, k * D_MODEL:(k + 1) * D_MODEL] = (dm * y_ref[...].astype(F32) * gk * (1.0 - gk)).astype(BF16)
        dyain = _dot_nt(dys[0], wlo_ref[...])
        gg, dgg = _gelu_and_grad(gate_ref[...].astype(F32))
        hs = hf_ref[...].astype(F32) + hr_ref[...].astype(F32)
        dhs_ref[...] = dyain * gg
        dgate_ref[...] = (dyain * hs * dgg).astype(BF16)
        dq_b = _dot_nt(dys[1], wso_ref[...])
        _, _, _, _, _, conv = _sc_conv(scc_ref, sccp_ref, sccn_ref, scx_ref, scxp_ref, scxn_ref, scw_ref, ti, nt, rows)
        dscb_ref[...] = (dq_b * conv).astype(BF16)
        dcp_ref[...] = dq_b * scb_ref[...].astype(F32)
        dycin = _dot_nt(dys[2], wgo_ref[...])
        dvg, vn, rs, v = _sgu_norm(sv_ref[...].astype(F32), lg_ref, lb_ref)
        vb = v.astype(BF16)
        ug, dug = _gelu_and_grad(su_ref[...].astype(F32))
        dmixed = dycin * ug
        dmb = dmixed.astype(BF16)
        dws = [jnp.zeros((CHUNK, CHUNK), F32) for _ in range(4)]
        dbs = [jnp.zeros((CHUNK, CHUNK), F32) for _ in range(4)]
        for n in range(nc):
            for g in range(4):
                rs_, cs_ = slice(n * CHUNK, (n + 1) * CHUNK), slice(g * CHUNK, (g + 1) * CHUNK)
                mixed_s[rs_, cs_] = _dot(ws_ref[g], vb[rs_, cs_]) + bs_ref[:, g:g + 1]
                dv_s[rs_, cs_] = _dot(wst_ref[g], dmb[rs_, cs_])
                dws[g] = dws[g] + _dot_nt(dmb[rs_, cs_], vb[rs_, cs_])
                dbs[g] = dbs[g] + dmixed[rs_, cs_]
        for g in range(4):
            _acc(dws_ref.at[g], dws[g], first)
            _acc(dbs_ref.at[g], dbs[g], first)
        dsu_ref[...] = (dycin * mixed_s[...] * dug).astype(BF16)
        dv = dv_s[...]
        _acc(dlg_ref, jnp.sum(dv * vn, axis=0, keepdims=True), first)
        _acc(dlb_ref, jnp.sum(dv, axis=0, keepdims=True), first)
        dvn = dv * lg_ref[...]
        dcen = rs * (dvn - jnp.mean(dvn, axis=-1, keepdims=True) - vn * jnp.mean(dvn * vn, axis=-1, keepdims=True))
        dsv_ref[...] = (dcen * dvg).astype(BF16)

    row = pl.BlockSpec((tm, D_MODEL), lambda i: (i, 0))
    half = pl.BlockSpec((tm, 512), lambda i: (i, 0))
    col = lambda c: pl.BlockSpec((tm, 512), lambda i: (i, c))
    zmrow = pl.BlockSpec((tm, ZM_W), lambda i: (i, 0))
    sq = _full((4, CHUNK, CHUNK))
    in_specs = ([row, pl.BlockSpec((tm, LRU_W), lambda i: (i, 0)), col(4)] + _halo(tm, 16, SC_W, 5, nt) + _halo(tm, 16, SC_W, 6, nt)
                + [col(7), col(8), zmrow, row, row, row, row, row, row,
                   _full((LRU_W, D_MODEL)), _full((3, SC_W)), _full((SC_W, D_MODEL)), _full((1, SGU_W)), _full((1, SGU_W)),
                   sq, sq, _full((CHUNK, 4)), _full((SGU_W, D_MODEL)), _full((D_MODEL, D_MODEL)), _full((1, D_MODEL))])
    bf = lambda w: jax.ShapeDtypeStruct((s, w), BF16)
    out_specs = [row, row, row, row, row, half, half, half, zmrow, row, half,
                 _full((1, D_MODEL)), _full((1, SGU_W)), _full((1, SGU_W)), sq, sq]
    out_shape = [bf(D_MODEL), bf(D_MODEL), bf(D_MODEL), bf(D_MODEL), bf(LRU_W), bf(SC_W), bf(SGU_W), bf(SGU_W), bf(ZM_W),
                 jax.ShapeDtypeStruct((s, LRU_W), F32), jax.ShapeDtypeStruct((s, SC_W), F32),
                 jax.ShapeDtypeStruct((1, D_MODEL), F32), jax.ShapeDtypeStruct((1, SGU_W), F32),
                 jax.ShapeDtypeStruct((1, SGU_W), F32), jax.ShapeDtypeStruct((4, CHUNK, CHUNK), F32),
                 jax.ShapeDtypeStruct((4, CHUNK, CHUNK), F32)]
    scratch = [pltpu.VMEM((tm, SGU_W), F32), pltpu.VMEM((tm, SGU_W), F32)]
    args = [dy, za, za, za, za, za, za, za, za, za, za, zm, hf, hr, ya, yb, yc, mx, lru_w_out, sc_conv_w, sc_w_out, ln_g, ln_b,
            w_s, w_s_t, b_s_t, sgu_w_out, w_o, post_g]
    assert len(in_specs) == n_in and len(out_specs) == n_out
    if exchange is not None:
        in_specs.append(HBM)
        out_specs.append(HBM)
        out_shape.append(jax.ShapeDtypeStruct(exchange.shape, exchange.dtype))
        scratch += [pltpu.SemaphoreType.DMA((3,)), pltpu.SemaphoreType.DMA((3,))]
        args.append(exchange)
    outs = list(pl.pallas_call(
        body, name="mix_out_bwd" if exchange is None else "mix_out_bwd_exchange", grid=(nt,),
        in_specs=in_specs, out_specs=out_specs, out_shape=out_shape, scratch_shapes=scratch,
        compiler_params=_cp(),
    )(*args))
    if exchange is not None:
        outs[-1] = _own_block(outs[-1], exchange)
    return outs


def _lru_bwd(za, h_dir, dhs, conv_w, conv_b, wa, ba, wx, bx, lam, rev, tm):
    s = za.shape[0]
    nt = s // tm
    back = not rev

    def body(x_ref, xp_ref, xn_ref, h_ref, hp_ref, hn_ref, dh_ref, cw_ref, cb_ref, wa_ref, ba_ref, wx_ref, bx_ref,
             lam_ref, dxc_ref, dwa_ref, dba_ref, dwx_ref, dbx_ref, dlam_ref,
             xc_s, ra_s, xa_s, a_s, m_s, l_s, hsh_s, c8_s, c_s):
        i = pl.program_id(0)
        first = i == 0
        ti = (nt - 1 - i) if back else i
        rows = lax.broadcasted_iota(jnp.int32, (tm, 1), 0)
        xcb = _lru_preact(x_ref, xp_ref, xn_ref, cw_ref, cb_ref, wa_ref, ba_ref, wx_ref, bx_ref, lam_ref, ti, nt, tm,
                          xc_s, ra_s, xa_s, c8_s)
        hprev, hnext = _edges(hp_ref, hn_ref, ti, nt)
        hsh_s[...] = _shift(h_ref[...].astype(F32), 1 if rev else -1, hprev, hnext, rows)

        def gate_blk(j, carry):
            rws = pl.ds(pl.multiple_of(j * LRU_RB, LRU_RB), LRU_RB)
            r = _sigmoid(ra_s[rws, :])
            a, mult = _lru_decay(c8_s[...], r)
            ra_s[rws, :] = r
            xa_s[rws, :] = _sigmoid(xa_s[rws, :])
            a_s[rws, :] = a
            m_s[rws, :] = mult
            return carry

        lax.fori_loop(0, tm // LRU_RB, gate_blk, 0)

        @pl.when(first)
        def _():
            c_s[...] = jnp.zeros_like(c_s)

        nb = tm // 8

        def blk(j, c):
            jb = (nb - 1 - j) if back else j
            base = pl.multiple_of(jb * 8, 8)
            for q in range(8):
                t = base + ((7 - q) if back else q)
                lt = dh_ref[pl.ds(t, 1), :] + c
                l_s[pl.ds(t, 1), :] = lt
                c = a_s[pl.ds(t, 1), :] * lt
            return c

        c_s[...] = lax.fori_loop(0, nb, blk, c_s[...])

        def grad_blk(j, sums):
            s_lam, s_ba, s_bx = sums
            rws = pl.ds(pl.multiple_of(j * LRU_RB, LRU_RB), LRU_RB)
            du = l_s[rws, :]
            a = a_s[rws, :]
            r = ra_s[rws, :]
            ig = xa_s[rws, :]
            mult = m_s[rws, :]
            xc = xc_s[rws, :]
            t1 = du * mult
            dla = du * hsh_s[rws, :] * a - (du * ig * xc) * (a * a) / mult
            dlr = dla * r
            drp = dlr * c8_s[...] * (1.0 - r)
            dip = (t1 * xc) * ig * (1.0 - ig)
            l_s[rws, :] = t1 * ig
            ra_s[rws, :] = drp
            xa_s[rws, :] = dip
            fold = lambda v: sum(v[8 * q:8 * q + 8] for q in range(1, LRU_RB // 8)) + v[0:8]
            return s_lam + fold(dlr), s_ba + fold(drp), s_bx + fold(dip)

        zero8 = jnp.zeros((8, LRU_W), F32)
        s_lam, s_ba, s_bx = lax.fori_loop(0, tm // LRU_RB, grad_blk, (zero8, zero8, zero8))
        _acc(dlam_ref, jnp.sum(s_lam, axis=0, keepdims=True), first)
        _acc(dba_ref, jnp.sum(s_ba, axis=0, keepdims=True), first)
        _acc(dbx_ref, jnp.sum(s_bx, axis=0, keepdims=True), first)
        drb = ra_s[...].astype(BF16)
        dib = xa_s[...].astype(BF16)
        for h in range(4):
            cs_ = slice(LRU_HD * h, LRU_HD * (h + 1))
            dxc_ref[:, cs_] = l_s[:, cs_] + _dot_nt(drb[:, cs_], wa_ref[h]) + _dot_nt(dib[:, cs_], wx_ref[h])
            _acc(dwa_ref.at[h], _dot_tn(xcb[:, cs_], drb[:, cs_]), first)
            _acc(dwx_ref.at[h], _dot_tn(xcb[:, cs_], dib[:, cs_]), first)

        @pl.when(i == nt - 1)
        def _():
            dlam_ref[...] = dlam_ref[...] * (LRU_C * _sigmoid(-lam_ref[...]))

    vec = _full((1, LRU_W))
    hd = _full((4, LRU_HD, LRU_HD))
    tix = (lambda i: nt - 1 - i) if back else (lambda i: i)
    rowspec = pl.BlockSpec((tm, LRU_W), lambda i: (tix(i), 0))
    return pl.pallas_call(
        body, name="lru_bwd_rev" if rev else "lru_bwd", grid=(nt,),
        in_specs=_halo(tm, 16, LRU_W, 1, nt, back) + _halo(tm, 16, LRU_W, 0, nt, back) + [rowspec, _full((4, LRU_W)), vec, hd, vec, hd, vec, vec],
        out_specs=[rowspec, hd, vec, hd, vec, vec],
        out_shape=[jax.ShapeDtypeStruct((s, LRU_W), F32), jax.ShapeDtypeStruct((4, LRU_HD, LRU_HD), F32),
                   jax.ShapeDtypeStruct((1, LRU_W), F32), jax.ShapeDtypeStruct((4, LRU_HD, LRU_HD), F32),
                   jax.ShapeDtypeStruct((1, LRU_W), F32), jax.ShapeDtypeStruct((1, LRU_W), F32)],
        scratch_shapes=[pltpu.VMEM((tm, LRU_W), F32)] * 7 + [pltpu.VMEM((LRU_RB, LRU_W), F32), pltpu.VMEM((1, LRU_W), F32)],
        compiler_params=_cp(),
    )(za, za, za, h_dir, h_dir, h_dir, dhs, conv_w, conv_b, wa, ba, wx, bx, lam)


def _mix_in_bwd(dy, x, za, dxc_f, dxc_r, dcp, dgate, dscb, dsu, dsv, dzm, pre_g, lru_conv_w, sc_conv_w, w_in, tm):
    s = x.shape[0]
    nt = s // tm
    cw = 512

    def body(dy_ref, x_ref, lx_ref, lxp_ref, lxn_ref, scc_ref, sccp_ref, sccn_ref, scx_ref, scxp_ref, scxn_ref,
             df_ref, dfp_ref, dfn_ref, dr_ref, drp_ref, drn_ref, dcp_ref, dcpp_ref, dcpn_ref,
             dgate_ref, dscb_ref, dsu_ref, dsv_ref, dzm_ref, pg_ref, lcw_ref, scw_ref, w_ref,
             dx_ref, dz_ref, dpg_ref, dlcw_ref, dlcb_ref, dscw_ref):
        ti = pl.program_id(0)
        first = ti == 0
        rows = lax.broadcasted_iota(jnp.int32, (tm, 1), 0)
        fp, fn = _edges(dfp_ref, dfn_ref, ti, nt)
        rp, rn = _edges(drp_ref, drn_ref, ti, nt)
        dxc = df_ref[...] + dr_ref[...]
        dprev, dnext = fp + rp, fn + rn
        dlx = lcw_ref[2:3, :] * dxc
        dlx = dlx + lcw_ref[0:1, :] * _shift(dxc, 2, dprev, dnext, rows)
        dlx = dlx + lcw_ref[1:2, :] * _shift(dxc, 1, dprev, dnext, rows)
        dlx = dlx + lcw_ref[3:4, :] * _shift(dxc, -1, dprev, dnext, rows)
        lprev, lnext = _edges(lxp_ref, lxn_ref, ti, nt)
        lx = lx_ref[...].astype(F32)
        _acc(dlcb_ref, jnp.sum(dxc, axis=0, keepdims=True), first)
        for k, d in enumerate((-2, -1, 0, 1)):
            _acc(dlcw_ref.at[pl.ds(k, 1), :], jnp.sum(dxc * _shift(lx, d, lprev, lnext, rows), axis=0, keepdims=True), first)
        cv, xv, p, pm1, pp1, _ = _sc_conv(scc_ref, sccp_ref, sccn_ref, scx_ref, scxp_ref, scxn_ref, scw_ref, ti, nt, rows)
        cprev, cnext = _edges(dcpp_ref, dcpn_ref, ti, nt)
        dcp_v = dcp_ref[...]
        dp = (scw_ref[1:2, :] * dcp_v + scw_ref[0:1, :] * _shift(dcp_v, 1, cprev, cnext, rows)
              + scw_ref[2:3, :] * _shift(dcp_v, -1, cprev, cnext, rows))
        for k, pk in enumerate((pm1, p, pp1)):
            _acc(dscw_ref.at[pl.ds(k, 1), :], jnp.sum(dcp_v * pk, axis=0, keepdims=True), first)
        dz_ref[:, 0:1024] = dgate_ref[...]
        dz_ref[:, 1024:2048] = dlx.astype(BF16)
        dz_ref[:, 2048:2560] = dscb_ref[...]
        dz_ref[:, 2560:3072] = (dp * xv).astype(BF16)
        dz_ref[:, 3072:3584] = (dp * cv).astype(BF16)
        dz_ref[:, 3584:4096] = dsu_ref[...]
        dz_ref[:, 4096:4608] = dsv_ref[...]
        dz_ref[:, 4608:7680] = dzm_ref[...]
        dh = _dot(dz_ref[...], w_ref[...])
        dxn, dpg = _rms_bwd(dh, x_ref[...], pg_ref[...])
        dx_ref[...] = dy_ref[...] + dxn
        _acc(dpg_ref, dpg, first)

    row = pl.BlockSpec((tm, D_MODEL), lambda i: (i, 0))
    half = pl.BlockSpec((tm, 512), lambda i: (i, 0))
    in_specs = ([row, row] + _halo(tm, 16, LRU_W, 1, nt) + _halo(tm, 16, SC_W, 5, nt) + _halo(tm, 16, SC_W, 6, nt)
                + _halo(tm, 8, LRU_W, 0, nt) + _halo(tm, 8, LRU_W, 0, nt) + _halo(tm, 8, SC_W, 0, nt)
                + [row, half, half, half, pl.BlockSpec((tm, ZM_W), lambda i: (i, 0)),
                   _full((1, D_MODEL)), _full((4, LRU_W)), _full((3, SC_W)), _res((D_IN, D_MODEL))])
    return pl.pallas_call(
        body, name="mix_in_bwd", grid=(nt,),
        in_specs=in_specs,
        out_specs=[row, pl.BlockSpec((tm, D_IN), lambda i: (i, 0)), _full((1, D_MODEL)), _full((4, LRU_W)),
                   _full((1, LRU_W)), _full((3, SC_W))],
        out_shape=[jax.ShapeDtypeStruct((s, D_MODEL), F32), jax.ShapeDtypeStruct((s, D_IN), BF16),
                   jax.ShapeDtypeStruct((1, D_MODEL), F32), jax.ShapeDtypeStruct((4, LRU_W), F32),
                   jax.ShapeDtypeStruct((1, LRU_W), F32), jax.ShapeDtypeStruct((3, SC_W), F32)],
        compiler_params=_cp(),
    )(dy, x, za, za, za, za, za, za, za, za, za, dxc_f, dxc_f, dxc_f, dxc_r, dxc_r, dxc_r, dcp, dcp, dcp,
      dgate, dscb, dsu, dsv, dzm, pre_g, lru_conv_w, sc_conv_w, w_in)


def _loss_head(y, target, tm):
    s = y.shape[0]
    nt = s // tm

    def body(y_ref, t_ref, dy_ref, acc_ref):
        err = y_ref[...] - t_ref[...]
        dy_ref[...] = err * (1.0 / D_MODEL)
        _acc(acc_ref, jnp.sum(err * err, axis=0, keepdims=True), pl.program_id(0) == 0)

    row = pl.BlockSpec((tm, D_MODEL), lambda i: (i, 0))
    return pl.pallas_call(
        body, name="loss_head", grid=(nt,), in_specs=[row, row], out_specs=[row, _full((1, D_MODEL))],
        out_shape=[jax.ShapeDtypeStruct((s, D_MODEL), F32), jax.ShapeDtypeStruct((1, D_MODEL), F32)],
        compiler_params=_cp(),
    )(y, target)


def _row_tile(rows, cols):
    cap = max(8, (2 ** 18) // cols)
    best = None
    for t in range(8, min(rows, cap) + 1, 8):
        if rows % t == 0:
            best = t
    return best if best is not None else rows


def _adamw(w, g, m, v, name):
    shape = w.shape
    cols = shape[-1]
    rows = int(np.prod(shape[:-1]))
    tr = _row_tile(rows, cols)
    bc1 = 1.0 - ADAM_B1 ** ADAM_STEP
    bc2 = 1.0 - ADAM_B2 ** ADAM_STEP

    def body(w_ref, g_ref, m_ref, v_ref, d_ref, nm_ref, nv_ref):
        gv = g_ref[...]
        mn = ADAM_B1 * m_ref[...] + (1.0 - ADAM_B1) * gv
        vn = ADAM_B2 * v_ref[...] + (1.0 - ADAM_B2) * (gv * gv)
        nm_ref[...] = mn
        nv_ref[...] = vn
        d_ref[...] = -ADAM_LR * ((mn / bc1) / (jnp.sqrt(vn / bc2) + ADAM_EPS) + ADAM_WD * w_ref[...])

    spec = pl.BlockSpec((tr, cols), lambda i: (i, 0))
    sds = jax.ShapeDtypeStruct((rows, cols), F32)
    outs = pl.pallas_call(
        body, name="adamw_" + name, grid=(rows // tr,), in_specs=[spec] * 4, out_specs=[spec] * 3,
        out_shape=[sds, sds, sds], compiler_params=_cp(("parallel",)),
    )(w.reshape(rows, cols), g.reshape(rows, cols), m.reshape(rows, cols), v.reshape(rows, cols))
    return [o.reshape(shape) for o in outs]


HBM = pl.BlockSpec(memory_space=pl.ANY)


def _place():
    x, y, c = lax.axis_index("x"), lax.axis_index("y"), lax.axis_index("c")
    chips = [(1 - x, y), (x, 1 - y), (1 - x, 1 - y)]
    return x, y, c, chips


def _all_gather(buf, name):
    r, cdim = buf.shape

    def body(b_ref, o_ref, ssem, rsem):
        _gather_steps(0, 1, b_ref, o_ref, ssem, rsem)

    got = pl.pallas_call(
        body, name=name, in_specs=[HBM], out_specs=HBM,
        out_shape=jax.ShapeDtypeStruct((N_SHARD, r, cdim), buf.dtype),
        scratch_shapes=[pltpu.SemaphoreType.DMA((6,)), pltpu.SemaphoreType.DMA((6,))],
    )(buf)
    return _own_slot(got, buf)


def _own_slot(got, buf):
    me = 2 * lax.axis_index("x") + lax.axis_index("y")
    return lax.dynamic_update_slice(got, buf[None], (me, 0, 0))


def _at_step(i, step):
    if isinstance(i, int):
        return (lambda f: f()) if i == step else (lambda f: None)
    return pl.when(i == step)


def _gather_steps(i, nt, b_ref, o_ref, ssem, rsem):
    r2 = b_ref.shape[0] // 2
    x, y, c, chips = _place()
    me = 2 * x + y
    sib = (x, y, 1 - c)
    mine = pl.ds(pl.multiple_of(c * r2, 16), r2)
    other = pl.ds(pl.multiple_of((1 - c) * r2, 16), r2)

    def rc(k, src, dst, to):
        return pltpu.make_async_remote_copy(src_ref=src, dst_ref=dst, send_sem=ssem.at[k], recv_sem=rsem.at[k],
                                            device_id=to, device_id_type=MESH)

    def sends():
        return [rc(j, b_ref.at[mine], o_ref.at[me, mine], (cx, cy, c)) for j, (cx, cy) in enumerate(chips)]

    def landed(j, rows):
        cx, cy = chips[j]
        return o_ref.at[2 * cx + cy, rows]

    @_at_step(i, 0)
    def _():
        for cp in sends():
            cp.start()

    @_at_step(i, max(nt - 4, 0))
    def _():
        for j, (cx, cy) in enumerate(chips):
            rc(j, landed(j, mine), landed(j, mine), (cx, cy, c)).wait_recv()
            rc(3 + j, landed(j, mine), landed(j, mine), sib).start()

    @_at_step(i, nt - 1)
    def _():
        for j in range(3):
            rc(3 + j, landed(j, other), landed(j, other), sib).wait_recv()
        for cp in sends():
            cp.wait_send()
        for j in range(3):
            rc(3 + j, landed(j, mine), landed(j, mine), sib).wait_send()


def _exchange_steps(i, nt, b_ref, o_ref, ssem, rsem):
    x, y, c, chips = _place()
    me = 2 * x + y

    def sends():
        return [pltpu.make_async_remote_copy(src_ref=b_ref.at[2 * cx + cy], dst_ref=o_ref.at[me], send_sem=ssem.at[j],
                                             recv_sem=rsem.at[j], device_id=(cx, cy, c), device_id_type=MESH)
                for j, (cx, cy) in enumerate(chips)]

    @_at_step(i, 0)
    def _():
        for cp in sends():
            cp.start()

    @_at_step(i, nt - 1)
    def _():
        for j, (cx, cy) in enumerate(chips):
            blk = o_ref.at[2 * cx + cy]
            pltpu.make_async_remote_copy(src_ref=blk, dst_ref=blk, send_sem=ssem.at[j], recv_sem=rsem.at[j],
                                         device_id=(cx, cy, c), device_id_type=MESH).wait_recv()
        for cp in sends():
            cp.wait_send()


def _pair_swap(send, name):
    def body(s_ref, r_ref, ssem, rsem):
        x, y, c, _ = _place()
        cp = pltpu.make_async_remote_copy(src_ref=s_ref, dst_ref=r_ref, send_sem=ssem, recv_sem=rsem,
                                          device_id=(x, y, 1 - c), device_id_type=MESH)
        cp.start()
        cp.wait()

    return pl.pallas_call(
        body, name=name, in_specs=[HBM], out_specs=HBM, out_shape=jax.ShapeDtypeStruct(send.shape, send.dtype),
        scratch_shapes=[pltpu.SemaphoreType.DMA, pltpu.SemaphoreType.DMA],
    )(send)


def _chip_exchange(blocks, name):
    def body(b_ref, o_ref, ssem, rsem):
        _exchange_steps(0, 1, b_ref, o_ref, ssem, rsem)

    got = pl.pallas_call(
        body, name=name, in_specs=[HBM], out_specs=HBM, out_shape=jax.ShapeDtypeStruct(blocks.shape, blocks.dtype),
        scratch_shapes=[pltpu.SemaphoreType.DMA((3,)), pltpu.SemaphoreType.DMA((3,))],
    )(blocks)
    return _own_block(got, blocks)


def _own_block(got, blocks):
    me = 2 * lax.axis_index("x") + lax.axis_index("y")
    own = lax.dynamic_slice_in_dim(blocks, me, 1, axis=0)
    return lax.dynamic_update_slice_in_dim(got, own, me, axis=0)


def _pair_gather(half, name):
    c = lax.axis_index("c")
    got = _pair_swap(half, name)
    return jnp.stack([jnp.where(c == 0, half, got), jnp.where(c == 0, got, half)])


def _pair_sum(keep, got):
    n, r2, cdim = keep.shape

    def body(a_ref, b_ref, o_ref):
        o_ref[...] = (a_ref[...].astype(F32) + b_ref[...].astype(F32)).astype(BF16)

    spec = pl.BlockSpec((1, RS_TILE, cdim), lambda k, i: (k, i, 0))
    return pl.pallas_call(
        body, name="rs_pair_sum", grid=(n, r2 // RS_TILE), in_specs=[spec, spec], out_specs=spec,
        out_shape=jax.ShapeDtypeStruct(keep.shape, BF16), compiler_params=_cp(("parallel", "parallel")),
    )(keep, got)


def _chip_sum(slots):
    n, r2, cdim = slots.shape

    def body(s_ref, o_ref):
        o_ref[...] = ((s_ref[0].astype(F32) + s_ref[1].astype(F32)) + s_ref[2].astype(F32)) + s_ref[3].astype(F32)

    return pl.pallas_call(
        body, name="rs_chip_sum", grid=(r2 // RS_TILE,),
        in_specs=[pl.BlockSpec((n, RS_TILE, cdim), lambda i: (0, i, 0))],
        out_specs=pl.BlockSpec((RS_TILE, cdim), lambda i: (i, 0)),
        out_shape=jax.ShapeDtypeStruct((r2, cdim), F32), compiler_params=_cp(("parallel",)),
    )(slots)


def _rs_pair_stage(packed, tag):
    r2 = packed.shape[1] // 2
    c = lax.axis_index("c")
    keep = lax.dynamic_slice_in_dim(packed, c * r2, r2, axis=1)
    send = lax.dynamic_slice_in_dim(packed, (1 - c) * r2, r2, axis=1)
    return _pair_sum(keep, _pair_swap(send, "rs_pair_swap" + tag))


def _rs_finish(slots, tag):
    both = _pair_gather(_chip_sum(slots), "rs_pair_gather" + tag)
    return both.reshape(2 * slots.shape[1], slots.shape[2])


TRANSPOSED = ("ffn1_w_gate", "ffn1_w_up", "ffn2_w_gate", "ffn2_w_up", "w_in")
BIG = [("ffn1_w_gate", (F_SH, D_MODEL)), ("ffn1_w_up", (F_SH, D_MODEL)), ("ffn1_w_down", (F_SH, D_MODEL)),
       ("ffn2_w_gate", (F_SH, D_MODEL)), ("ffn2_w_up", (F_SH, D_MODEL)), ("ffn2_w_down", (F_SH, D_MODEL)),
       ("w_in", (D_IN_SH, D_MODEL)), ("lru_wa", (2, 4, 64, LRU_HD)), ("lru_wx", (2, 4, 64, LRU_HD)),
       ("lru_w_out", (256, D_MODEL)), ("sc_w_out", (SC_W, 256)), ("sgu_w_out", (SGU_W, 256)), ("w_o", (256, D_MODEL))]
FFN_NAMES = [n for n, _ in BIG[:6]]
TAIL_BIG = BIG[7:]
FFN_BLK0 = 3
TAIL_ROW0 = (FFN_BLK0 + 6) * F_SH
TAIL_ROWS = 1152
SMALL = [("lru_conv_w", (4, 256)), ("lru_ba", (2, 256)), ("lru_bx", (2, 256)), ("lru_lambda", (2, 256)),
         ("sc_conv_w", (3, 128))]
REPL = [("ffn1_pre_g", (D_MODEL,)), ("ffn1_post_g", (D_MODEL,)), ("mix_pre_g", (D_MODEL,)), ("lru_conv_b", (LRU_W,)),
        ("sgu_ln_g", (SGU_W,)), ("sgu_ln_b", (SGU_W,)), ("sgu_w_s", (4, CHUNK, CHUNK)), ("sgu_b", (4, CHUNK)),
        ("mix_post_g", (D_MODEL,)), ("ffn2_pre_g", (D_MODEL,)), ("ffn2_post_g", (D_MODEL,))]
WEIGHTS = ['ffn1_pre_g', 'ffn1_w_gate', 'ffn1_w_up', 'ffn1_w_down', 'ffn1_post_g', 'mix_pre_g', 'w_in', 'lru_conv_w',
           'lru_conv_b', 'lru_wa', 'lru_ba', 'lru_wx', 'lru_bx', 'lru_lambda', 'lru_w_out', 'sc_conv_w', 'sc_w_out',
           'sgu_ln_g', 'sgu_ln_b', 'sgu_w_s', 'sgu_b', 'sgu_w_out', 'w_o', 'mix_post_g', 'ffn2_pre_g', 'ffn2_w_gate',
           'ffn2_w_up', 'ffn2_w_down', 'ffn2_post_g']


def _seg_rows(shape):
    return -(-int(np.prod(shape)) // PACK_C)


def _pack(parts, rows_total, dtype):
    lead = parts[0].shape[0]
    segs = []
    used = 0
    for p in parts:
        flat = p.reshape(lead, -1).astype(dtype)
        nr = -(-flat.shape[1] // PACK_C)
        pad = nr * PACK_C - flat.shape[1]
        if pad:
            flat = jnp.pad(flat, ((0, 0), (0, pad)))
        segs.append(flat.reshape(lead, nr, PACK_C))
        used += nr
    if rows_total > used:
        segs.append(jnp.zeros((lead, rows_total - used, PACK_C), dtype))
    return jnp.concatenate(segs, axis=1)


def _unpack(buf, specs):
    lead = buf.shape[0]
    out = {}
    r0 = 0
    for name, shape in specs:
        nr = _seg_rows(shape)
        n = int(np.prod(shape))
        out[name] = buf[:, r0:r0 + nr].reshape(lead, nr * PACK_C)[:, :n].reshape((lead,) + tuple(shape))
        r0 += nr
    return out


def _full_from_shards(name, t):
    if name in ("ffn1_w_gate", "ffn1_w_up", "ffn1_w_down", "ffn2_w_gate", "ffn2_w_up", "ffn2_w_down"):
        return t
    if name == "w_in":
        return t.reshape(D_IN, D_MODEL)
    if name in ("sc_w_out", "sgu_w_out", "lru_conv_w", "lru_ba", "lru_bx", "lru_lambda", "sc_conv_w"):
        return jnp.moveaxis(t, 0, -2).reshape(t.shape[1:-1] + (N_SHARD * t.shape[-1],))
    if name in ("lru_wa", "lru_wx"):
        return jnp.moveaxis(t, 0, 2).reshape(2, 4, LRU_HD, LRU_HD)
    if name in ("lru_w_out", "w_o"):
        return t.reshape(N_SHARD * t.shape[1], t.shape[2])
    raise ValueError(name)


def _shards_from_full(name, gfull):
    if name in ("ffn1_w_gate", "ffn1_w_up", "ffn1_w_down", "ffn2_w_gate", "ffn2_w_up", "ffn2_w_down"):
        return gfull
    if name == "w_in":
        return gfull.reshape(N_SHARD, D_IN_SH, D_MODEL)
    if name in ("sc_w_out", "sgu_w_out", "lru_conv_w", "lru_ba", "lru_bx", "lru_lambda", "sc_conv_w"):
        lastdim = gfull.shape[-1] // N_SHARD
        return jnp.moveaxis(gfull.reshape(gfull.shape[:-1] + (N_SHARD, lastdim)), -2, 0)
    if name in ("lru_wa", "lru_wx"):
        return jnp.moveaxis(gfull.reshape(2, 4, N_SHARD, 64, LRU_HD), 2, 0)
    if name in ("lru_w_out", "w_o"):
        return gfull.reshape(N_SHARD, gfull.shape[0] // N_SHARD, gfull.shape[1])
    raise ValueError(name)


def _tiles(s):
    return dict(ffn=min(512, s), ffn_bwd=min(512, s), tn=min(2048, s), mix_in=min(512, s), lru=min(512, s), mix=min(256, s), loss=min(512, s))


def _mixer_fwd(x, w, t):
    hb, za, zm = _mix_in_fwd(x, w["mix_pre_g"], w["w_in"], t["mix_in"])
    hf = _lru_fwd(za, w["lru_conv_w"], w["lru_conv_b"], w["lru_wa"][0], w["lru_ba"][0:1], w["lru_wx"][0],
                  w["lru_bx"][0:1], w["lru_lambda"][0:1], False, t["lru"])
    hr = _lru_fwd(za, w["lru_conv_w"], w["lru_conv_b"], w["lru_wa"][1], w["lru_ba"][1:2], w["lru_wx"][1],
                  w["lru_bx"][1:2], w["lru_lambda"][1:2], True, t["lru"])
    out, yain, q, ycin, ya, yb, yc, mb, mx = _mix_out_fwd(
        x, za, zm, hf, hr, w["lru_w_out"], w["sc_conv_w"], w["sc_w_out"], w["sgu_ln_g"], w["sgu_ln_b"], w["sgu_w_s"],
        w["sgu_b_t"], w["sgu_w_out"], w["w_o"], w["mix_post_g"], t["mix"])
    return out, dict(x=x, hb=hb, za=za, zm=zm, hf=hf, hr=hr, yain=yain, q=q, ycin=ycin, ya=ya, yb=yb, yc=yc, mb=mb, mx=mx)


def _mixer_bwd(dy, sv, w, t, buf, exchange=None):
    g = {}
    (dmx, dya, dyb, dyc, dgate, dscb, dsu, dsv, dzm, dhs, dcp, g["mix_post_g"], g["sgu_ln_g"], g["sgu_ln_b"],
     g["sgu_w_s"], dbs, *slots) = _mix_out_bwd(
        dy, sv["za"], sv["zm"], sv["hf"], sv["hr"], sv["ya"], sv["yb"], sv["yc"], sv["mx"], w["lru_w_out"], w["sc_conv_w"],
        w["sc_w_out"], w["sgu_ln_g"], w["sgu_ln_b"], w["sgu_w_s"], w["sgu_w_s_t"], w["sgu_b_t"], w["sgu_w_out"], w["w_o"],
        w["mix_post_g"], t["mix"], exchange)
    g["sgu_b"] = jnp.sum(dbs, axis=-1)
    ts = t["tn"]
    g["w_o"] = _tn_plain(sv["mb"], dmx, "dw_o", ts)
    g["lru_w_out"] = _tn_plain(sv["yain"], dya, "dw_lru_out", ts)
    g["sc_w_out"] = _tn_plain(sv["q"], dyb, "dw_sc_out", ts)
    g["sgu_w_out"] = _tn_plain(sv["ycin"], dyc, "dw_sgu_out", ts)
    dxc, dwa, dba, dwx, dbx, dlam = [], [], [], [], [], []
    for d, rev in enumerate((False, True)):
        o = _lru_bwd(sv["za"], sv["hr"] if rev else sv["hf"], dhs, w["lru_conv_w"], w["lru_conv_b"], w["lru_wa"][d],
                     w["lru_ba"][d:d + 1], w["lru_wx"][d], w["lru_bx"][d:d + 1], w["lru_lambda"][d:d + 1], rev, t["lru"])
        for lst, val in zip((dxc, dwa, dba, dwx, dbx, dlam), o):
            lst.append(val)
    g["lru_wa"] = jnp.stack(dwa)
    g["lru_wx"] = jnp.stack(dwx)
    g["lru_ba"] = jnp.concatenate(dba, axis=0)
    g["lru_bx"] = jnp.concatenate(dbx, axis=0)
    g["lru_lambda"] = jnp.concatenate(dlam, axis=0)
    dx, dz, g["mix_pre_g"], g["lru_conv_w"], g["lru_conv_b"], g["sc_conv_w"] = _mix_in_bwd(
        dy, sv["x"], sv["za"], dxc[0], dxc[1], dcp, dgate, dscb, dsu, dsv, dzm, w["mix_pre_g"], w["lru_conv_w"],
        w["sc_conv_w"], w["w_in"], t["mix"])
    buf = _tn(dz, sv["hb"], "dw_in", N_SHARD, (ts, D_IN_SH), lambda b, s_: (s_, b), (ts, D_MODEL), lambda b, s_: (s_, 0),
              None, (None, D_IN_SH, D_MODEL), None, ts, into=(buf, 0))
    return dx, g, buf, (slots[0] if slots else None)


def kernel(x, ffn1_pre_g, ffn1_w_gate, ffn1_w_up, ffn1_w_down, ffn1_post_g, mix_pre_g, w_in, lru_conv_w, lru_conv_b, lru_wa, lru_ba, lru_wx, lru_bx, lru_lambda, lru_w_out, sc_conv_w, sc_w_out, sgu_ln_g, sgu_ln_b, sgu_w_s, sgu_b, sgu_w_out, w_o, mix_post_g, ffn2_pre_g, ffn2_w_gate, ffn2_w_up, ffn2_w_down, ffn2_post_g, loss_target, m_ffn1_pre_g, m_ffn1_w_gate, m_ffn1_w_up, m_ffn1_w_down, m_ffn1_post_g, m_mix_pre_g, m_w_in, m_lru_conv_w, m_lru_conv_b, m_lru_wa, m_lru_ba, m_lru_wx, m_lru_bx, m_lru_lambda, m_lru_w_out, m_sc_conv_w, m_sc_w_out, m_sgu_ln_g, m_sgu_ln_b, m_sgu_w_s, m_sgu_b, m_sgu_w_out, m_w_o, m_mix_post_g, m_ffn2_pre_g, m_ffn2_w_gate, m_ffn2_w_up, m_ffn2_w_down, m_ffn2_post_g, v_ffn1_pre_g, v_ffn1_w_gate, v_ffn1_w_up, v_ffn1_w_down, v_ffn1_post_g, v_mix_pre_g, v_w_in, v_lru_conv_w, v_lru_conv_b, v_lru_wa, v_lru_ba, v_lru_wx, v_lru_bx, v_lru_lambda, v_lru_w_out, v_sc_conv_w, v_sc_w_out, v_sgu_ln_g, v_sgu_ln_b, v_sgu_w_s, v_sgu_b, v_sgu_w_out, v_w_o, v_mix_post_g, v_ffn2_pre_g, v_ffn2_w_gate, v_ffn2_w_up, v_ffn2_w_down, v_ffn2_post_g):
    args = (ffn1_pre_g, ffn1_w_gate, ffn1_w_up, ffn1_w_down, ffn1_post_g, mix_pre_g, w_in, lru_conv_w, lru_conv_b, lru_wa, lru_ba, lru_wx, lru_bx, lru_lambda, lru_w_out, sc_conv_w, sc_w_out, sgu_ln_g, sgu_ln_b, sgu_w_s, sgu_b, sgu_w_out, w_o, mix_post_g, ffn2_pre_g, ffn2_w_gate, ffn2_w_up, ffn2_w_down, ffn2_post_g)
    margs = (m_ffn1_pre_g, m_ffn1_w_gate, m_ffn1_w_up, m_ffn1_w_down, m_ffn1_post_g, m_mix_pre_g, m_w_in, m_lru_conv_w, m_lru_conv_b, m_lru_wa, m_lru_ba, m_lru_wx, m_lru_bx, m_lru_lambda, m_lru_w_out, m_sc_conv_w, m_sc_w_out, m_sgu_ln_g, m_sgu_ln_b, m_sgu_w_s, m_sgu_b, m_sgu_w_out, m_w_o, m_mix_post_g, m_ffn2_pre_g, m_ffn2_w_gate, m_ffn2_w_up, m_ffn2_w_down, m_ffn2_post_g)
    vargs = (v_ffn1_pre_g, v_ffn1_w_gate, v_ffn1_w_up, v_ffn1_w_down, v_ffn1_post_g, v_mix_pre_g, v_w_in, v_lru_conv_w, v_lru_conv_b, v_lru_wa, v_lru_ba, v_lru_wx, v_lru_bx, v_lru_lambda, v_lru_w_out, v_sc_conv_w, v_sc_w_out, v_sgu_ln_g, v_sgu_ln_b, v_sgu_w_s, v_sgu_b, v_sgu_w_out, v_w_o, v_mix_post_g, v_ffn2_pre_g, v_ffn2_w_gate, v_ffn2_w_up, v_ffn2_w_down, v_ffn2_post_g)
    wsh = dict(zip(WEIGHTS, args))
    msh = dict(zip(WEIGHTS, margs))
    vsh = dict(zip(WEIGHTS, vargs))
    xs = x[0]
    s = xs.shape[0]
    t = _tiles(s)

    small = _pack([wsh[n].reshape(1, -1) for n, _ in SMALL], SMALL_ROWS, F32)[0]
    small_all = _all_gather(small, "ag_small")
    small_un = _unpack(small_all, [(n, (DEPTH,) + sh) for n, sh in SMALL])
    packed_w = [_pack([(wsh[n][l].T if n in TRANSPOSED else wsh[n][l])[None] for n, _ in BIG], AG_ROWS, BF16)[0]
                for l in range(DEPTH)]

    def layer_weights(l, gathered):
        un = _unpack(gathered, BIG)
        w = {n: _full_from_shards(n, un[n]) for n, _ in BIG}
        for n, _ in SMALL:
            w[n] = _full_from_shards(n, small_un[n][:, l])
        for n, _ in REPL:
            w[n] = wsh[n][l]
        for n in ("ffn1_pre_g", "ffn1_post_g", "mix_pre_g", "lru_conv_b", "sgu_ln_g", "sgu_ln_b", "mix_post_g", "ffn2_pre_g",
                  "ffn2_post_g"):
            w[n] = w[n].reshape(1, -1)
        w["sgu_w_s_t"] = jnp.swapaxes(w["sgu_w_s"], 1, 2).astype(BF16)
        w["sgu_w_s"] = w["sgu_w_s"].astype(BF16)
        w["sgu_b_t"] = w["sgu_b"].T
        return w

    saved = []
    layers = []
    cur = xs
    gathered = _all_gather(packed_w[0], "ag_weights_l0")
    for l in range(DEPTH):
        w = layer_weights(l, gathered)
        layers.append(w)
        sv = {}
        x1, *rest = _ffn_fwd(cur, w["ffn1_pre_g"], w["ffn1_w_gate"], w["ffn1_w_up"], w["ffn1_w_down"], w["ffn1_post_g"], t["ffn"],
                             gather=packed_w[l + 1] if l + 1 < DEPTH else None)
        if l + 1 < DEPTH:
            gathered = rest.pop()
        sv["ffn1"] = (cur, *rest)
        x2, sv["mix"] = _mixer_fwd(x1, w, t)
        x3, *rest = _ffn_fwd(x2, w["ffn2_pre_g"], w["ffn2_w_gate"], w["ffn2_w_up"], w["ffn2_w_down"], w["ffn2_post_g"], t["ffn"])
        sv["ffn2"] = (x2, *rest)
        saved.append(sv)
        cur = x3

    dy, sq = _loss_head(cur, loss_target[0], t["loss"])
    loss = lax.psum(0.5 * jnp.sum(sq) / D_MODEL, ("x", "y", "c"))

    grads = [None] * DEPTH

    def finish(layer, slots):
        total = _rs_finish(slots, "_l%d" % layer)
        un = _unpack(total[None, TAIL_ROW0:TAIL_ROW0 + TAIL_ROWS], TAIL_BIG + SMALL + REPL)
        un["w_in"] = total[None, 0:D_IN_SH]
        for j, n in enumerate(FFN_NAMES):
            un[n] = total[None, (FFN_BLK0 + j) * F_SH:(FFN_BLK0 + j + 1) * F_SH]
        grads[layer] = {n: (un[n][0].T if n in TRANSPOSED else un[n][0]) for n in WEIGHTS}

    pending = None
    for l in reversed(range(DEPTH)):
        w = layers[l]
        sv = saved[l]
        g = {}
        buf = jnp.zeros((N_SHARD, RS_ROWS, PACK_C), BF16)
        for tag in ("ffn2", "mix", "ffn1"):
            if tag == "mix":
                dy, gm, buf, slots = _mixer_bwd(dy, sv["mix"], w, t, buf, pending)
                if pending is not None:
                    finish(l + 1, slots)
                g.update(gm)
                continue
            xin, ab, d1, d2, ff = sv[tag]
            dy, hb, dgb, dub, dfb, g[tag + "_pre_g"], g[tag + "_post_g"] = _ffn_bwd(
                dy, xin, d1, d2, ff, w[tag + "_pre_g"], w[tag + "_post_g"], w[tag + "_w_gate"], w[tag + "_w_up"],
                w[tag + "_w_down"], t["ffn_bwd"])
            buf = _ffn_wgrads(hb, ab, dgb, dub, dfb, t["tn"], buf, FFN_BLK0 + (3 if tag == "ffn2" else 0))
        parts = [_shards_from_full(n, g[n]) for n, _ in TAIL_BIG] + [_shards_from_full(n, g[n]) for n, _ in SMALL]
        parts += [jnp.broadcast_to(g[n].reshape((1,) + sh), (N_SHARD,) + sh) for n, sh in REPL]
        buf = lax.dynamic_update_slice(buf, _pack(parts, TAIL_ROWS, BF16), (0, TAIL_ROW0, 0))
        pending = _rs_pair_stage(buf, "_l%d" % l)
    finish(0, _chip_exchange(pending, "rs_chip_exchange_l0"))
    grad_x = dy[None]

    gw, dw, nm, nv = [], [], [], []
    for n in WEIGHTS:
        gfull = jnp.stack([grads[l][n] for l in range(DEPTH)])
        d_, m_, v_ = _adamw(wsh[n], gfull, msh[n], vsh[n], n)
        gw.append(gfull)
        dw.append(d_)
        nm.append(m_)
        nv.append(v_)
    return (loss, grad_x, *gw, *dw, *nm, *nv)
```

```python
import jax
import jax.numpy as jnp
import numpy as np
from jax import lax
from jax.experimental import pallas as pl
from jax.experimental.pallas import tpu as pltpu

F32 = jnp.float32
BF16 = jnp.bfloat16
MESH = pl.DeviceIdType.MESH

D_MODEL = 1024
D_FF = 2816
N_SHARD = 4
F_SH = D_FF // N_SHARD
D_IN = 7680
D_IN_SH = D_IN // N_SHARD
ZA_W = 4608
ZM_W = 3072
LRU_W = 1024
LRU_HD = 256
SC_W = 512
SGU_W = 512
CHUNK = 128
DEPTH = 4
EPS = 1e-6
LRU_C = 8.0
LRU_RB = 16

ADAM_LR, ADAM_B1, ADAM_B2, ADAM_EPS, ADAM_WD, ADAM_STEP = 0.001, 0.9, 0.999, 1e-08, 0.01, 10

VMEM_LIMIT = 56 * 2 ** 20
PACK_C = 1024
AG_ROWS = 7168
RS_ROWS = 7680
RS_TILE = 192
SMALL_ROWS = 32


def _cp(sem=("arbitrary",)):
    return pltpu.CompilerParams(dimension_semantics=sem, vmem_limit_bytes=VMEM_LIMIT)


def _full(shape):
    return pl.BlockSpec(shape, lambda *_: (0,) * len(shape))


def _res(shape):
    return pl.BlockSpec(shape, lambda *_: (0,) * len(shape), pipeline_mode=pl.Buffered(1))


def _dot(a, b):
    return jnp.dot(a, b, preferred_element_type=F32)


def _dot_nt(a, b):
    return lax.dot_general(a, b, (((1,), (1,)), ((), ())), preferred_element_type=F32)


def _dot_tn(a, b):
    return lax.dot_general(a, b, (((0,), (0,)), ((), ())), preferred_element_type=F32)


def _sigmoid(x):
    return 0.5 * jnp.tanh(0.5 * x) + 0.5


_GELU_K = 0.7978845608028654
_GELU_C = 0.044715


def _gelu(x):
    return 0.5 * x * (1.0 + jnp.tanh(_GELU_K * (x + _GELU_C * x * x * x)))


def _gelu_and_grad(x):
    t = jnp.tanh(_GELU_K * (x + _GELU_C * x * x * x))
    g = 0.5 * x * (1.0 + t)
    dg = 0.5 * (1.0 + t) + 0.5 * x * (1.0 - t * t) * (_GELU_K * (1.0 + 3.0 * _GELU_C * x * x))
    return g, dg


def _rms_fwd(x, g):
    rs = lax.rsqrt(jnp.mean(x * x, axis=-1, keepdims=True) + EPS)
    return x * rs * g


def _rms_bwd(dy, x, g):
    rs = lax.rsqrt(jnp.mean(x * x, axis=-1, keepdims=True) + EPS)
    n = x * rs
    dn = dy * g
    dx = rs * (dn - n * jnp.mean(dn * n, axis=-1, keepdims=True))
    return dx, jnp.sum(dy * n, axis=0, keepdims=True)


def _acc(ref, val, first):
    @pl.when(first)
    def _():
        ref[...] = val

    @pl.when(jnp.logical_not(first))
    def _():
        ref[...] += val


def _shift(xm, d, prev, nxt, rows):
    tm = xm.shape[0]
    if d == 0:
        return xm
    y = pltpu.roll(xm, (-d) % tm, 0)
    rows8 = rows[0:8]
    if d < 0:
        hb = prev.shape[0]
        top = y[0:8]
        for r in range(-d):
            top = jnp.where(rows8 == r, prev[hb + r + d:hb + r + d + 1, :], top)
        return jnp.concatenate([top, y[8:]], axis=0)
    bot = y[tm - 8:]
    for r in range(d):
        bot = jnp.where(rows8 == 8 - d + r, nxt[r:r + 1, :], bot)
    return jnp.concatenate([y[:tm - 8], bot], axis=0)


def _halo(tm, hb, w, col, nt, rev=False):
    r = tm // hb
    last = nt * r - 1
    ti = (lambda i: nt - 1 - i) if rev else (lambda i: i)
    return [pl.BlockSpec((tm, w), lambda i: (ti(i), col)),
            pl.BlockSpec((hb, w), lambda i: (jnp.maximum(ti(i) * r - 1, 0), col)),
            pl.BlockSpec((hb, w), lambda i: (jnp.minimum((ti(i) + 1) * r, last), col))]


def _edges(prev_ref, next_ref, ti, nt):
    prev = jnp.where(ti > 0, prev_ref[...].astype(F32), 0.0)
    nxt = jnp.where(ti < nt - 1, next_ref[...].astype(F32), 0.0)
    return prev, nxt


def _ffn_fwd(x, pre_g, wg, wu, wd, post_g, tm, gather=None):
    s = x.shape[0]
    nt = s // tm

    def body(x_ref, pg_ref, wg_ref, wu_ref, wd_ref, qg_ref, *rest):
        if gather is None:
            o_ref, a_ref, d1_ref, d2_ref, f_ref = rest
        else:
            pk_ref, o_ref, a_ref, d1_ref, d2_ref, f_ref, gat_ref, ssem, rsem = rest
            _gather_steps(pl.program_id(0), nt, pk_ref, gat_ref, ssem, rsem)
        xv = x_ref[...]
        hb = _rms_fwd(xv, pg_ref[...]).astype(BF16)
        f = jnp.zeros((tm, D_MODEL), F32)
        for k in range(N_SHARD):
            g = _dot_nt(hb, wg_ref[k])
            u = _dot_nt(hb, wu_ref[k])
            sg = _sigmoid(g)
            silu = g * sg
            ab = (silu * u).astype(BF16)
            a_ref[k] = ab
            d1_ref[k] = (u * (sg * (1.0 + g * (1.0 - sg)))).astype(BF16)
            d2_ref[k] = silu.astype(BF16)
            f = f + _dot(ab, wd_ref[k])
        f_ref[...] = f.astype(BF16)
        o_ref[...] = xv + 0.5 * _rms_fwd(f, qg_ref[...])

    row = pl.BlockSpec((tm, D_MODEL), lambda i: (i, 0))
    gu = pl.BlockSpec((N_SHARD, tm, F_SH), lambda i: (0, i, 0))
    in_specs = [row, _full((1, D_MODEL)), _res((N_SHARD, F_SH, D_MODEL)), _res((N_SHARD, F_SH, D_MODEL)),
                _res((N_SHARD, F_SH, D_MODEL)), _full((1, D_MODEL))]
    out_specs = [row, gu, gu, gu, row]
    out_shape = [jax.ShapeDtypeStruct((s, D_MODEL), F32), jax.ShapeDtypeStruct((N_SHARD, s, F_SH), BF16),
                 jax.ShapeDtypeStruct((N_SHARD, s, F_SH), BF16), jax.ShapeDtypeStruct((N_SHARD, s, F_SH), BF16),
                 jax.ShapeDtypeStruct((s, D_MODEL), BF16)]
    args = [x, pre_g, wg, wu, wd, post_g]
    scratch = []
    if gather is not None:
        in_specs.append(HBM)
        out_specs.append(HBM)
        out_shape.append(jax.ShapeDtypeStruct((N_SHARD,) + gather.shape, gather.dtype))
        args.append(gather)
        scratch = [pltpu.SemaphoreType.DMA((6,)), pltpu.SemaphoreType.DMA((6,))]
    outs = list(pl.pallas_call(
        body, name="ffn_fwd" if gather is None else "ffn_fwd_gather", grid=(nt,),
        in_specs=in_specs, out_specs=out_specs, out_shape=out_shape, scratch_shapes=scratch,
        compiler_params=_cp(("parallel",) if gather is None else ("arbitrary",)),
    )(*args))
    if gather is not None:
        outs[-1] = _own_slot(outs[-1], gather)
    return outs


def _ffn_bwd(dy, x, d1, d2, f, pre_g, post_g, wg, wu, wd, tm):
    s = x.shape[0]
    nt = s // tm

    def body_a(dy_ref, d1_ref, d2_ref, f_ref, qg_ref, wd_ref, dg_ref, du_ref, df_ref, dqg_ref):
        df, dq = _rms_bwd(0.5 * dy_ref[...], f_ref[...].astype(F32), qg_ref[...])
        dfb = df.astype(BF16)
        df_ref[...] = dfb
        for k in range(N_SHARD):
            da = _dot_nt(dfb, wd_ref[k])
            dg_ref[k] = (da * d1_ref[k].astype(F32)).astype(BF16)
            du_ref[k] = (da * d2_ref[k].astype(F32)).astype(BF16)
        _acc(dqg_ref, dq, pl.program_id(0) == 0)

    def body_b(dy_ref, x_ref, dg_ref, du_ref, pg_ref, wg_ref, wu_ref, dx_ref, h_ref, dpg_ref):
        xv = x_ref[...]
        h_ref[...] = _rms_fwd(xv, pg_ref[...]).astype(BF16)
        dh = jnp.zeros((tm, D_MODEL), F32)
        for k in range(N_SHARD):
            dh = dh + _dot(dg_ref[k], wg_ref[k]) + _dot(du_ref[k], wu_ref[k])
        dxn, dp = _rms_bwd(dh, xv, pg_ref[...])
        dx_ref[...] = dy_ref[...] + dxn
        _acc(dpg_ref, dp, pl.program_id(0) == 0)

    row = pl.BlockSpec((tm, D_MODEL), lambda i: (i, 0))
    gu = pl.BlockSpec((N_SHARD, tm, F_SH), lambda i: (0, i, 0))
    vec = _full((1, D_MODEL))
    wsp = _res((N_SHARD, F_SH, D_MODEL))
    big = jax.ShapeDtypeStruct((N_SHARD, s, F_SH), BF16)
    tok_bf = jax.ShapeDtypeStruct((s, D_MODEL), BF16)
    vec_f = jax.ShapeDtypeStruct((1, D_MODEL), F32)
    dg, du, df, dqg = pl.pallas_call(
        body_a, name="ffn_bwd_a", grid=(nt,),
        in_specs=[row, gu, gu, row, vec, wsp],
        out_specs=[gu, gu, row, vec],
        out_shape=[big, big, tok_bf, vec_f],
        compiler_params=_cp(),
    )(dy, d1, d2, f, post_g, wd)
    dx, h, dpg = pl.pallas_call(
        body_b, name="ffn_bwd_b", grid=(nt,),
        in_specs=[row, row, gu, gu, vec, wsp, wsp],
        out_specs=[row, row, vec],
        out_shape=[jax.ShapeDtypeStruct((s, D_MODEL), F32), tok_bf, vec_f],
        compiler_params=_cp(),
    )(dy, x, dg, du, pre_g, wg, wu)
    return dx, h, dg, du, df, dpg, dqg


def _tn(lhs, rhs, name, nb, lhs_blk, lhs_map, rhs_blk, rhs_map, out_shape, out_blk, out_map, ts, into=None):
    s = lhs.shape[-2]
    ns = s // ts
    acc_shape = tuple(d for d in out_blk if d is not None)

    def body(l_ref, r_ref, *rest):
        o_ref, acc_s = rest[-2:]
        t = pl.program_id(1)
        _acc(acc_s, _dot_tn(l_ref[...], r_ref[...]), t == 0)

        @pl.when(t == ns - 1)
        def _():
            o_ref[...] = acc_s[...].astype(BF16)

    in_specs = [pl.BlockSpec(lhs_blk, lhs_map), pl.BlockSpec(rhs_blk, rhs_map)]
    args = (lhs, rhs)
    aliases = {}
    if into is not None:
        buf, blk_row = into
        in_specs.append(HBM)
        args = (lhs, rhs, buf)
        aliases = {2: 0}
        out_shape = buf.shape
        out_map = lambda b, t: (b, blk_row, 0)
    return pl.pallas_call(
        body, name=name, grid=(nb, ns),
        in_specs=in_specs,
        out_specs=pl.BlockSpec(out_blk, out_map),
        out_shape=jax.ShapeDtypeStruct(out_shape, BF16),
        scratch_shapes=[pltpu.VMEM(acc_shape, F32)],
        input_output_aliases=aliases,
        compiler_params=_cp(("parallel", "arbitrary")),
    )(*args)


def _tn_plain(lhs, rhs, name, ts):
    m, n = lhs.shape[1], rhs.shape[1]
    return _tn(lhs, rhs, name, 1, (ts, m), lambda b, t: (t, 0), (ts, n), lambda b, t: (t, 0),
               (m, n), (m, n), lambda b, t: (0, 0), ts)


def _ffn_wgrads(h, a, dg, du, df, ts, buf, blk0):
    lm2 = lambda b, t: (t, 0)
    bm3 = lambda b, t: (b, t, 0)
    for j, (lhs, rhs, name) in enumerate(((dg, h, "ffn_dwg"), (du, h, "ffn_dwu"), (a, df, "ffn_dwd"))):
        buf = _tn(lhs, rhs, name, N_SHARD, (None, ts, F_SH), bm3, (ts, D_MODEL), lm2, None, (None, F_SH, D_MODEL), None, ts,
                  into=(buf, blk0 + j))
    return buf


def _mix_in_fwd(x, pre_g, w_in, tm):
    s = x.shape[0]
    nt = s // tm
    cw = 512

    def body(x_ref, pg_ref, w_ref, h_ref, za_ref, zm_ref):
        hb = _rms_fwd(x_ref[...], pg_ref[...]).astype(BF16)
        h_ref[...] = hb
        for j in range(ZA_W // cw):
            za_ref[:, j * cw:(j + 1) * cw] = _dot_nt(hb, w_ref[j * cw:(j + 1) * cw, :]).astype(BF16)
        for j in range(ZM_W // cw):
            zm_ref[:, j * cw:(j + 1) * cw] = _dot_nt(hb, w_ref[ZA_W + j * cw:ZA_W + (j + 1) * cw, :]).astype(BF16)

    row = pl.BlockSpec((tm, D_MODEL), lambda i: (i, 0))
    return pl.pallas_call(
        body, name="mix_in_fwd", grid=(nt,),
        in_specs=[row, _full((1, D_MODEL)), _res((D_IN, D_MODEL))],
        out_specs=[row, pl.BlockSpec((tm, ZA_W), lambda i: (i, 0)), pl.BlockSpec((tm, ZM_W), lambda i: (i, 0))],
        out_shape=[jax.ShapeDtypeStruct((s, D_MODEL), BF16), jax.ShapeDtypeStruct((s, ZA_W), BF16),
                   jax.ShapeDtypeStruct((s, ZM_W), BF16)],
        compiler_params=_cp(("parallel",)),
    )(x, pre_g, w_in)


def _lru_conv(xm, prev, nxt, rows, cw_ref, cb_ref):
    acc = cb_ref[...] + cw_ref[2:3, :] * xm
    acc = acc + cw_ref[0:1, :] * _shift(xm, -2, prev, nxt, rows)
    acc = acc + cw_ref[1:2, :] * _shift(xm, -1, prev, nxt, rows)
    acc = acc + cw_ref[3:4, :] * _shift(xm, 1, prev, nxt, rows)
    return acc


def _lru_preact(x_ref, xp_ref, xn_ref, cw_ref, cb_ref, wa_ref, ba_ref, wx_ref, bx_ref, lam_ref, ti, nt, tm,
                xc_s, ra_s, xa_s, c8_s):
    rows = lax.broadcasted_iota(jnp.int32, (tm, 1), 0)
    prev, nxt = _edges(xp_ref, xn_ref, ti, nt)
    xc = _lru_conv(x_ref[...].astype(F32), prev, nxt, rows, cw_ref, cb_ref)
    xc_s[...] = xc
    xcb = xc.astype(BF16)
    for h in range(4):
        cs_ = slice(LRU_HD * h, LRU_HD * (h + 1))
        ra_s[:, cs_] = _dot(xcb[:, cs_], wa_ref[h]) + ba_ref[:, cs_]
        xa_s[:, cs_] = _dot(xcb[:, cs_], wx_ref[h]) + bx_ref[:, cs_]
    lam = lam_ref[...]
    e = jnp.exp(-jnp.abs(lam))
    log1p_e = jnp.where(e < 1e-2, e * (1.0 - e * (0.5 - e * (1.0 / 3.0))), jnp.log(1.0 + e))
    c8_s[...] = jnp.broadcast_to(-LRU_C * (jnp.maximum(-lam, 0.0) + log1p_e), c8_s.shape)
    return xcb


def _lru_decay(cl, r):
    la = cl * r
    a = jnp.exp(la)
    y2 = 2.0 * la
    em = jnp.where(y2 > -0.004, y2 * (-1.0 + y2 * (-0.5 - y2 * (1.0 / 6.0))), 1.0 - a * a)
    return a, jnp.sqrt(em)


def _scan_rows(a_s, b_s, o_s, carry, tm, descending):
    nb = tm // 8

    def blk(j, h):
        jb = (nb - 1 - j) if descending else j
        base = pl.multiple_of(jb * 8, 8)
        for r in range(8):
            t = base + ((7 - r) if descending else r)
            h = a_s[pl.ds(t, 1), :] * h + b_s[pl.ds(t, 1), :]
            o_s[pl.ds(t, 1), :] = h
        return h

    return lax.fori_loop(0, nb, blk, carry)


def _lru_fwd(za, conv_w, conv_b, wa, ba, wx, bx, lam, rev, tm):
    s = za.shape[0]
    nt = s // tm

    def body(x_ref, xp_ref, xn_ref, cw_ref, cb_ref, wa_ref, ba_ref, wx_ref, bx_ref, lam_ref, h_ref,
             xc_s, ra_s, xa_s, o_s, c8_s, c_s):
        i = pl.program_id(0)
        ti = (nt - 1 - i) if rev else i
        _lru_preact(x_ref, xp_ref, xn_ref, cw_ref, cb_ref, wa_ref, ba_ref, wx_ref, bx_ref, lam_ref, ti, nt, tm,
                    xc_s, ra_s, xa_s, c8_s)

        def gate_blk(j, carry):
            rws = pl.ds(pl.multiple_of(j * LRU_RB, LRU_RB), LRU_RB)
            r = _sigmoid(ra_s[rws, :])
            ig = _sigmoid(xa_s[rws, :])
            a, mult = _lru_decay(c8_s[...], r)
            ra_s[rws, :] = a
            xa_s[rws, :] = ig * xc_s[rws, :] * mult
            return carry

        lax.fori_loop(0, tm // LRU_RB, gate_blk, 0)

        @pl.when(i == 0)
        def _():
            c_s[...] = jnp.zeros_like(c_s)

        c_s[...] = _scan_rows(ra_s, xa_s, o_s, c_s[...], tm, rev)
        h_ref[...] = o_s[...].astype(BF16)

    vec = _full((1, LRU_W))
    hd = _full((4, LRU_HD, LRU_HD))
    ti = (lambda i: nt - 1 - i) if rev else (lambda i: i)
    tile = pltpu.VMEM((tm, LRU_W), F32)
    return pl.pallas_call(
        body, name="lru_fwd_rev" if rev else "lru_fwd", grid=(nt,),
        in_specs=_halo(tm, 16, LRU_W, 1, nt, rev) + [_full((4, LRU_W)), vec, hd, vec, hd, vec, vec],
        out_specs=pl.BlockSpec((tm, LRU_W), lambda i: (ti(i), 0)),
        out_shape=jax.ShapeDtypeStruct((s, LRU_W), BF16),
        scratch_shapes=[tile, tile, tile, tile, pltpu.VMEM((LRU_RB, LRU_W), F32), pltpu.VMEM((1, LRU_W), F32)],
        compiler_params=_cp(),
    )(za, za, za, conv_w, conv_b, wa, ba, wx, bx, lam)


def _sc_conv(c_ref, cp_ref, cn_ref, x_ref, xp_ref, xn_ref, w_ref, ti, nt, rows):
    cprev, cnext = _edges(cp_ref, cn_ref, ti, nt)
    xprev, xnext = _edges(xp_ref, xn_ref, ti, nt)
    cv = c_ref[...].astype(F32)
    xv = x_ref[...].astype(F32)
    p = cv * xv
    pm1 = _shift(p, -1, cprev * xprev, cnext * xnext, rows)
    pp1 = _shift(p, 1, cprev * xprev, cnext * xnext, rows)
    conv = w_ref[0:1, :] * pm1 + w_ref[1:2, :] * p + w_ref[2:3, :] * pp1
    return cv, xv, p, pm1, pp1, conv


def _sgu_norm(vz, g_ref, b_ref):
    vg, dvg = _gelu_and_grad(vz)
    mu = jnp.mean(vg, axis=-1, keepdims=True)
    cen = vg - mu
    rs = lax.rsqrt(jnp.mean(cen * cen, axis=-1, keepdims=True) + EPS)
    vn = cen * rs
    return dvg, vn, rs, vn * g_ref[...] + b_ref[...]


def _mix_out_fwd(x, za, zm, hf, hr, lru_w_out, sc_conv_w, sc_w_out, ln_g, ln_b, w_s, b_s_t, sgu_w_out, w_o, post_g, tm):
    s = x.shape[0]
    nt = s // tm
    nc = tm // CHUNK

    def body(x_ref, gate_ref, scb_ref, scc_ref, sccp_ref, sccn_ref, scx_ref, scxp_ref, scxn_ref, su_ref, sv_ref,
             zm_ref, hf_ref, hr_ref, wlo_ref, scw_ref, wso_ref, lg_ref, lb_ref, ws_ref, bs_ref, wgo_ref, wo_ref,
             qg_ref, o_ref, yain_ref, q_ref, ycin_ref, ya_ref, yb_ref, yc_ref, m_ref, mx_ref, mixed_s):
        ti = pl.program_id(0)
        rows = lax.broadcasted_iota(jnp.int32, (tm, 1), 0)
        hs = hf_ref[...].astype(F32) + hr_ref[...].astype(F32)
        yain = (hs * _gelu(gate_ref[...].astype(F32))).astype(BF16)
        yain_ref[...] = yain
        ya = _dot(yain, wlo_ref[...])
        _, _, _, _, _, conv = _sc_conv(scc_ref, sccp_ref, sccn_ref, scx_ref, scxp_ref, scxn_ref, scw_ref, ti, nt, rows)
        qb = (scb_ref[...].astype(F32) * conv).astype(BF16)
        q_ref[...] = qb
        yb = _dot(qb, wso_ref[...])
        _, _, _, v = _sgu_norm(sv_ref[...].astype(F32), lg_ref, lb_ref)
        vb = v.astype(BF16)
        for n in range(nc):
            for g in range(4):
                blk = vb[n * CHUNK:(n + 1) * CHUNK, g * CHUNK:(g + 1) * CHUNK]
                mixed_s[n * CHUNK:(n + 1) * CHUNK, g * CHUNK:(g + 1) * CHUNK] = _dot(ws_ref[g], blk) + bs_ref[:, g:g + 1]
        ycin = (_gelu(su_ref[...].astype(F32)) * mixed_s[...]).astype(BF16)
        ycin_ref[...] = ycin
        yc = _dot(ycin, wgo_ref[...])
        m = (_sigmoid(zm_ref[:, 0:D_MODEL].astype(F32)) * ya + _sigmoid(zm_ref[:, D_MODEL:2 * D_MODEL].astype(F32)) * yb
             + _sigmoid(zm_ref[:, 2 * D_MODEL:3 * D_MODEL].astype(F32)) * yc)
        mb = m.astype(BF16)
        mx = _dot(mb, wo_ref[...])
        ya_ref[...] = ya.astype(BF16)
        yb_ref[...] = yb.astype(BF16)
        yc_ref[...] = yc.astype(BF16)
        m_ref[...] = mb
        mx_ref[...] = mx.astype(BF16)
        o_ref[...] = x_ref[...] + _rms_fwd(mx, qg_ref[...])

    row = pl.BlockSpec((tm, D_MODEL), lambda i: (i, 0))
    half = pl.BlockSpec((tm, 512), lambda i: (i, 0))
    col = lambda c: pl.BlockSpec((tm, 512), lambda i: (i, c))
    in_specs = ([row, pl.BlockSpec((tm, LRU_W), lambda i: (i, 0)), col(4)] + _halo(tm, 16, SC_W, 5, nt) + _halo(tm, 16, SC_W, 6, nt)
                + [col(7), col(8), pl.BlockSpec((tm, ZM_W), lambda i: (i, 0)), row, row,
                   _full((LRU_W, D_MODEL)), _full((3, SC_W)), _full((SC_W, D_MODEL)), _full((1, SGU_W)), _full((1, SGU_W)),
                   _full((4, CHUNK, CHUNK)), _full((CHUNK, 4)), _full((SGU_W, D_MODEL)), _full((D_MODEL, D_MODEL)),
                   _full((1, D_MODEL))])
    bf = lambda w: jax.ShapeDtypeStruct((s, w), BF16)
    return pl.pallas_call(
        body, name="mix_out_fwd", grid=(nt,),
        in_specs=in_specs,
        out_specs=[row, row, half, half, row, row, row, row, row],
        out_shape=[jax.ShapeDtypeStruct((s, D_MODEL), F32), bf(LRU_W), bf(SC_W), bf(SGU_W), bf(D_MODEL), bf(D_MODEL),
                   bf(D_MODEL), bf(D_MODEL), bf(D_MODEL)],
        scratch_shapes=[pltpu.VMEM((tm, SGU_W), F32)],
        compiler_params=_cp(("parallel",)),
    )(x, za, za, za, za, za, za, za, za, za, za, zm, hf, hr, lru_w_out, sc_conv_w, sc_w_out, ln_g, ln_b, w_s, b_s_t,
      sgu_w_out, w_o, post_g)


def _mix_out_bwd(dy, za, zm, hf, hr, ya, yb, yc, mx, lru_w_out, sc_conv_w, sc_w_out, ln_g, ln_b, w_s, w_s_t, b_s_t,
                 sgu_w_out, w_o, post_g, tm, exchange=None):
    s = dy.shape[0]
    nt = s // tm
    nc = tm // CHUNK

    n_in, n_out = 29, 16

    def body(*refs):
        if exchange is not None:
            _exchange_steps(pl.program_id(0), nt, refs[n_in], refs[n_in + 1 + n_out], refs[-2], refs[-1])
            refs = refs[:n_in] + refs[n_in + 1:n_in + 1 + n_out] + refs[n_in + 2 + n_out:-2]
        compute(*refs)

    def compute(dy_ref, gate_ref, scb_ref, scc_ref, sccp_ref, sccn_ref, scx_ref, scxp_ref, scxn_ref, su_ref, sv_ref,
                zm_ref, hf_ref, hr_ref, ya_ref, yb_ref, yc_ref, mx_ref, wlo_ref, scw_ref, wso_ref, lg_ref, lb_ref,
                ws_ref, wst_ref, bs_ref, wgo_ref, wo_ref, qg_ref,
                dmx_ref, dya_ref, dyb_ref, dyc_ref, dgate_ref, dscb_ref, dsu_ref, dsv_ref, dzm_ref, dhs_ref, dcp_ref,
                dqg_ref, dlg_ref, dlb_ref, dws_ref, dbs_ref, mixed_s, dv_s):
        ti = pl.program_id(0)
        first = ti == 0
        rows = lax.broadcasted_iota(jnp.int32, (tm, 1), 0)
        dmx, dq = _rms_bwd(dy_ref[...], mx_ref[...].astype(F32), qg_ref[...])
        _acc(dqg_ref, dq, first)
        dmxb = dmx.astype(BF16)
        dmx_ref[...] = dmxb
        dm = _dot_nt(dmxb, wo_ref[...])
        dys = []
        for k, (y_ref, d_ref) in enumerate(((ya_ref, dya_ref), (yb_ref, dyb_ref), (yc_ref, dyc_ref))):
            gk = _sigmoid(zm_ref[:, k * D_MODEL:(k + 1) * D_MODEL].astype(F32))
            dyk = (dm * gk).astype(BF16)
            d_ref[...] = dyk
            dys.append(dyk)
            dzm_ref[:, k * D_MODEL:(k + 1) * D_MODEL] = (dm * y_ref[...].astype(F32) * gk * (1.0 - gk)).astype(BF16)
        dyain = _dot_nt(dys[0], wlo_ref[...])
        gg, dgg = _gelu_and_grad(gate_ref[...].astype(F32))
        hs = hf_ref[...].astype(F32) + hr_ref[...].astype(F32)
        dhs_ref[...] = dyain * gg
        dgate_ref[...] = (dyain * hs * dgg).astype(BF16)
        dq_b = _dot_nt(dys[1], wso_ref[...])
        _, _, _, _, _, conv = _sc_conv(scc_ref, sccp_ref, sccn_ref, scx_ref, scxp_ref, scxn_ref, scw_ref, ti, nt, rows)
        dscb_ref[...] = (dq_b * conv).astype(BF16)
        dcp_ref[...] = dq_b * scb_ref[...].astype(F32)
        dycin = _dot_nt(dys[2], wgo_ref[...])
        dvg, vn, rs, v = _sgu_norm(sv_ref[...].astype(F32), lg_ref, lb_ref)
        vb = v.astype(BF16)
        ug, dug = _gelu_and_grad(su_ref[...].astype(F32))
        dmixed = dycin * ug
        dmb = dmixed.astype(BF16)
        dws = [jnp.zeros((CHUNK, CHUNK), F32) for _ in range(4)]
        dbs = [jnp.zeros((CHUNK, CHUNK), F32) for _ in range(4)]
        for n in range(nc):
            for g in range(4):
                rs_, cs_ = slice(n * CHUNK, (n + 1) * CHUNK), slice(g * CHUNK, (g + 1) * CHUNK)
                mixed_s[rs_, cs_] = _dot(ws_ref[g], vb[rs_, cs_]) + bs_ref[:, g:g + 1]
                dv_s[rs_, cs_] = _dot(wst_ref[g], dmb[rs_, cs_])
                dws[g] = dws[g] + _dot_nt(dmb[rs_, cs_], vb[rs_, cs_])
                dbs[g] = dbs[g] + dmixed[rs_, cs_]
        for g in range(4):
            _acc(dws_ref.at[g], dws[g], first)
            _acc(dbs_ref.at[g], dbs[g], first)
        dsu_ref[...] = (dycin * mixed_s[...] * dug).astype(BF16)
        dv = dv_s[...]
        _acc(dlg_ref, jnp.sum(dv * vn, axis=0, keepdims=True), first)
        _acc(dlb_ref, jnp.sum(dv, axis=0, keepdims=True), first)
        dvn = dv * lg_ref[...]
        dcen = rs * (dvn - jnp.mean(dvn, axis=-1, keepdims=True) - vn * jnp.mean(dvn * vn, axis=-1, keepdims=True))
        dsv_ref[...] = (dcen * dvg).astype(BF16)

    row = pl.BlockSpec((tm, D_MODEL), lambda i: (i, 0))
    half = pl.BlockSpec((tm, 512), lambda i: (i, 0))
    col = lambda c: pl.BlockSpec((tm, 512), lambda i: (i, c))
    zmrow = pl.BlockSpec((tm, ZM_W), lambda i: (i, 0))
    sq = _full((4, CHUNK, CHUNK))
    in_specs = ([row, pl.BlockSpec((tm, LRU_W), lambda i: (i, 0)), col(4)] + _halo(tm, 16, SC_W, 5, nt) + _halo(tm, 16, SC_W, 6, nt)
                + [col(7), col(8), zmrow, row, row, row, row, row, row,
                   _full((LRU_W, D_MODEL)), _full((3, SC_W)), _full((SC_W, D_MODEL)), _full((1, SGU_W)), _full((1, SGU_W)),
                   sq, sq, _full((CHUNK, 4)), _full((SGU_W, D_MODEL)), _full((D_MODEL, D_MODEL)), _full((1, D_MODEL))])
    bf = lambda w: jax.ShapeDtypeStruct((s, w), BF16)
    out_specs = [row, row, row, row, row, half, half, half, zmrow, row, half,
                 _full((1, D_MODEL)), _full((1, SGU_W)), _full((1, SGU_W)), sq, sq]
    out_shape = [bf(D_MODEL), bf(D_MODEL), bf(D_MODEL), bf(D_MODEL), bf(LRU_W), bf(SC_W), bf(SGU_W), bf(SGU_W), bf(ZM_W),
                 jax.ShapeDtypeStruct((s, LRU_W), F32), jax.ShapeDtypeStruct((s, SC_W), F32),
                 jax.ShapeDtypeStruct((1, D_MODEL), F32), jax.ShapeDtypeStruct((1, SGU_W), F32),
                 jax.ShapeDtypeStruct((1, SGU_W), F32), jax.ShapeDtypeStruct((4, CHUNK, CHUNK), F32),
                 jax.ShapeDtypeStruct((4, CHUNK, CHUNK), F32)]
    scratch = [pltpu.VMEM((tm, SGU_W), F32), pltpu.VMEM((tm, SGU_W), F32)]
    args = [dy, za, za, za, za, za, za, za, za, za, za, zm, hf, hr, ya, yb, yc, mx, lru_w_out, sc_conv_w, sc_w_out, ln_g, ln_b,
            w_s, w_s_t, b_s_t, sgu_w_out, w_o, post_g]
    assert len(in_specs) == n_in and len(out_specs) == n_out
    if exchange is not None:
        in_specs.append(HBM)
        out_specs.append(HBM)
        out_shape.append(jax.ShapeDtypeStruct(exchange.shape, exchange.dtype))
        scratch += [pltpu.SemaphoreType.DMA((3,)), pltpu.SemaphoreType.DMA((3,))]
        args.append(exchange)
    outs = list(pl.pallas_call(
        body, name="mix_out_bwd" if exchange is None else "mix_out_bwd_exchange", grid=(nt,),
        in_specs=in_specs, out_specs=out_specs, out_shape=out_shape, scratch_shapes=scratch,
        compiler_params=_cp(),
    )(*args))
    if exchange is not None:
        outs[-1] = _own_block(outs[-1], exchange)
    return outs


def _lru_bwd(za, h_dir, dhs, conv_w, conv_b, wa, ba, wx, bx, lam, rev, tm):
    s = za.shape[0]
    nt = s // tm
    back = not rev

    def body(x_ref, xp_ref, xn_ref, h_ref, hp_ref, hn_ref, dh_ref, cw_ref, cb_ref, wa_ref, ba_ref, wx_ref, bx_ref,
             lam_ref, dxc_ref, dwa_ref, dba_ref, dwx_ref, dbx_ref, dlam_ref,
             xc_s, ra_s, xa_s, a_s, m_s, l_s, hsh_s, c8_s, c_s):
        i = pl.program_id(0)
        first = i == 0
        ti = (nt - 1 - i) if back else i
        rows = lax.broadcasted_iota(jnp.int32, (tm, 1), 0)
        xcb = _lru_preact(x_ref, xp_ref, xn_ref, cw_ref, cb_ref, wa_ref, ba_ref, wx_ref, bx_ref, lam_ref, ti, nt, tm,
                          xc_s, ra_s, xa_s, c8_s)
        hprev, hnext = _edges(hp_ref, hn_ref, ti, nt)
        hsh_s[...] = _shift(h_ref[...].astype(F32), 1 if rev else -1, hprev, hnext, rows)

        def gate_blk(j, carry):
            rws = pl.ds(pl.multiple_of(j * LRU_RB, LRU_RB), LRU_RB)
            r = _sigmoid(ra_s[rws, :])
            a, mult = _lru_decay(c8_s[...], r)
            ra_s[rws, :] = r
            xa_s[rws, :] = _sigmoid(xa_s[rws, :])
            a_s[rws, :] = a
            m_s[rws, :] = mult
            return carry

        lax.fori_loop(0, tm // LRU_RB, gate_blk, 0)

        @pl.when(first)
        def _():
            c_s[...] = jnp.zeros_like(c_s)

        nb = tm // 8

        def blk(j, c):
            jb = (nb - 1 - j) if back else j
            base = pl.multiple_of(jb * 8, 8)
            for q in range(8):
                t = base + ((7 - q) if back else q)
                lt = dh_ref[pl.ds(t, 1), :] + c
                l_s[pl.ds(t, 1), :] = lt
                c = a_s[pl.ds(t, 1), :] * lt
            return c

        c_s[...] = lax.fori_loop(0, nb, blk, c_s[...])

        def grad_blk(j, sums):
            s_lam, s_ba, s_bx = sums
            rws = pl.ds(pl.multiple_of(j * LRU_RB, LRU_RB), LRU_RB)
            du = l_s[rws, :]
            a = a_s[rws, :]
            r = ra_s[rws, :]
            ig = xa_s[rws, :]
            mult = m_s[rws, :]
            xc = xc_s[rws, :]
            t1 = du * mult
            dla = du * hsh_s[rws, :] * a - (du * ig * xc) * (a * a) / mult
            dlr = dla * r
            drp = dlr * c8_s[...] * (1.0 - r)
            dip = (t1 * xc) * ig * (1.0 - ig)
            l_s[rws, :] = t1 * ig
            ra_s[rws, :] = drp
            xa_s[rws, :] = dip
            fold = lambda v: sum(v[8 * q:8 * q + 8] for q in range(1, LRU_RB // 8)) + v[0:8]
            return s_lam + fold(dlr), s_ba + fold(drp), s_bx + fold(dip)

        zero8 = jnp.zeros((8, LRU_W), F32)
        s_lam, s_ba, s_bx = lax.fori_loop(0, tm // LRU_RB, grad_blk, (zero8, zero8, zero8))
        _acc(dlam_ref, jnp.sum(s_lam, axis=0, keepdims=True), first)
        _acc(dba_ref, jnp.sum(s_ba, axis=0, keepdims=True), first)
        _acc(dbx_ref, jnp.sum(s_bx, axis=0, keepdims=True), first)
        drb = ra_s[...].astype(BF16)
        dib = xa_s[...].astype(BF16)
        for h in range(4):
            cs_ = slice(LRU_HD * h, LRU_HD * (h + 1))
            dxc_ref[:, cs_] = l_s[:, cs_] + _dot_nt(drb[:, cs_], wa_ref[h]) + _dot_nt(dib[:, cs_], wx_ref[h])
            _acc(dwa_ref.at[h], _dot_tn(xcb[:, cs_], drb[:, cs_]), first)
            _acc(dwx_ref.at[h], _dot_tn(xcb[:, cs_], dib[:, cs_]), first)

        @pl.when(i == nt - 1)
        def _():
            dlam_ref[...] = dlam_ref[...] * (LRU_C * _sigmoid(-lam_ref[...]))

    vec = _full((1, LRU_W))
    hd = _full((4, LRU_HD, LRU_HD))
    tix = (lambda i: nt - 1 - i) if back else (lambda i: i)
    rowspec = pl.BlockSpec((tm, LRU_W), lambda i: (tix(i), 0))
    return pl.pallas_call(
        body, name="lru_bwd_rev" if rev else "lru_bwd", grid=(nt,),
        in_specs=_halo(tm, 16, LRU_W, 1, nt, back) + _halo(tm, 16, LRU_W, 0, nt, back) + [rowspec, _full((4, LRU_W)), vec, hd, vec, hd, vec, vec],
        out_specs=[rowspec, hd, vec, hd, vec, vec],
        out_shape=[jax.ShapeDtypeStruct((s, LRU_W), F32), jax.ShapeDtypeStruct((4, LRU_HD, LRU_HD), F32),
                   jax.ShapeDtypeStruct((1, LRU_W), F32), jax.ShapeDtypeStruct((4, LRU_HD, LRU_HD), F32),
                   jax.ShapeDtypeStruct((1, LRU_W), F32), jax.ShapeDtypeStruct((1, LRU_W), F32)],
        scratch_shapes=[pltpu.VMEM((tm, LRU_W), F32)] * 7 + [pltpu.VMEM((LRU_RB, LRU_W), F32), pltpu.VMEM((1, LRU_W), F32)],
        compiler_params=_cp(),
    )(za, za, za, h_dir, h_dir, h_dir, dhs, conv_w, conv_b, wa, ba, wx, bx, lam)


def _mix_in_bwd(dy, x, za, dxc_f, dxc_r, dcp, dgate, dscb, dsu, dsv, dzm, pre_g, lru_conv_w, sc_conv_w, w_in, tm):
    s = x.shape[0]
    nt = s // tm
    cw = 512

    def body(dy_ref, x_ref, lx_ref, lxp_ref, lxn_ref, scc_ref, sccp_ref, sccn_ref, scx_ref, scxp_ref, scxn_ref,
             df_ref, dfp_ref, dfn_ref, dr_ref, drp_ref, drn_ref, dcp_ref, dcpp_ref, dcpn_ref,
             dgate_ref, dscb_ref, dsu_ref, dsv_ref, dzm_ref, pg_ref, lcw_ref, scw_ref, w_ref,
             dx_ref, dz_ref, dpg_ref, dlcw_ref, dlcb_ref, dscw_ref):
        ti = pl.program_id(0)
        first = ti == 0
        rows = lax.broadcasted_iota(jnp.int32, (tm, 1), 0)
        fp, fn = _edges(dfp_ref, dfn_ref, ti, nt)
        rp, rn = _edges(drp_ref, drn_ref, ti, nt)
        dxc = df_ref[...] + dr_ref[...]
        dprev, dnext = fp + rp, fn + rn
        dlx = lcw_ref[2:3, :] * dxc
        dlx = dlx + lcw_ref[0:1, :] * _shift(dxc, 2, dprev, dnext, rows)
        dlx = dlx + lcw_ref[1:2, :] * _shift(dxc, 1, dprev, dnext, rows)
        dlx = dlx + lcw_ref[3:4, :] * _shift(dxc, -1, dprev, dnext, rows)
        lprev, lnext = _edges(lxp_ref, lxn_ref, ti, nt)
        lx = lx_ref[...].astype(F32)
        _acc(dlcb_ref, jnp.sum(dxc, axis=0, keepdims=True), first)
        for k, d in enumerate((-2, -1, 0, 1)):
            _acc(dlcw_ref.at[pl.ds(k, 1), :], jnp.sum(dxc * _shift(lx, d, lprev, lnext, rows), axis=0, keepdims=True), first)
        cv, xv, p, pm1, pp1, _ = _sc_conv(scc_ref, sccp_ref, sccn_ref, scx_ref, scxp_ref, scxn_ref, scw_ref, ti, nt, rows)
        cprev, cnext = _edges(dcpp_ref, dcpn_ref, ti, nt)
        dcp_v = dcp_ref[...]
        dp = (scw_ref[1:2, :] * dcp_v + scw_ref[0:1, :] * _shift(dcp_v, 1, cprev, cnext, rows)
              + scw_ref[2:3, :] * _shift(dcp_v, -1, cprev, cnext, rows))
        for k, pk in enumerate((pm1, p, pp1)):
            _acc(dscw_ref.at[pl.ds(k, 1), :], jnp.sum(dcp_v * pk, axis=0, keepdims=True), first)
        dz_ref[:, 0:1024] = dgate_ref[...]
        dz_ref[:, 1024:2048] = dlx.astype(BF16)
        dz_ref[:, 2048:2560] = dscb_ref[...]
        dz_ref[:, 2560:3072] = (dp * xv).astype(BF16)
        dz_ref[:, 3072:3584] = (dp * cv).astype(BF16)
        dz_ref[:, 3584:4096] = dsu_ref[...]
        dz_ref[:, 4096:4608] = dsv_ref[...]
        dz_ref[:, 4608:7680] = dzm_ref[...]
        dh = _dot(dz_ref[...], w_ref[...])
        dxn, dpg = _rms_bwd(dh, x_ref[...], pg_ref[...])
        dx_ref[...] = dy_ref[...] + dxn
        _acc(dpg_ref, dpg, first)

    row = pl.BlockSpec((tm, D_MODEL), lambda i: (i, 0))
    half = pl.BlockSpec((tm, 512), lambda i: (i, 0))
    in_specs = ([row, row] + _halo(tm, 16, LRU_W, 1, nt) + _halo(tm, 16, SC_W, 5, nt) + _halo(tm, 16, SC_W, 6, nt)
                + _halo(tm, 8, LRU_W, 0, nt) + _halo(tm, 8, LRU_W, 0, nt) + _halo(tm, 8, SC_W, 0, nt)
                + [row, half, half, half, pl.BlockSpec((tm, ZM_W), lambda i: (i, 0)),
                   _full((1, D_MODEL)), _full((4, LRU_W)), _full((3, SC_W)), _res((D_IN, D_MODEL))])
    return pl.pallas_call(
        body, name="mix_in_bwd", grid=(nt,),
        in_specs=in_specs,
        out_specs=[row, pl.BlockSpec((tm, D_IN), lambda i: (i, 0)), _full((1, D_MODEL)), _full((4, LRU_W)),
                   _full((1, LRU_W)), _full((3, SC_W))],
        out_shape=[jax.ShapeDtypeStruct((s, D_MODEL), F32), jax.ShapeDtypeStruct((s, D_IN), BF16),
                   jax.ShapeDtypeStruct((1, D_MODEL), F32), jax.ShapeDtypeStruct((4, LRU_W), F32),
                   jax.ShapeDtypeStruct((1, LRU_W), F32), jax.ShapeDtypeStruct((3, SC_W), F32)],
        compiler_params=_cp(),
    )(dy, x, za, za, za, za, za, za, za, za, za, dxc_f, dxc_f, dxc_f, dxc_r, dxc_r, dxc_r, dcp, dcp, dcp,
      dgate, dscb, dsu, dsv, dzm, pre_g, lru_conv_w, sc_conv_w, w_in)


def _loss_head(y, target, tm):
    s = y.shape[0]
    nt = s // tm

    def body(y_ref, t_ref, dy_ref, acc_ref):
        err = y_ref[...] - t_ref[...]
        dy_ref[...] = err * (1.0 / D_MODEL)
        _acc(acc_ref, jnp.sum(err * err, axis=0, keepdims=True), pl.program_id(0) == 0)

    row = pl.BlockSpec((tm, D_MODEL), lambda i: (i, 0))
    return pl.pallas_call(
        body, name="loss_head", grid=(nt,), in_specs=[row, row], out_specs=[row, _full((1, D_MODEL))],
        out_shape=[jax.ShapeDtypeStruct((s, D_MODEL), F32), jax.ShapeDtypeStruct((1, D_MODEL), F32)],
        compiler_params=_cp(),
    )(y, target)


def _row_tile(rows, cols):
    cap = max(8, (2 ** 18) // cols)
    best = None
    for t in range(8, min(rows, cap) + 1, 8):
        if rows % t == 0:
            best = t
    return best if best is not None else rows


def _adamw(w, g, m, v, name):
    shape = w.shape
    cols = shape[-1]
    rows = int(np.prod(shape[:-1]))
    tr = _row_tile(rows, cols)
    bc1 = 1.0 - ADAM_B1 ** ADAM_STEP
    bc2 = 1.0 - ADAM_B2 ** ADAM_STEP

    def body(w_ref, g_ref, m_ref, v_ref, d_ref, nm_ref, nv_ref):
        gv = g_ref[...]
        mn = ADAM_B1 * m_ref[...] + (1.0 - ADAM_B1) * gv
        vn = ADAM_B2 * v_ref[...] + (1.0 - ADAM_B2) * (gv * gv)
        nm_ref[...] = mn
        nv_ref[...] = vn
        d_ref[...] = -ADAM_LR * ((mn / bc1) / (jnp.sqrt(vn / bc2) + ADAM_EPS) + ADAM_WD * w_ref[...])

    spec = pl.BlockSpec((tr, cols), lambda i: (i, 0))
    sds = jax.ShapeDtypeStruct((rows, cols), F32)
    outs = pl.pallas_call(
        body, name="adamw_" + name, grid=(rows // tr,), in_specs=[spec] * 4, out_specs=[spec] * 3,
        out_shape=[sds, sds, sds], compiler_params=_cp(("parallel",)),
    )(w.reshape(rows, cols), g.reshape(rows, cols), m.reshape(rows, cols), v.reshape(rows, cols))
    return [o.reshape(shape) for o in outs]


HBM = pl.BlockSpec(memory_space=pl.ANY)


def _place():
    x, y, c = lax.axis_index("x"), lax.axis_index("y"), lax.axis_index("c")
    chips = [(1 - x, y), (x, 1 - y), (1 - x, 1 - y)]
    return x, y, c, chips


def _all_gather(buf, name):
    r, cdim = buf.shape

    def body(b_ref, o_ref, ssem, rsem):
        _gather_steps(0, 1, b_ref, o_ref, ssem, rsem)

    got = pl.pallas_call(
        body, name=name, in_specs=[HBM], out_specs=HBM,
        out_shape=jax.ShapeDtypeStruct((N_SHARD, r, cdim), buf.dtype),
        scratch_shapes=[pltpu.SemaphoreType.DMA((6,)), pltpu.SemaphoreType.DMA((6,))],
    )(buf)
    return _own_slot(got, buf)


def _own_slot(got, buf):
    me = 2 * lax.axis_index("x") + lax.axis_index("y")
    return lax.dynamic_update_slice(got, buf[None], (me, 0, 0))


def _at_step(i, step):
    if isinstance(i, int):
        return (lambda f: f()) if i == step else (lambda f: None)
    return pl.when(i == step)


def _gather_steps(i, nt, b_ref, o_ref, ssem, rsem):
    r2 = b_ref.shape[0] // 2
    x, y, c, chips = _place()
    me = 2 * x + y
    sib = (x, y, 1 - c)
    mine = pl.ds(pl.multiple_of(c * r2, 16), r2)
    other = pl.ds(pl.multiple_of((1 - c) * r2, 16), r2)

    def rc(k, src, dst, to):
        return pltpu.make_async_remote_copy(src_ref=src, dst_ref=dst, send_sem=ssem.at[k], recv_sem=rsem.at[k],
                                            device_id=to, device_id_type=MESH)

    def sends():
        return [rc(j, b_ref.at[mine], o_ref.at[me, mine], (cx, cy, c)) for j, (cx, cy) in enumerate(chips)]

    def landed(j, rows):
        cx, cy = chips[j]
        return o_ref.at[2 * cx + cy, rows]

    @_at_step(i, 0)
    def _():
        for cp in sends():
            cp.start()

    @_at_step(i, max(nt - 4, 0))
    def _():
        for j, (cx, cy) in enumerate(chips):
            rc(j, landed(j, mine), landed(j, mine), (cx, cy, c)).wait_recv()
            rc(3 + j, landed(j, mine), landed(j, mine), sib).start()

    @_at_step(i, nt - 1)
    def _():
        for j in range(3):
            rc(3 + j, landed(j, other), landed(j, other), sib).wait_recv()
        for cp in sends():
            cp.wait_send()
        for j in range(3):
            rc(3 + j, landed(j, mine), landed(j, mine), sib).wait_send()


def _exchange_steps(i, nt, b_ref, o_ref, ssem, rsem):
    x, y, c, chips = _place()
    me = 2 * x + y

    def sends():
        return [pltpu.make_async_remote_copy(src_ref=b_ref.at[2 * cx + cy], dst_ref=o_ref.at[me], send_sem=ssem.at[j],
                                             recv_sem=rsem.at[j], device_id=(cx, cy, c), device_id_type=MESH)
                for j, (cx, cy) in enumerate(chips)]

    @_at_step(i, 0)
    def _():
        for cp in sends():
            cp.start()

    @_at_step(i, nt - 1)
    def _():
        for j, (cx, cy) in enumerate(chips):
            blk = o_ref.at[2 * cx + cy]
            pltpu.make_async_remote_copy(src_ref=blk, dst_ref=blk, send_sem=ssem.at[j], recv_sem=rsem.at[j],
                                         device_id=(cx, cy, c), device_id_type=MESH).wait_recv()
        for cp in sends():
            cp.wait_send()


def _pair_swap(send, name, sibling_half=False):
    shape = send.shape
    if sibling_half:
        r2 = shape[1] // 2
        shape = (shape[0], r2, shape[2])

    def body(s_ref, r_ref, ssem, rsem):
        x, y, c, _ = _place()
        src = s_ref.at[:, pl.ds(pl.multiple_of((1 - c) * r2, 16), r2), :] if sibling_half else s_ref
        cp = pltpu.make_async_remote_copy(src_ref=src, dst_ref=r_ref, send_sem=ssem, recv_sem=rsem,
                                          device_id=(x, y, 1 - c), device_id_type=MESH)
        cp.start()
        cp.wait()

    return pl.pallas_call(
        body, name=name, in_specs=[HBM], out_specs=HBM, out_shape=jax.ShapeDtypeStruct(shape, send.dtype),
        scratch_shapes=[pltpu.SemaphoreType.DMA, pltpu.SemaphoreType.DMA],
    )(send)


def _chip_exchange(blocks, name):
    def body(b_ref, o_ref, ssem, rsem):
        _exchange_steps(0, 1, b_ref, o_ref, ssem, rsem)

    got = pl.pallas_call(
        body, name=name, in_specs=[HBM], out_specs=HBM, out_shape=jax.ShapeDtypeStruct(blocks.shape, blocks.dtype),
        scratch_shapes=[pltpu.SemaphoreType.DMA((3,)), pltpu.SemaphoreType.DMA((3,))],
    )(blocks)
    return _own_block(got, blocks)


def _own_block(got, blocks):
    me = 2 * lax.axis_index("x") + lax.axis_index("y")
    own = lax.dynamic_slice_in_dim(blocks, me, 1, axis=0)
    return lax.dynamic_update_slice_in_dim(got, own, me, axis=0)


def _pair_gather(half, name):
    c = lax.axis_index("c")
    got = _pair_swap(half, name)
    return jnp.stack([jnp.where(c == 0, half, got), jnp.where(c == 0, got, half)])


def _pair_sum(keep, got):
    n, r2, cdim = keep.shape

    def body(a_ref, b_ref, o_ref):
        o_ref[...] = (a_ref[...].astype(F32) + b_ref[...].astype(F32)).astype(BF16)

    spec = pl.BlockSpec((1, RS_TILE, cdim), lambda k, i: (k, i, 0))
    return pl.pallas_call(
        body, name="rs_pair_sum", grid=(n, r2 // RS_TILE), in_specs=[spec, spec], out_specs=spec,
        out_shape=jax.ShapeDtypeStruct(keep.shape, BF16), compiler_params=_cp(("parallel", "parallel")),
    )(keep, got)


def _chip_sum(slots):
    n, r2, cdim = slots.shape

    def body(s_ref, o_ref):
        o_ref[...] = ((s_ref[0].astype(F32) + s_ref[1].astype(F32)) + s_ref[2].astype(F32)) + s_ref[3].astype(F32)

    return pl.pallas_call(
        body, name="rs_chip_sum", grid=(r2 // RS_TILE,),
        in_specs=[pl.BlockSpec((n, RS_TILE, cdim), lambda i: (0, i, 0))],
        out_specs=pl.BlockSpec((RS_TILE, cdim), lambda i: (i, 0)),
        out_shape=jax.ShapeDtypeStruct((r2, cdim), F32), compiler_params=_cp(("parallel",)),
    )(slots)


def _rs_pair_stage(packed, tag):
    r2 = packed.shape[1] // 2
    c = lax.axis_index("c")
    keep = lax.dynamic_slice_in_dim(packed, c * r2, r2, axis=1)
    return _pair_sum(keep, _pair_swap(packed, "rs_pair_swap" + tag, sibling_half=True))


def _rs_finish(slots, tag):
    both = _pair_gather(_chip_sum(slots), "rs_pair_gather" + tag)
    return both.reshape(2 * slots.shape[1], slots.shape[2])


TRANSPOSED = ("ffn1_w_gate", "ffn1_w_up", "ffn2_w_gate", "ffn2_w_up", "w_in")
BIG = [("ffn1_w_gate", (F_SH, D_MODEL)), ("ffn1_w_up", (F_SH, D_MODEL)), ("ffn1_w_down", (F_SH, D_MODEL)),
       ("ffn2_w_gate", (F_SH, D_MODEL)), ("ffn2_w_up", (F_SH, D_MODEL)), ("ffn2_w_down", (F_SH, D_MODEL)),
       ("w_in", (D_IN_SH, D_MODEL)), ("lru_wa", (2, 4, 64, LRU_HD)), ("lru_wx", (2, 4, 64, LRU_HD)),
       ("lru_w_out", (256, D_MODEL)), ("sc_w_out", (SC_W, 256)), ("sgu_w_out", (SGU_W, 256)), ("w_o", (256, D_MODEL))]
FFN_NAMES = [n for n, _ in BIG[:6]]
TAIL_BIG = BIG[7:]
FFN_BLK0 = 3
TAIL_ROW0 = (FFN_BLK0 + 6) * F_SH
TAIL_ROWS = 1152
SMALL = [("lru_conv_w", (4, 256)), ("lru_ba", (2, 256)), ("lru_bx", (2, 256)), ("lru_lambda", (2, 256)),
         ("sc_conv_w", (3, 128))]
REPL = [("ffn1_pre_g", (D_MODEL,)), ("ffn1_post_g", (D_MODEL,)), ("mix_pre_g", (D_MODEL,)), ("lru_conv_b", (LRU_W,)),
        ("sgu_ln_g", (SGU_W,)), ("sgu_ln_b", (SGU_W,)), ("sgu_w_s", (4, CHUNK, CHUNK)), ("sgu_b", (4, CHUNK)),
        ("mix_post_g", (D_MODEL,)), ("ffn2_pre_g", (D_MODEL,)), ("ffn2_post_g", (D_MODEL,))]
WEIGHTS = ['ffn1_pre_g', 'ffn1_w_gate', 'ffn1_w_up', 'ffn1_w_down', 'ffn1_post_g', 'mix_pre_g', 'w_in', 'lru_conv_w',
           'lru_conv_b', 'lru_wa', 'lru_ba', 'lru_wx', 'lru_bx', 'lru_lambda', 'lru_w_out', 'sc_conv_w', 'sc_w_out',
           'sgu_ln_g', 'sgu_ln_b', 'sgu_w_s', 'sgu_b', 'sgu_w_out', 'w_o', 'mix_post_g', 'ffn2_pre_g', 'ffn2_w_gate',
           'ffn2_w_up', 'ffn2_w_down', 'ffn2_post_g']


def _seg_rows(shape):
    return -(-int(np.prod(shape)) // PACK_C)


def _pack(parts, rows_total, dtype):
    lead = parts[0].shape[0]
    segs = []
    used = 0
    for p in parts:
        flat = p.reshape(lead, -1).astype(dtype)
        nr = -(-flat.shape[1] // PACK_C)
        pad = nr * PACK_C - flat.shape[1]
        if pad:
            flat = jnp.pad(flat, ((0, 0), (0, pad)))
        segs.append(flat.reshape(lead, nr, PACK_C))
        used += nr
    if rows_total > used:
        segs.append(jnp.zeros((lead, rows_total - used, PACK_C), dtype))
    return jnp.concatenate(segs, axis=1)


def _unpack(buf, specs):
    lead = buf.shape[0]
    out = {}
    r0 = 0
    for name, shape in specs:
        nr = _seg_rows(shape)
        n = int(np.prod(shape))
        out[name] = buf[:, r0:r0 + nr].reshape(lead, nr * PACK_C)[:, :n].reshape((lead,) + tuple(shape))
        r0 += nr
    return out


def _full_from_shards(name, t):
    if name in ("ffn1_w_gate", "ffn1_w_up", "ffn1_w_down", "ffn2_w_gate", "ffn2_w_up", "ffn2_w_down"):
        return t
    if name == "w_in":
        return t.reshape(D_IN, D_MODEL)
    if name in ("sc_w_out", "sgu_w_out", "lru_conv_w", "lru_ba", "lru_bx", "lru_lambda", "sc_conv_w"):
        return jnp.moveaxis(t, 0, -2).reshape(t.shape[1:-1] + (N_SHARD * t.shape[-1],))
    if name in ("lru_wa", "lru_wx"):
        return jnp.moveaxis(t, 0, 2).reshape(2, 4, LRU_HD, LRU_HD)
    if name in ("lru_w_out", "w_o"):
        return t.reshape(N_SHARD * t.shape[1], t.shape[2])
    raise ValueError(name)


def _shards_from_full(name, gfull):
    if name in ("ffn1_w_gate", "ffn1_w_up", "ffn1_w_down", "ffn2_w_gate", "ffn2_w_up", "ffn2_w_down"):
        return gfull
    if name == "w_in":
        return gfull.reshape(N_SHARD, D_IN_SH, D_MODEL)
    if name in ("sc_w_out", "sgu_w_out", "lru_conv_w", "lru_ba", "lru_bx", "lru_lambda", "sc_conv_w"):
        lastdim = gfull.shape[-1] // N_SHARD
        return jnp.moveaxis(gfull.reshape(gfull.shape[:-1] + (N_SHARD, lastdim)), -2, 0)
    if name in ("lru_wa", "lru_wx"):
        return jnp.moveaxis(gfull.reshape(2, 4, N_SHARD, 64, LRU_HD), 2, 0)
    if name in ("lru_w_out", "w_o"):
        return gfull.reshape(N_SHARD, gfull.shape[0] // N_SHARD, gfull.shape[1])
    raise ValueError(name)


def _tiles(s):
    return dict(ffn=min(512, s), ffn_bwd=min(512, s), tn=min(2048, s), tn_ffn=min(4096, s), mix_in=min(512, s), lru=min(512, s), mix=min(256, s), loss=min(512, s))


def _mixer_fwd(x, w, t):
    hb, za, zm = _mix_in_fwd(x, w["mix_pre_g"], w["w_in"], t["mix_in"])
    hf = _lru_fwd(za, w["lru_conv_w"], w["lru_conv_b"], w["lru_wa"][0], w["lru_ba"][0:1], w["lru_wx"][0],
                  w["lru_bx"][0:1], w["lru_lambda"][0:1], False, t["lru"])
    hr = _lru_fwd(za, w["lru_conv_w"], w["lru_conv_b"], w["lru_wa"][1], w["lru_ba"][1:2], w["lru_wx"][1],
                  w["lru_bx"][1:2], w["lru_lambda"][1:2], True, t["lru"])
    out, yain, q, ycin, ya, yb, yc, mb, mx = _mix_out_fwd(
        x, za, zm, hf, hr, w["lru_w_out"], w["sc_conv_w"], w["sc_w_out"], w["sgu_ln_g"], w["sgu_ln_b"], w["sgu_w_s"],
        w["sgu_b_t"], w["sgu_w_out"], w["w_o"], w["mix_post_g"], t["mix"])
    return out, dict(x=x, hb=hb, za=za, zm=zm, hf=hf, hr=hr, yain=yain, q=q, ycin=ycin, ya=ya, yb=yb, yc=yc, mb=mb, mx=mx)


def _mixer_bwd(dy, sv, w, t, buf, exchange=None):
    g = {}
    (dmx, dya, dyb, dyc, dgate, dscb, dsu, dsv, dzm, dhs, dcp, g["mix_post_g"], g["sgu_ln_g"], g["sgu_ln_b"],
     g["sgu_w_s"], dbs, *slots) = _mix_out_bwd(
        dy, sv["za"], sv["zm"], sv["hf"], sv["hr"], sv["ya"], sv["yb"], sv["yc"], sv["mx"], w["lru_w_out"], w["sc_conv_w"],
        w["sc_w_out"], w["sgu_ln_g"], w["sgu_ln_b"], w["sgu_w_s"], w["sgu_w_s_t"], w["sgu_b_t"], w["sgu_w_out"], w["w_o"],
        w["mix_post_g"], t["mix"], exchange)
    g["sgu_b"] = jnp.sum(dbs, axis=-1)
    ts = t["tn"]
    g["w_o"] = _tn_plain(sv["mb"], dmx, "dw_o", ts)
    g["lru_w_out"] = _tn_plain(sv["yain"], dya, "dw_lru_out", ts)
    g["sc_w_out"] = _tn_plain(sv["q"], dyb, "dw_sc_out", ts)
    g["sgu_w_out"] = _tn_plain(sv["ycin"], dyc, "dw_sgu_out", ts)
    dxc, dwa, dba, dwx, dbx, dlam = [], [], [], [], [], []
    for d, rev in enumerate((False, True)):
        o = _lru_bwd(sv["za"], sv["hr"] if rev else sv["hf"], dhs, w["lru_conv_w"], w["lru_conv_b"], w["lru_wa"][d],
                     w["lru_ba"][d:d + 1], w["lru_wx"][d], w["lru_bx"][d:d + 1], w["lru_lambda"][d:d + 1], rev, t["lru"])
        for lst, val in zip((dxc, dwa, dba, dwx, dbx, dlam), o):
            lst.append(val)
    g["lru_wa"] = jnp.stack(dwa)
    g["lru_wx"] = jnp.stack(dwx)
    g["lru_ba"] = jnp.concatenate(dba, axis=0)
    g["lru_bx"] = jnp.concatenate(dbx, axis=0)
    g["lru_lambda"] = jnp.concatenate(dlam, axis=0)
    dx, dz, g["mix_pre_g"], g["lru_conv_w"], g["lru_conv_b"], g["sc_conv_w"] = _mix_in_bwd(
        dy, sv["x"], sv["za"], dxc[0], dxc[1], dcp, dgate, dscb, dsu, dsv, dzm, w["mix_pre_g"], w["lru_conv_w"],
        w["sc_conv_w"], w["w_in"], t["mix"])
    buf = _tn(dz, sv["hb"], "dw_in", N_SHARD, (ts, D_IN_SH), lambda b, s_: (s_, b), (ts, D_MODEL), lambda b, s_: (s_, 0),
              None, (None, D_IN_SH, D_MODEL), None, ts, into=(buf, 0))
    return dx, g, buf, (slots[0] if slots else None)


def kernel(x, ffn1_pre_g, ffn1_w_gate, ffn1_w_up, ffn1_w_down, ffn1_post_g, mix_pre_g, w_in, lru_conv_w, lru_conv_b, lru_wa, lru_ba, lru_wx, lru_bx, lru_lambda, lru_w_out, sc_conv_w, sc_w_out, sgu_ln_g, sgu_ln_b, sgu_w_s, sgu_b, sgu_w_out, w_o, mix_post_g, ffn2_pre_g, ffn2_w_gate, ffn2_w_up, ffn2_w_down, ffn2_post_g, loss_target, m_ffn1_pre_g, m_ffn1_w_gate, m_ffn1_w_up, m_ffn1_w_down, m_ffn1_post_g, m_mix_pre_g, m_w_in, m_lru_conv_w, m_lru_conv_b, m_lru_wa, m_lru_ba, m_lru_wx, m_lru_bx, m_lru_lambda, m_lru_w_out, m_sc_conv_w, m_sc_w_out, m_sgu_ln_g, m_sgu_ln_b, m_sgu_w_s, m_sgu_b, m_sgu_w_out, m_w_o, m_mix_post_g, m_ffn2_pre_g, m_ffn2_w_gate, m_ffn2_w_up, m_ffn2_w_down, m_ffn2_post_g, v_ffn1_pre_g, v_ffn1_w_gate, v_ffn1_w_up, v_ffn1_w_down, v_ffn1_post_g, v_mix_pre_g, v_w_in, v_lru_conv_w, v_lru_conv_b, v_lru_wa, v_lru_ba, v_lru_wx, v_lru_bx, v_lru_lambda, v_lru_w_out, v_sc_conv_w, v_sc_w_out, v_sgu_ln_g, v_sgu_ln_b, v_sgu_w_s, v_sgu_b, v_sgu_w_out, v_w_o, v_mix_post_g, v_ffn2_pre_g, v_ffn2_w_gate, v_ffn2_w_up, v_ffn2_w_down, v_ffn2_post_g):
    args = (ffn1_pre_g, ffn1_w_gate, ffn1_w_up, ffn1_w_down, ffn1_post_g, mix_pre_g, w_in, lru_conv_w, lru_conv_b, lru_wa, lru_ba, lru_wx, lru_bx, lru_lambda, lru_w_out, sc_conv_w, sc_w_out, sgu_ln_g, sgu_ln_b, sgu_w_s, sgu_b, sgu_w_out, w_o, mix_post_g, ffn2_pre_g, ffn2_w_gate, ffn2_w_up, ffn2_w_down, ffn2_post_g)
    margs = (m_ffn1_pre_g, m_ffn1_w_gate, m_ffn1_w_up, m_ffn1_w_down, m_ffn1_post_g, m_mix_pre_g, m_w_in, m_lru_conv_w, m_lru_conv_b, m_lru_wa, m_lru_ba, m_lru_wx, m_lru_bx, m_lru_lambda, m_lru_w_out, m_sc_conv_w, m_sc_w_out, m_sgu_ln_g, m_sgu_ln_b, m_sgu_w_s, m_sgu_b, m_sgu_w_out, m_w_o, m_mix_post_g, m_ffn2_pre_g, m_ffn2_w_gate, m_ffn2_w_up, m_ffn2_w_down, m_ffn2_post_g)
    vargs = (v_ffn1_pre_g, v_ffn1_w_gate, v_ffn1_w_up, v_ffn1_w_down, v_ffn1_post_g, v_mix_pre_g, v_w_in, v_lru_conv_w, v_lru_conv_b, v_lru_wa, v_lru_ba, v_lru_wx, v_lru_bx, v_lru_lambda, v_lru_w_out, v_sc_conv_w, v_sc_w_out, v_sgu_ln_g, v_sgu_ln_b, v_sgu_w_s, v_sgu_b, v_sgu_w_out, v_w_o, v_mix_post_g, v_ffn2_pre_g, v_ffn2_w_gate, v_ffn2_w_up, v_ffn2_w_down, v_ffn2_post_g)
    wsh = dict(zip(WEIGHTS, args))
    msh = dict(zip(WEIGHTS, margs))
    vsh = dict(zip(WEIGHTS, vargs))
    xs = x[0]
    s = xs.shape[0]
    t = _tiles(s)

    small = _pack([wsh[n].reshape(1, -1) for n, _ in SMALL], SMALL_ROWS, F32)[0]
    small_all = _all_gather(small, "ag_small")
    small_un = _unpack(small_all, [(n, (DEPTH,) + sh) for n, sh in SMALL])
    packed_w = [_pack([(wsh[n][l].T if n in TRANSPOSED else wsh[n][l])[None] for n, _ in BIG], AG_ROWS, BF16)[0]
                for l in range(DEPTH)]

    def layer_weights(l, gathered):
        un = _unpack(gathered, BIG)
        w = {n: _full_from_shards(n, un[n]) for n, _ in BIG}
        for n, _ in SMALL:
            w[n] = _full_from_shards(n, small_un[n][:, l])
        for n, _ in REPL:
            w[n] = wsh[n][l]
        for n in ("ffn1_pre_g", "ffn1_post_g", "mix_pre_g", "lru_conv_b", "sgu_ln_g", "sgu_ln_b", "mix_post_g", "ffn2_pre_g",
                  "ffn2_post_g"):
            w[n] = w[n].reshape(1, -1)
        w["sgu_w_s_t"] = jnp.swapaxes(w["sgu_w_s"], 1, 2).astype(BF16)
        w["sgu_w_s"] = w["sgu_w_s"].astype(BF16)
        w["sgu_b_t"] = w["sgu_b"].T
        return w

    saved = []
    layers = []
    cur = xs
    gathered = _all_gather(packed_w[0], "ag_weights_l0")
    for l in range(DEPTH):
        w = layer_weights(l, gathered)
        layers.append(w)
        sv = {}
        x1, *rest = _ffn_fwd(cur, w["ffn1_pre_g"], w["ffn1_w_gate"], w["ffn1_w_up"], w["ffn1_w_down"], w["ffn1_post_g"], t["ffn"],
                             gather=packed_w[l + 1] if l + 1 < DEPTH else None)
        if l + 1 < DEPTH:
            gathered = rest.pop()
        sv["ffn1"] = (cur, *rest)
        x2, sv["mix"] = _mixer_fwd(x1, w, t)
        x3, *rest = _ffn_fwd(x2, w["ffn2_pre_g"], w["ffn2_w_gate"], w["ffn2_w_up"], w["ffn2_w_down"], w["ffn2_post_g"], t["ffn"])
        sv["ffn2"] = (x2, *rest)
        saved.append(sv)
        cur = x3

    dy, sq = _loss_head(cur, loss_target[0], t["loss"])
    loss = lax.psum(0.5 * jnp.sum(sq) / D_MODEL, ("x", "y", "c"))

    grads = [None] * DEPTH

    def finish(layer, slots):
        total = _rs_finish(slots, "_l%d" % layer)
        un = _unpack(total[None, TAIL_ROW0:TAIL_ROW0 + TAIL_ROWS], TAIL_BIG + SMALL + REPL)
        un["w_in"] = total[None, 0:D_IN_SH]
        for j, n in enumerate(FFN_NAMES):
            un[n] = total[None, (FFN_BLK0 + j) * F_SH:(FFN_BLK0 + j + 1) * F_SH]
        grads[layer] = {n: (un[n][0].T if n in TRANSPOSED else un[n][0]) for n in WEIGHTS}

    pending = None
    for l in reversed(range(DEPTH)):
        w = layers[l]
        sv = saved[l]
        g = {}
        buf = jnp.zeros((N_SHARD, RS_ROWS, PACK_C), BF16)
        for tag in ("ffn2", "mix", "ffn1"):
            if tag == "mix":
                dy, gm, buf, slots = _mixer_bwd(dy, sv["mix"], w, t, buf, pending)
                if pending is not None:
                    finish(l + 1, slots)
                g.update(gm)
                continue
            xin, ab, d1, d2, ff = sv[tag]
            dy, hb, dgb, dub, dfb, g[tag + "_pre_g"], g[tag + "_post_g"] = _ffn_bwd(
                dy, xin, d1, d2, ff, w[tag + "_pre_g"], w[tag + "_post_g"], w[tag + "_w_gate"], w[tag + "_w_up"],
                w[tag + "_w_down"], t["ffn_bwd"])
            buf = _ffn_wgrads(hb, ab, dgb, dub, dfb, t["tn_ffn"], buf, FFN_BLK0 + (3 if tag == "ffn2" else 0))
        parts = [_shards_from_full(n, g[n]) for n, _ in TAIL_BIG] + [_shards_from_full(n, g[n]) for n, _ in SMALL]
        parts += [jnp.broadcast_to(g[n].reshape((1,) + sh), (N_SHARD,) + sh) for n, sh in REPL]
        buf = lax.dynamic_update_slice(buf, _pack(parts, TAIL_ROWS, BF16), (0, TAIL_ROW0, 0))
        pending = _rs_pair_stage(buf, "_l%d" % l)
    finish(0, _chip_exchange(pending, "rs_chip_exchange_l0"))
    grad_x = dy[None]

    gw, dw, nm, nv = [], [], [], []
    for n in WEIGHTS:
        gfull = jnp.stack([grads[l][n] for l in range(DEPTH)])
        d_, m_, v_ = _adamw(wsh[n], gfull, msh[n], vsh[n], n)
        gw.append(gfull)
        dw.append(d_)
        nm.append(m_)
        nv.append(v_)
    return (loss, grad_x, *gw, *dw, *nm, *nv)
```

```python
import jax
import jax.numpy as jnp
import numpy as np
from jax import lax
from jax.experimental import pallas as pl
from jax.experimental.pallas import tpu as pltpu

F32 = jnp.float32
BF16 = jnp.bfloat16
MESH = pl.DeviceIdType.MESH

D_MODEL = 1024
D_FF = 2816
N_SHARD = 4
F_SH = D_FF // N_SHARD
D_IN = 7680
D_IN_SH = D_IN // N_SHARD
ZA_W = 4608
ZM_W = 3072
LRU_W = 1024
LRU_HD = 256
SC_W = 512
SGU_W = 512
CHUNK = 128
DEPTH = 4
EPS = 1e-6
LRU_C = 8.0
LRU_RB = 16

ADAM_LR, ADAM_B1, ADAM_B2, ADAM_EPS, ADAM_WD, ADAM_STEP = 0.001, 0.9, 0.999, 1e-08, 0.01, 10

VMEM_LIMIT = 56 * 2 ** 20
PACK_C = 1024
AG_ROWS = 7168
RS_ROWS = 7680
RS_TILE = 192
SMALL_ROWS = 32


def _cp(sem=("arbitrary",)):
    return pltpu.CompilerParams(dimension_semantics=sem, vmem_limit_bytes=VMEM_LIMIT)


def _full(shape):
    return pl.BlockSpec(shape, lambda *_: (0,) * len(shape))


def _res(shape):
    return pl.BlockSpec(shape, lambda *_: (0,) * len(shape), pipeline_mode=pl.Buffered(1))


def _dot(a, b):
    return jnp.dot(a, b, preferred_element_type=F32)


def _dot_nt(a, b):
    return lax.dot_general(a, b, (((1,), (1,)), ((), ())), preferred_element_type=F32)


def _dot_tn(a, b):
    return lax.dot_general(a, b, (((0,), (0,)), ((), ())), preferred_element_type=F32)


def _sigmoid(x):
    return 0.5 * jnp.tanh(0.5 * x) + 0.5


_GELU_K = 0.7978845608028654
_GELU_C = 0.044715


def _gelu(x):
    return 0.5 * x * (1.0 + jnp.tanh(_GELU_K * (x + _GELU_C * x * x * x)))


def _gelu_and_grad(x):
    t = jnp.tanh(_GELU_K * (x + _GELU_C * x * x * x))
    g = 0.5 * x * (1.0 + t)
    dg = 0.5 * (1.0 + t) + 0.5 * x * (1.0 - t * t) * (_GELU_K * (1.0 + 3.0 * _GELU_C * x * x))
    return g, dg


def _rms_fwd(x, g):
    rs = lax.rsqrt(jnp.mean(x * x, axis=-1, keepdims=True) + EPS)
    return x * rs * g


def _rms_bwd(dy, x, g):
    rs = lax.rsqrt(jnp.mean(x * x, axis=-1, keepdims=True) + EPS)
    n = x * rs
    dn = dy * g
    dx = rs * (dn - n * jnp.mean(dn * n, axis=-1, keepdims=True))
    return dx, jnp.sum(dy * n, axis=0, keepdims=True)


def _acc(ref, val, first):
    @pl.when(first)
    def _():
        ref[...] = val

    @pl.when(jnp.logical_not(first))
    def _():
        ref[...] += val


def _shift(xm, d, prev, nxt, rows):
    tm = xm.shape[0]
    if d == 0:
        return xm
    y = pltpu.roll(xm, (-d) % tm, 0)
    rows8 = rows[0:8]
    if d < 0:
        hb = prev.shape[0]
        top = y[0:8]
        for r in range(-d):
            top = jnp.where(rows8 == r, prev[hb + r + d:hb + r + d + 1, :], top)
        return jnp.concatenate([top, y[8:]], axis=0)
    bot = y[tm - 8:]
    for r in range(d):
        bot = jnp.where(rows8 == 8 - d + r, nxt[r:r + 1, :], bot)
    return jnp.concatenate([y[:tm - 8], bot], axis=0)


def _halo(tm, hb, w, col, nt, rev=False):
    r = tm // hb
    last = nt * r - 1
    ti = (lambda i: nt - 1 - i) if rev else (lambda i: i)
    return [pl.BlockSpec((tm, w), lambda i: (ti(i), col)),
            pl.BlockSpec((hb, w), lambda i: (jnp.maximum(ti(i) * r - 1, 0), col)),
            pl.BlockSpec((hb, w), lambda i: (jnp.minimum((ti(i) + 1) * r, last), col))]


def _edges(prev_ref, next_ref, ti, nt):
    prev = jnp.where(ti > 0, prev_ref[...].astype(F32), 0.0)
    nxt = jnp.where(ti < nt - 1, next_ref[...].astype(F32), 0.0)
    return prev, nxt


def _ffn_fwd(x, pre_g, wg, wu, wd, post_g, tm, gather=None):
    s = x.shape[0]
    nt = s // tm

    def body(x_ref, pg_ref, wg_ref, wu_ref, wd_ref, qg_ref, *rest):
        if gather is None:
            o_ref, a_ref, d1_ref, d2_ref, f_ref = rest
        else:
            pk_ref, o_ref, a_ref, d1_ref, d2_ref, f_ref, gat_ref, ssem, rsem = rest
            _gather_steps(pl.program_id(0), nt, pk_ref, gat_ref, ssem, rsem)
        xv = x_ref[...]
        hb = _rms_fwd(xv, pg_ref[...]).astype(BF16)
        f = jnp.zeros((tm, D_MODEL), F32)
        for k in range(N_SHARD):
            g = _dot_nt(hb, wg_ref[k])
            u = _dot_nt(hb, wu_ref[k])
            sg = _sigmoid(g)
            silu = g * sg
            ab = (silu * u).astype(BF16)
            a_ref[k] = ab
            d1_ref[k] = (u * (sg * (1.0 + g * (1.0 - sg)))).astype(BF16)
            d2_ref[k] = silu.astype(BF16)
            f = f + _dot(ab, wd_ref[k])
        f_ref[...] = f.astype(BF16)
        o_ref[...] = xv + 0.5 * _rms_fwd(f, qg_ref[...])

    row = pl.BlockSpec((tm, D_MODEL), lambda i: (i, 0))
    gu = pl.BlockSpec((N_SHARD, tm, F_SH), lambda i: (0, i, 0))
    in_specs = [row, _full((1, D_MODEL)), _res((N_SHARD, F_SH, D_MODEL)), _res((N_SHARD, F_SH, D_MODEL)),
                _res((N_SHARD, F_SH, D_MODEL)), _full((1, D_MODEL))]
    out_specs = [row, gu, gu, gu, row]
    out_shape = [jax.ShapeDtypeStruct((s, D_MODEL), F32), jax.ShapeDtypeStruct((N_SHARD, s, F_SH), BF16),
                 jax.ShapeDtypeStruct((N_SHARD, s, F_SH), BF16), jax.ShapeDtypeStruct((N_SHARD, s, F_SH), BF16),
                 jax.ShapeDtypeStruct((s, D_MODEL), BF16)]
    args = [x, pre_g, wg, wu, wd, post_g]
    scratch = []
    if gather is not None:
        in_specs.append(HBM)
        out_specs.append(HBM)
        out_shape.append(jax.ShapeDtypeStruct((N_SHARD,) + gather.shape, gather.dtype))
        args.append(gather)
        scratch = [pltpu.SemaphoreType.DMA((6,)), pltpu.SemaphoreType.DMA((6,))]
    outs = list(pl.pallas_call(
        body, name="ffn_fwd" if gather is None else "ffn_fwd_gather", grid=(nt,),
        in_specs=in_specs, out_specs=out_specs, out_shape=out_shape, scratch_shapes=scratch,
        compiler_params=_cp(("parallel",) if gather is None else ("arbitrary",)),
    )(*args))
    if gather is not None:
        outs[-1] = _own_slot(outs[-1], gather)
    return outs


def _ffn_bwd(dy, x, d1, d2, f, pre_g, post_g, wg, wu, wd, tm, swap=None):
    s = x.shape[0]
    nt = s // tm

    def body_a(dy_ref, d1_ref, d2_ref, f_ref, qg_ref, wd_ref, dg_ref, du_ref, df_ref, dqg_ref):
        df, dq = _rms_bwd(0.5 * dy_ref[...], f_ref[...].astype(F32), qg_ref[...])
        dfb = df.astype(BF16)
        df_ref[...] = dfb
        for k in range(N_SHARD):
            da = _dot_nt(dfb, wd_ref[k])
            dg_ref[k] = (da * d1_ref[k].astype(F32)).astype(BF16)
            du_ref[k] = (da * d2_ref[k].astype(F32)).astype(BF16)
        _acc(dqg_ref, dq, pl.program_id(0) == 0)

    def body_b(dy_ref, x_ref, dg_ref, du_ref, pg_ref, wg_ref, wu_ref, *rest):
        if swap is None:
            dx_ref, h_ref, dpg_ref = rest
        else:
            sw_ref, dx_ref, h_ref, dpg_ref, got_ref, ssem, rsem = rest
            _swap_steps(pl.program_id(0), nt, sw_ref, got_ref, ssem, rsem)
        xv = x_ref[...]
        h_ref[...] = _rms_fwd(xv, pg_ref[...]).astype(BF16)
        dh = jnp.zeros((tm, D_MODEL), F32)
        for k in range(N_SHARD):
            dh = dh + _dot(dg_ref[k], wg_ref[k]) + _dot(du_ref[k], wu_ref[k])
        dxn, dp = _rms_bwd(dh, xv, pg_ref[...])
        dx_ref[...] = dy_ref[...] + dxn
        _acc(dpg_ref, dp, pl.program_id(0) == 0)

    row = pl.BlockSpec((tm, D_MODEL), lambda i: (i, 0))
    gu = pl.BlockSpec((N_SHARD, tm, F_SH), lambda i: (0, i, 0))
    vec = _full((1, D_MODEL))
    wsp = _res((N_SHARD, F_SH, D_MODEL))
    big = jax.ShapeDtypeStruct((N_SHARD, s, F_SH), BF16)
    tok_bf = jax.ShapeDtypeStruct((s, D_MODEL), BF16)
    vec_f = jax.ShapeDtypeStruct((1, D_MODEL), F32)
    dg, du, df, dqg = pl.pallas_call(
        body_a, name="ffn_bwd_a", grid=(nt,),
        in_specs=[row, gu, gu, row, vec, wsp],
        out_specs=[gu, gu, row, vec],
        out_shape=[big, big, tok_bf, vec_f],
        compiler_params=_cp(),
    )(dy, d1, d2, f, post_g, wd)
    in_specs = [row, row, gu, gu, vec, wsp, wsp]
    out_specs = [row, row, vec]
    out_shape = [jax.ShapeDtypeStruct((s, D_MODEL), F32), tok_bf, vec_f]
    args = [dy, x, dg, du, pre_g, wg, wu]
    scratch = []
    if swap is not None:
        in_specs.append(HBM)
        out_specs.append(HBM)
        out_shape.append(jax.ShapeDtypeStruct((swap.shape[0], swap.shape[1] // 2, swap.shape[2]), swap.dtype))
        args.append(swap)
        scratch = [pltpu.SemaphoreType.DMA, pltpu.SemaphoreType.DMA]
    dx, h, dpg, *got = pl.pallas_call(
        body_b, name="ffn_bwd_b" if swap is None else "ffn_bwd_b_swap", grid=(nt,),
        in_specs=in_specs, out_specs=out_specs, out_shape=out_shape, scratch_shapes=scratch,
        compiler_params=_cp(),
    )(*args)
    return (dx, h, dg, du, df, dpg, dqg, *got)


def _tn(lhs, rhs, name, nb, lhs_blk, lhs_map, rhs_blk, rhs_map, out_shape, out_blk, out_map, ts, into=None):
    s = lhs.shape[-2]
    ns = s // ts
    acc_shape = tuple(d for d in out_blk if d is not None)

    def body(l_ref, r_ref, *rest):
        o_ref, acc_s = rest[-2:]
        t = pl.program_id(1)
        _acc(acc_s, _dot_tn(l_ref[...], r_ref[...]), t == 0)

        @pl.when(t == ns - 1)
        def _():
            o_ref[...] = acc_s[...].astype(BF16)

    in_specs = [pl.BlockSpec(lhs_blk, lhs_map), pl.BlockSpec(rhs_blk, rhs_map)]
    args = (lhs, rhs)
    aliases = {}
    if into is not None:
        buf, blk_row = into
        in_specs.append(HBM)
        args = (lhs, rhs, buf)
        aliases = {2: 0}
        out_shape = buf.shape
        out_map = lambda b, t: (b, blk_row, 0)
    return pl.pallas_call(
        body, name=name, grid=(nb, ns),
        in_specs=in_specs,
        out_specs=pl.BlockSpec(out_blk, out_map),
        out_shape=jax.ShapeDtypeStruct(out_shape, BF16),
        scratch_shapes=[pltpu.VMEM(acc_shape, F32)],
        input_output_aliases=aliases,
        compiler_params=_cp(("parallel", "arbitrary")),
    )(*args)


def _tn_plain(lhs, rhs, name, ts):
    m, n = lhs.shape[1], rhs.shape[1]
    return _tn(lhs, rhs, name, 1, (ts, m), lambda b, t: (t, 0), (ts, n), lambda b, t: (t, 0),
               (m, n), (m, n), lambda b, t: (0, 0), ts)


def _ffn_wgrads(h, a, dg, du, df, ts, buf, blk0):
    lm2 = lambda b, t: (t, 0)
    bm3 = lambda b, t: (b, t, 0)
    for j, (lhs, rhs, name) in enumerate(((dg, h, "ffn_dwg"), (du, h, "ffn_dwu"), (a, df, "ffn_dwd"))):
        buf = _tn(lhs, rhs, name, N_SHARD, (None, ts, F_SH), bm3, (ts, D_MODEL), lm2, None, (None, F_SH, D_MODEL), None, ts,
                  into=(buf, blk0 + j))
    return buf


def _mix_in_fwd(x, pre_g, w_in, tm):
    s = x.shape[0]
    nt = s // tm
    cw = 512

    def body(x_ref, pg_ref, w_ref, h_ref, za_ref, zm_ref):
        hb = _rms_fwd(x_ref[...], pg_ref[...]).astype(BF16)
        h_ref[...] = hb
        for j in range(ZA_W // cw):
            za_ref[:, j * cw:(j + 1) * cw] = _dot_nt(hb, w_ref[j * cw:(j + 1) * cw, :]).astype(BF16)
        for j in range(ZM_W // cw):
            zm_ref[:, j * cw:(j + 1) * cw] = _dot_nt(hb, w_ref[ZA_W + j * cw:ZA_W + (j + 1) * cw, :]).astype(BF16)

    row = pl.BlockSpec((tm, D_MODEL), lambda i: (i, 0))
    return pl.pallas_call(
        body, name="mix_in_fwd", grid=(nt,),
        in_specs=[row, _full((1, D_MODEL)), _res((D_IN, D_MODEL))],
        out_specs=[row, pl.BlockSpec((tm, ZA_W), lambda i: (i, 0)), pl.BlockSpec((tm, ZM_W), lambda i: (i, 0))],
        out_shape=[jax.ShapeDtypeStruct((s, D_MODEL), BF16), jax.ShapeDtypeStruct((s, ZA_W), BF16),
                   jax.ShapeDtypeStruct((s, ZM_W), BF16)],
        compiler_params=_cp(("parallel",)),
    )(x, pre_g, w_in)


def _lru_conv(xm, prev, nxt, rows, cw_ref, cb_ref):
    acc = cb_ref[...] + cw_ref[2:3, :] * xm
    acc = acc + cw_ref[0:1, :] * _shift(xm, -2, prev, nxt, rows)
    acc = acc + cw_ref[1:2, :] * _shift(xm, -1, prev, nxt, rows)
    acc = acc + cw_ref[3:4, :] * _shift(xm, 1, prev, nxt, rows)
    return acc


def _lru_preact(x_ref, xp_ref, xn_ref, cw_ref, cb_ref, wa_ref, ba_ref, wx_ref, bx_ref, lam_ref, ti, nt, tm,
                xc_s, ra_s, xa_s, c8_s):
    rows = lax.broadcasted_iota(jnp.int32, (tm, 1), 0)
    prev, nxt = _edges(xp_ref, xn_ref, ti, nt)
    xc = _lru_conv(x_ref[...].astype(F32), prev, nxt, rows, cw_ref, cb_ref)
    xc_s[...] = xc
    xcb = xc.astype(BF16)
    for h in range(4):
        cs_ = slice(LRU_HD * h, LRU_HD * (h + 1))
        ra_s[:, cs_] = _dot(xcb[:, cs_], wa_ref[h]) + ba_ref[:, cs_]
        xa_s[:, cs_] = _dot(xcb[:, cs_], wx_ref[h]) + bx_ref[:, cs_]
    lam = lam_ref[...]
    e = jnp.exp(-jnp.abs(lam))
    log1p_e = jnp.where(e < 1e-2, e * (1.0 - e * (0.5 - e * (1.0 / 3.0))), jnp.log(1.0 + e))
    c8_s[...] = jnp.broadcast_to(-LRU_C * (jnp.maximum(-lam, 0.0) + log1p_e), c8_s.shape)
    return xcb


def _lru_decay(cl, r):
    la = cl * r
    a = jnp.exp(la)
    y2 = 2.0 * la
    em = jnp.where(y2 > -0.004, y2 * (-1.0 + y2 * (-0.5 - y2 * (1.0 / 6.0))), 1.0 - a * a)
    return a, jnp.sqrt(em)


def _scan_rows(a_s, b_s, o_s, carry, tm, descending):
    nb = tm // 8

    def blk(j, h):
        jb = (nb - 1 - j) if descending else j
        base = pl.multiple_of(jb * 8, 8)
        for r in range(8):
            t = base + ((7 - r) if descending else r)
            h = a_s[pl.ds(t, 1), :] * h + b_s[pl.ds(t, 1), :]
            o_s[pl.ds(t, 1), :] = h
        return h

    return lax.fori_loop(0, nb, blk, carry)


def _lru_fwd(za, conv_w, conv_b, wa, ba, wx, bx, lam, rev, tm):
    s = za.shape[0]
    nt = s // tm

    def body(x_ref, xp_ref, xn_ref, cw_ref, cb_ref, wa_ref, ba_ref, wx_ref, bx_ref, lam_ref, h_ref,
             xc_s, ra_s, xa_s, o_s, c8_s, c_s):
        i = pl.program_id(0)
        ti = (nt - 1 - i) if rev else i
        _lru_preact(x_ref, xp_ref, xn_ref, cw_ref, cb_ref, wa_ref, ba_ref, wx_ref, bx_ref, lam_ref, ti, nt, tm,
                    xc_s, ra_s, xa_s, c8_s)

        def gate_blk(j, carry):
            rws = pl.ds(pl.multiple_of(j * LRU_RB, LRU_RB), LRU_RB)
            r = _sigmoid(ra_s[rws, :])
            ig = _sigmoid(xa_s[rws, :])
            a, mult = _lru_decay(c8_s[...], r)
            ra_s[rws, :] = a
            xa_s[rws, :] = ig * xc_s[rws, :] * mult
            return carry

        lax.fori_loop(0, tm // LRU_RB, gate_blk, 0)

        @pl.when(i == 0)
        def _():
            c_s[...] = jnp.zeros_like(c_s)

        c_s[...] = _scan_rows(ra_s, xa_s, o_s, c_s[...], tm, rev)
        h_ref[...] = o_s[...].astype(BF16)

    vec = _full((1, LRU_W))
    hd = _full((4, LRU_HD, LRU_HD))
    ti = (lambda i: nt - 1 - i) if rev else (lambda i: i)
    tile = pltpu.VMEM((tm, LRU_W), F32)
    return pl.pallas_call(
        body, name="lru_fwd_rev" if rev else "lru_fwd", grid=(nt,),
        in_specs=_halo(tm, 16, LRU_W, 1, nt, rev) + [_full((4, LRU_W)), vec, hd, vec, hd, vec, vec],
        out_specs=pl.BlockSpec((tm, LRU_W), lambda i: (ti(i), 0)),
        out_shape=jax.ShapeDtypeStruct((s, LRU_W), BF16),
        scratch_shapes=[tile, tile, tile, tile, pltpu.VMEM((LRU_RB, LRU_W), F32), pltpu.VMEM((1, LRU_W), F32)],
        compiler_params=_cp(),
    )(za, za, za, conv_w, conv_b, wa, ba, wx, bx, lam)


def _sc_conv(c_ref, cp_ref, cn_ref, x_ref, xp_ref, xn_ref, w_ref, ti, nt, rows):
    cprev, cnext = _edges(cp_ref, cn_ref, ti, nt)
    xprev, xnext = _edges(xp_ref, xn_ref, ti, nt)
    cv = c_ref[...].astype(F32)
    xv = x_ref[...].astype(F32)
    p = cv * xv
    pm1 = _shift(p, -1, cprev * xprev, cnext * xnext, rows)
    pp1 = _shift(p, 1, cprev * xprev, cnext * xnext, rows)
    conv = w_ref[0:1, :] * pm1 + w_ref[1:2, :] * p + w_ref[2:3, :] * pp1
    return cv, xv, p, pm1, pp1, conv


def _sgu_norm(vz, g_ref, b_ref):
    vg, dvg = _gelu_and_grad(vz)
    mu = jnp.mean(vg, axis=-1, keepdims=True)
    cen = vg - mu
    rs = lax.rsqrt(jnp.mean(cen * cen, axis=-1, keepdims=True) + EPS)
    vn = cen * rs
    return dvg, vn, rs, vn * g_ref[...] + b_ref[...]


def _mix_out_fwd(x, za, zm, hf, hr, lru_w_out, sc_conv_w, sc_w_out, ln_g, ln_b, w_s, b_s_t, sgu_w_out, w_o, post_g, tm):
    s = x.shape[0]
    nt = s // tm
    nc = tm // CHUNK

    def body(x_ref, gate_ref, scb_ref, scc_ref, sccp_ref, sccn_ref, scx_ref, scxp_ref, scxn_ref, su_ref, sv_ref,
             zm_ref, hf_ref, hr_ref, wlo_ref, scw_ref, wso_ref, lg_ref, lb_ref, ws_ref, bs_ref, wgo_ref, wo_ref,
             qg_ref, o_ref, yain_ref, q_ref, ycin_ref, ya_ref, yb_ref, yc_ref, m_ref, mx_ref, mixed_s):
        ti = pl.program_id(0)
        rows = lax.broadcasted_iota(jnp.int32, (tm, 1), 0)
        hs = hf_ref[...].astype(F32) + hr_ref[...].astype(F32)
        yain = (hs * _gelu(gate_ref[...].astype(F32))).astype(BF16)
        yain_ref[...] = yain
        ya = _dot(yain, wlo_ref[...])
        _, _, _, _, _, conv = _sc_conv(scc_ref, sccp_ref, sccn_ref, scx_ref, scxp_ref, scxn_ref, scw_ref, ti, nt, rows)
        qb = (scb_ref[...].astype(F32) * conv).astype(BF16)
        q_ref[...] = qb
        yb = _dot(qb, wso_ref[...])
        _, _, _, v = _sgu_norm(sv_ref[...].astype(F32), lg_ref, lb_ref)
        vb = v.astype(BF16)
        for n in range(nc):
            for g in range(4):
                blk = vb[n * CHUNK:(n + 1) * CHUNK, g * CHUNK:(g + 1) * CHUNK]
                mixed_s[n * CHUNK:(n + 1) * CHUNK, g * CHUNK:(g + 1) * CHUNK] = _dot(ws_ref[g], blk) + bs_ref[:, g:g + 1]
        ycin = (_gelu(su_ref[...].astype(F32)) * mixed_s[...]).astype(BF16)
        ycin_ref[...] = ycin
        yc = _dot(ycin, wgo_ref[...])
        m = (_sigmoid(zm_ref[:, 0:D_MODEL].astype(F32)) * ya + _sigmoid(zm_ref[:, D_MODEL:2 * D_MODEL].astype(F32)) * yb
             + _sigmoid(zm_ref[:, 2 * D_MODEL:3 * D_MODEL].astype(F32)) * yc)
        mb = m.astype(BF16)
        mx = _dot(mb, wo_ref[...])
        ya_ref[...] = ya.astype(BF16)
        yb_ref[...] = yb.astype(BF16)
        yc_ref[...] = yc.astype(BF16)
        m_ref[...] = mb
        mx_ref[...] = mx.astype(BF16)
        o_ref[...] = x_ref[...] + _rms_fwd(mx, qg_ref[...])

    row = pl.BlockSpec((tm, D_MODEL), lambda i: (i, 0))
    half = pl.BlockSpec((tm, 512), lambda i: (i, 0))
    col = lambda c: pl.BlockSpec((tm, 512), lambda i: (i, c))
    in_specs = ([row, pl.BlockSpec((tm, LRU_W), lambda i: (i, 0)), col(4)] + _halo(tm, 16, SC_W, 5, nt) + _halo(tm, 16, SC_W, 6, nt)
                + [col(7), col(8), pl.BlockSpec((tm, ZM_W), lambda i: (i, 0)), row, row,
                   _full((LRU_W, D_MODEL)), _full((3, SC_W)), _full((SC_W, D_MODEL)), _full((1, SGU_W)), _full((1, SGU_W)),
                   _full((4, CHUNK, CHUNK)), _full((CHUNK, 4)), _full((SGU_W, D_MODEL)), _full((D_MODEL, D_MODEL)),
                   _full((1, D_MODEL))])
    bf = lambda w: jax.ShapeDtypeStruct((s, w), BF16)
    return pl.pallas_call(
        body, name="mix_out_fwd", grid=(nt,),
        in_specs=in_specs,
        out_specs=[row, row, half, half, row, row, row, row, row],
        out_shape=[jax.ShapeDtypeStruct((s, D_MODEL), F32), bf(LRU_W), bf(SC_W), bf(SGU_W), bf(D_MODEL), bf(D_MODEL),
                   bf(D_MODEL), bf(D_MODEL), bf(D_MODEL)],
        scratch_shapes=[pltpu.VMEM((tm, SGU_W), F32)],
        compiler_params=_cp(("parallel",)),
    )(x, za, za, za, za, za, za, za, za, za, za, zm, hf, hr, lru_w_out, sc_conv_w, sc_w_out, ln_g, ln_b, w_s, b_s_t,
      sgu_w_out, w_o, post_g)


def _mix_out_bwd(dy, za, zm, hf, hr, ya, yb, yc, mx, lru_w_out, sc_conv_w, sc_w_out, ln_g, ln_b, w_s, w_s_t, b_s_t,
                 sgu_w_out, w_o, post_g, tm, exchange=None):
    s = dy.shape[0]
    nt = s // tm
    nc = tm // CHUNK

    n_in, n_out = 29, 16

    def body(*refs):
        if exchange is not None:
            _exchange_steps(pl.program_id(0), nt, refs[n_in], refs[n_in + 1 + n_out], refs[-2], refs[-1])
            refs = refs[:n_in] + refs[n_in + 1:n_in + 1 + n_out] + refs[n_in + 2 + n_out:-2]
        compute(*refs)

    def compute(dy_ref, gate_ref, scb_ref, scc_ref, sccp_ref, sccn_ref, scx_ref, scxp_ref, scxn_ref, su_ref, sv_ref,
                zm_ref, hf_ref, hr_ref, ya_ref, yb_ref, yc_ref, mx_ref, wlo_ref, scw_ref, wso_ref, lg_ref, lb_ref,
                ws_ref, wst_ref, bs_ref, wgo_ref, wo_ref, qg_ref,
                dmx_ref, dya_ref, dyb_ref, dyc_ref, dgate_ref, dscb_ref, dsu_ref, dsv_ref, dzm_ref, dhs_ref, dcp_ref,
                dqg_ref, dlg_ref, dlb_ref, dws_ref, dbs_ref, mixed_s, dv_s):
        ti = pl.program_id(0)
        first = ti == 0
        rows = lax.broadcasted_iota(jnp.int32, (tm, 1), 0)
        dmx, dq = _rms_bwd(dy_ref[...], mx_ref[...].astype(F32), qg_ref[...])
        _acc(dqg_ref, dq, first)
        dmxb = dmx.astype(BF16)
        dmx_ref[...] = dmxb
        dm = _dot_nt(dmxb, wo_ref[...])
        dys = []
        for k, (y_ref, d_ref) in enumerate(((ya_ref, dya_ref), (yb_ref, dyb_ref), (yc_ref, dyc_ref))):
            gk = _sigmoid(zm_ref[:, k * D_MODEL:(k + 1) * D_MODEL].astype(F32))
            dyk = (dm * gk).astype(BF16)
            d_ref[...] = dyk
            dys.append(dyk)
            dzm_ref[:, k * D_MODEL:(k + 1) * D_MODEL] = (dm * y_ref[...].astype(F32) * gk * (1.0 - gk)).astype(BF16)
        dyain = _dot_nt(dys[0], wlo_ref[...])
        gg, dgg = _gelu_and_grad(gate_ref[...].astype(F32))
        hs = hf_ref[...].astype(F32) + hr_ref[...].astype(F32)
        dhs_ref[...] = dyain * gg
        dgate_ref[...] = (dyain * hs * dgg).astype(BF16)
        dq_b = _dot_nt(dys[1], wso_ref[...])
        _, _, _, _, _, conv = _sc_conv(scc_ref, sccp_ref, sccn_ref, scx_ref, scxp_ref, scxn_ref, scw_ref, ti, nt, rows)
        dscb_ref[...] = (dq_b * conv).astype(BF16)
        dcp_ref[...] = dq_b * scb_ref[...].astype(F32)
        dycin = _dot_nt(dys[2], wgo_ref[...])
        dvg, vn, rs, v = _sgu_norm(sv_ref[...].astype(F32), lg_ref, lb_ref)
        vb = v.astype(BF16)
        ug, dug = _gelu_and_grad(su_ref[...].astype(F32))
        dmixed = dycin * ug
        dmb = dmixed.astype(BF16)
        dws = [jnp.zeros((CHUNK, CHUNK), F32) for _ in range(4)]
        dbs = [jnp.zeros((CHUNK, CHUNK), F32) for _ in range(4)]
        for n in range(nc):
            for g in range(4):
                rs_, cs_ = slice(n * CHUNK, (n + 1) * CHUNK), slice(g * CHUNK, (g + 1) * CHUNK)
                mixed_s[rs_, cs_] = _dot(ws_ref[g], vb[rs_, cs_]) + bs_ref[:, g:g + 1]
                dv_s[rs_, cs_] = _dot(wst_ref[g], dmb[rs_, cs_])
                dws[g] = dws[g] + _dot_nt(dmb[rs_, cs_], vb[rs_, cs_])
                dbs[g] = dbs[g] + dmixed[rs_, cs_]
        for g in range(4):
            _acc(dws_ref.at[g], dws[g], first)
            _acc(dbs_ref.at[g], dbs[g], first)
        dsu_ref[...] = (dycin * mixed_s[...] * dug).astype(BF16)
        dv = dv_s[...]
        _acc(dlg_ref, jnp.sum(dv * vn, axis=0, keepdims=True), first)
        _acc(dlb_ref, jnp.sum(dv, axis=0, keepdims=True), first)
        dvn = dv * lg_ref[...]
        dcen = rs * (dvn - jnp.mean(dvn, axis=-1, keepdims=True) - vn * jnp.mean(dvn * vn, axis=-1, keepdims=True))
        dsv_ref[...] = (dcen * dvg).astype(BF16)

    row = pl.BlockSpec((tm, D_MODEL), lambda i: (i, 0))
    half = pl.BlockSpec((tm, 512), lambda i: (i, 0))
    col = lambda c: pl.BlockSpec((tm, 512), lambda i: (i, c))
    zmrow = pl.BlockSpec((tm, ZM_W), lambda i: (i, 0))
    sq = _full((4, CHUNK, CHUNK))
    in_specs = ([row, pl.BlockSpec((tm, LRU_W), lambda i: (i, 0)), col(4)] + _halo(tm, 16, SC_W, 5, nt) + _halo(tm, 16, SC_W, 6, nt)
                + [col(7), col(8), zmrow, row, row, row, row, row, row,
                   _full((LRU_W, D_MODEL)), _full((3, SC_W)), _full((SC_W, D_MODEL)), _full((1, SGU_W)), _full((1, SGU_W)),
                   sq, sq, _full((CHUNK, 4)), _full((SGU_W, D_MODEL)), _full((D_MODEL, D_MODEL)), _full((1, D_MODEL))])
    bf = lambda w: jax.ShapeDtypeStruct((s, w), BF16)
    out_specs = [row, row, row, row, row, half, half, half, zmrow, row, half,
                 _full((1, D_MODEL)), _full((1, SGU_W)), _full((1, SGU_W)), sq, sq]
    out_shape = [bf(D_MODEL), bf(D_MODEL), bf(D_MODEL), bf(D_MODEL), bf(LRU_W), bf(SC_W), bf(SGU_W), bf(SGU_W), bf(ZM_W),
                 jax.ShapeDtypeStruct((s, LRU_W), F32), jax.ShapeDtypeStruct((s, SC_W), F32),
                 jax.ShapeDtypeStruct((1, D_MODEL), F32), jax.ShapeDtypeStruct((1, SGU_W), F32),
                 jax.ShapeDtypeStruct((1, SGU_W), F32), jax.ShapeDtypeStruct((4, CHUNK, CHUNK), F32),
                 jax.ShapeDtypeStruct((4, CHUNK, CHUNK), F32)]
    scratch = [pltpu.VMEM((tm, SGU_W), F32), pltpu.VMEM((tm, SGU_W), F32)]
    args = [dy, za, za, za, za, za, za, za, za, za, za, zm, hf, hr, ya, yb, yc, mx, lru_w_out, sc_conv_w, sc_w_out, ln_g, ln_b,
            w_s, w_s_t, b_s_t, sgu_w_out, w_o, post_g]
    assert len(in_specs) == n_in and len(out_specs) == n_out
    if exchange is not None:
        in_specs.append(HBM)
        out_specs.append(HBM)
        out_shape.append(jax.ShapeDtypeStruct(exchange.shape, exchange.dtype))
        scratch += [pltpu.SemaphoreType.DMA((3,)), pltpu.SemaphoreType.DMA((3,))]
        args.append(exchange)
    outs = list(pl.pallas_call(
        body, name="mix_out_bwd" if exchange is None else "mix_out_bwd_exchange", grid=(nt,),
        in_specs=in_specs, out_specs=out_specs, out_shape=out_shape, scratch_shapes=scratch,
        compiler_params=_cp(),
    )(*args))
    if exchange is not None:
        outs[-1] = _own_block(outs[-1], exchange)
    return outs


def _lru_bwd(za, h_dir, dhs, conv_w, conv_b, wa, ba, wx, bx, lam, rev, tm):
    s = za.shape[0]
    nt = s // tm
    back = not rev

    def body(x_ref, xp_ref, xn_ref, h_ref, hp_ref, hn_ref, dh_ref, cw_ref, cb_ref, wa_ref, ba_ref, wx_ref, bx_ref,
             lam_ref, dxc_ref, dwa_ref, dba_ref, dwx_ref, dbx_ref, dlam_ref,
             xc_s, ra_s, xa_s, a_s, m_s, l_s, hsh_s, c8_s, c_s):
        i = pl.program_id(0)
        first = i == 0
        ti = (nt - 1 - i) if back else i
        rows = lax.broadcasted_iota(jnp.int32, (tm, 1), 0)
        xcb = _lru_preact(x_ref, xp_ref, xn_ref, cw_ref, cb_ref, wa_ref, ba_ref, wx_ref, bx_ref, lam_ref, ti, nt, tm,
                          xc_s, ra_s, xa_s, c8_s)
        hprev, hnext = _edges(hp_ref, hn_ref, ti, nt)
        hsh_s[...] = _shift(h_ref[...].astype(F32), 1 if rev else -1, hprev, hnext, rows)

        def gate_blk(j, carry):
            rws = pl.ds(pl.multiple_of(j * LRU_RB, LRU_RB), LRU_RB)
            r = _sigmoid(ra_s[rws, :])
            a, mult = _lru_decay(c8_s[...], r)
            ra_s[rws, :] = r
            xa_s[rws, :] = _sigmoid(xa_s[rws, :])
            a_s[rws, :] = a
            m_s[rws, :] = mult
            return carry

        lax.fori_loop(0, tm // LRU_RB, gate_blk, 0)

        @pl.when(first)
        def _():
            c_s[...] = jnp.zeros_like(c_s)

        nb = tm // 8

        def blk(j, c):
            jb = (nb - 1 - j) if back else j
            base = pl.multiple_of(jb * 8, 8)
            for q in range(8):
                t = base + ((7 - q) if back else q)
                lt = dh_ref[pl.ds(t, 1), :] + c
                l_s[pl.ds(t, 1), :] = lt
                c = a_s[pl.ds(t, 1), :] * lt
            return c

        c_s[...] = lax.fori_loop(0, nb, blk, c_s[...])

        def grad_blk(j, sums):
            s_lam, s_ba, s_bx = sums
            rws = pl.ds(pl.multiple_of(j * LRU_RB, LRU_RB), LRU_RB)
            du = l_s[rws, :]
            a = a_s[rws, :]
            r = ra_s[rws, :]
            ig = xa_s[rws, :]
            mult = m_s[rws, :]
            xc = xc_s[rws, :]
            t1 = du * mult
            dla = du * hsh_s[rws, :] * a - (du * ig * xc) * (a * a) / mult
            dlr = dla * r
            drp = dlr * c8_s[...] * (1.0 - r)
            dip = (t1 * xc) * ig * (1.0 - ig)
            l_s[rws, :] = t1 * ig
            ra_s[rws, :] = drp
            xa_s[rws, :] = dip
            fold = lambda v: sum(v[8 * q:8 * q + 8] for q in range(1, LRU_RB // 8)) + v[0:8]
            return s_lam + fold(dlr), s_ba + fold(drp), s_bx + fold(dip)

        zero8 = jnp.zeros((8, LRU_W), F32)
        s_lam, s_ba, s_bx = lax.fori_loop(0, tm // LRU_RB, grad_blk, (zero8, zero8, zero8))
        _acc(dlam_ref, jnp.sum(s_lam, axis=0, keepdims=True), first)
        _acc(dba_ref, jnp.sum(s_ba, axis=0, keepdims=True), first)
        _acc(dbx_ref, jnp.sum(s_bx, axis=0, keepdims=True), first)
        drb = ra_s[...].astype(BF16)
        dib = xa_s[...].astype(BF16)
        for h in range(4):
            cs_ = slice(LRU_HD * h, LRU_HD * (h + 1))
            dxc_ref[:, cs_] = l_s[:, cs_] + _dot_nt(drb[:, cs_], wa_ref[h]) + _dot_nt(dib[:, cs_], wx_ref[h])
            _acc(dwa_ref.at[h], _dot_tn(xcb[:, cs_], drb[:, cs_]), first)
            _acc(dwx_ref.at[h], _dot_tn(xcb[:, cs_], dib[:, cs_]), first)

        @pl.when(i == nt - 1)
        def _():
            dlam_ref[...] = dlam_ref[...] * (LRU_C * _sigmoid(-lam_ref[...]))

    vec = _full((1, LRU_W))
    hd = _full((4, LRU_HD, LRU_HD))
    tix = (lambda i: nt - 1 - i) if back else (lambda i: i)
    rowspec = pl.BlockSpec((tm, LRU_W), lambda i: (tix(i), 0))
    return pl.pallas_call(
        body, name="lru_bwd_rev" if rev else "lru_bwd", grid=(nt,),
        in_specs=_halo(tm, 16, LRU_W, 1, nt, back) + _halo(tm, 16, LRU_W, 0, nt, back) + [rowspec, _full((4, LRU_W)), vec, hd, vec, hd, vec, vec],
        out_specs=[rowspec, hd, vec, hd, vec, vec],
        out_shape=[jax.ShapeDtypeStruct((s, LRU_W), F32), jax.ShapeDtypeStruct((4, LRU_HD, LRU_HD), F32),
                   jax.ShapeDtypeStruct((1, LRU_W), F32), jax.ShapeDtypeStruct((4, LRU_HD, LRU_HD), F32),
                   jax.ShapeDtypeStruct((1, LRU_W), F32), jax.ShapeDtypeStruct((1, LRU_W), F32)],
        scratch_shapes=[pltpu.VMEM((tm, LRU_W), F32)] * 7 + [pltpu.VMEM((LRU_RB, LRU_W), F32), pltpu.VMEM((1, LRU_W), F32)],
        compiler_params=_cp(),
    )(za, za, za, h_dir, h_dir, h_dir, dhs, conv_w, conv_b, wa, ba, wx, bx, lam)


def _mix_in_bwd(dy, x, za, dxc_f, dxc_r, dcp, dgate, dscb, dsu, dsv, dzm, pre_g, lru_conv_w, sc_conv_w, w_in, tm):
    s = x.shape[0]
    nt = s // tm
    cw = 512

    def body(dy_ref, x_ref, lx_ref, lxp_ref, lxn_ref, scc_ref, sccp_ref, sccn_ref, scx_ref, scxp_ref, scxn_ref,
             df_ref, dfp_ref, dfn_ref, dr_ref, drp_ref, drn_ref, dcp_ref, dcpp_ref, dcpn_ref,
             dgate_ref, dscb_ref, dsu_ref, dsv_ref, dzm_ref, pg_ref, lcw_ref, scw_ref, w_ref,
             dx_ref, dz_ref, dpg_ref, dlcw_ref, dlcb_ref, dscw_ref):
        ti = pl.program_id(0)
        first = ti == 0
        rows = lax.broadcasted_iota(jnp.int32, (tm, 1), 0)
        fp, fn = _edges(dfp_ref, dfn_ref, ti, nt)
        rp, rn = _edges(drp_ref, drn_ref, ti, nt)
        dxc = df_ref[...] + dr_ref[...]
        dprev, dnext = fp + rp, fn + rn
        dlx = lcw_ref[2:3, :] * dxc
        dlx = dlx + lcw_ref[0:1, :] * _shift(dxc, 2, dprev, dnext, rows)
        dlx = dlx + lcw_ref[1:2, :] * _shift(dxc, 1, dprev, dnext, rows)
        dlx = dlx + lcw_ref[3:4, :] * _shift(dxc, -1, dprev, dnext, rows)
        lprev, lnext = _edges(lxp_ref, lxn_ref, ti, nt)
        lx = lx_ref[...].astype(F32)
        _acc(dlcb_ref, jnp.sum(dxc, axis=0, keepdims=True), first)
        for k, d in enumerate((-2, -1, 0, 1)):
            _acc(dlcw_ref.at[pl.ds(k, 1), :], jnp.sum(dxc * _shift(lx, d, lprev, lnext, rows), axis=0, keepdims=True), first)
        cv, xv, p, pm1, pp1, _ = _sc_conv(scc_ref, sccp_ref, sccn_ref, scx_ref, scxp_ref, scxn_ref, scw_ref, ti, nt, rows)
        cprev, cnext = _edges(dcpp_ref, dcpn_ref, ti, nt)
        dcp_v = dcp_ref[...]
        dp = (scw_ref[1:2, :] * dcp_v + scw_ref[0:1, :] * _shift(dcp_v, 1, cprev, cnext, rows)
              + scw_ref[2:3, :] * _shift(dcp_v, -1, cprev, cnext, rows))
        for k, pk in enumerate((pm1, p, pp1)):
            _acc(dscw_ref.at[pl.ds(k, 1), :], jnp.sum(dcp_v * pk, axis=0, keepdims=True), first)
        dz_ref[:, 0:1024] = dgate_ref[...]
        dz_ref[:, 1024:2048] = dlx.astype(BF16)
        dz_ref[:, 2048:2560] = dscb_ref[...]
        dz_ref[:, 2560:3072] = (dp * xv).astype(BF16)
        dz_ref[:, 3072:3584] = (dp * cv).astype(BF16)
        dz_ref[:, 3584:4096] = dsu_ref[...]
        dz_ref[:, 4096:4608] = dsv_ref[...]
        dz_ref[:, 4608:7680] = dzm_ref[...]
        dh = _dot(dz_ref[...], w_ref[...])
        dxn, dpg = _rms_bwd(dh, x_ref[...], pg_ref[...])
        dx_ref[...] = dy_ref[...] + dxn
        _acc(dpg_ref, dpg, first)

    row = pl.BlockSpec((tm, D_MODEL), lambda i: (i, 0))
    half = pl.BlockSpec((tm, 512), lambda i: (i, 0))
    in_specs = ([row, row] + _halo(tm, 16, LRU_W, 1, nt) + _halo(tm, 16, SC_W, 5, nt) + _halo(tm, 16, SC_W, 6, nt)
                + _halo(tm, 8, LRU_W, 0, nt) + _halo(tm, 8, LRU_W, 0, nt) + _halo(tm, 8, SC_W, 0, nt)
                + [row, half, half, half, pl.BlockSpec((tm, ZM_W), lambda i: (i, 0)),
                   _full((1, D_MODEL)), _full((4, LRU_W)), _full((3, SC_W)), _res((D_IN, D_MODEL))])
    return pl.pallas_call(
        body, name="mix_in_bwd", grid=(nt,),
        in_specs=in_specs,
        out_specs=[row, pl.BlockSpec((tm, D_IN), lambda i: (i, 0)), _full((1, D_MODEL)), _full((4, LRU_W)),
                   _full((1, LRU_W)), _full((3, SC_W))],
        out_shape=[jax.ShapeDtypeStruct((s, D_MODEL), F32), jax.ShapeDtypeStruct((s, D_IN), BF16),
                   jax.ShapeDtypeStruct((1, D_MODEL), F32), jax.ShapeDtypeStruct((4, LRU_W), F32),
                   jax.ShapeDtypeStruct((1, LRU_W), F32), jax.ShapeDtypeStruct((3, SC_W), F32)],
        compiler_params=_cp(),
    )(dy, x, za, za, za, za, za, za, za, za, za, dxc_f, dxc_f, dxc_f, dxc_r, dxc_r, dxc_r, dcp, dcp, dcp,
      dgate, dscb, dsu, dsv, dzm, pre_g, lru_conv_w, sc_conv_w, w_in)


def _loss_head(y, target, tm):
    s = y.shape[0]
    nt = s // tm

    def body(y_ref, t_ref, dy_ref, acc_ref):
        err = y_ref[...] - t_ref[...]
        dy_ref[...] = err * (1.0 / D_MODEL)
        _acc(acc_ref, jnp.sum(err * err, axis=0, keepdims=True), pl.program_id(0) == 0)

    row = pl.BlockSpec((tm, D_MODEL), lambda i: (i, 0))
    return pl.pallas_call(
        body, name="loss_head", grid=(nt,), in_specs=[row, row], out_specs=[row, _full((1, D_MODEL))],
        out_shape=[jax.ShapeDtypeStruct((s, D_MODEL), F32), jax.ShapeDtypeStruct((1, D_MODEL), F32)],
        compiler_params=_cp(),
    )(y, target)


def _row_tile(rows, cols):
    cap = max(8, (2 ** 18) // cols)
    best = None
    for t in range(8, min(rows, cap) + 1, 8):
        if rows % t == 0:
            best = t
    return best if best is not None else rows


def _adamw(w, g, m, v, name):
    shape = w.shape
    cols = shape[-1]
    rows = int(np.prod(shape[:-1]))
    tr = _row_tile(rows, cols)
    bc1 = 1.0 - ADAM_B1 ** ADAM_STEP
    bc2 = 1.0 - ADAM_B2 ** ADAM_STEP

    def body(w_ref, g_ref, m_ref, v_ref, d_ref, nm_ref, nv_ref):
        gv = g_ref[...]
        mn = ADAM_B1 * m_ref[...] + (1.0 - ADAM_B1) * gv
        vn = ADAM_B2 * v_ref[...] + (1.0 - ADAM_B2) * (gv * gv)
        nm_ref[...] = mn
        nv_ref[...] = vn
        d_ref[...] = -ADAM_LR * ((mn / bc1) / (jnp.sqrt(vn / bc2) + ADAM_EPS) + ADAM_WD * w_ref[...])

    spec = pl.BlockSpec((tr, cols), lambda i: (i, 0))
    sds = jax.ShapeDtypeStruct((rows, cols), F32)
    outs = pl.pallas_call(
        body, name="adamw_" + name, grid=(rows // tr,), in_specs=[spec] * 4, out_specs=[spec] * 3,
        out_shape=[sds, sds, sds], compiler_params=_cp(("parallel",)),
    )(w.reshape(rows, cols), g.reshape(rows, cols), m.reshape(rows, cols), v.reshape(rows, cols))
    return [o.reshape(shape) for o in outs]


HBM = pl.BlockSpec(memory_space=pl.ANY)


def _place():
    x, y, c = lax.axis_index("x"), lax.axis_index("y"), lax.axis_index("c")
    chips = [(1 - x, y), (x, 1 - y), (1 - x, 1 - y)]
    return x, y, c, chips


def _all_gather(buf, name):
    r, cdim = buf.shape

    def body(b_ref, o_ref, ssem, rsem):
        _gather_steps(0, 1, b_ref, o_ref, ssem, rsem)

    got = pl.pallas_call(
        body, name=name, in_specs=[HBM], out_specs=HBM,
        out_shape=jax.ShapeDtypeStruct((N_SHARD, r, cdim), buf.dtype),
        scratch_shapes=[pltpu.SemaphoreType.DMA((6,)), pltpu.SemaphoreType.DMA((6,))],
    )(buf)
    return _own_slot(got, buf)


def _own_slot(got, buf):
    me = 2 * lax.axis_index("x") + lax.axis_index("y")
    return lax.dynamic_update_slice(got, buf[None], (me, 0, 0))


def _at_step(i, step):
    if isinstance(i, int):
        return (lambda f: f()) if i == step else (lambda f: None)
    return pl.when(i == step)


def _gather_steps(i, nt, b_ref, o_ref, ssem, rsem):
    r2 = b_ref.shape[0] // 2
    x, y, c, chips = _place()
    me = 2 * x + y
    sib = (x, y, 1 - c)
    mine = pl.ds(pl.multiple_of(c * r2, 16), r2)
    other = pl.ds(pl.multiple_of((1 - c) * r2, 16), r2)

    def rc(k, src, dst, to):
        return pltpu.make_async_remote_copy(src_ref=src, dst_ref=dst, send_sem=ssem.at[k], recv_sem=rsem.at[k],
                                            device_id=to, device_id_type=MESH)

    def sends():
        return [rc(j, b_ref.at[mine], o_ref.at[me, mine], (cx, cy, c)) for j, (cx, cy) in enumerate(chips)]

    def landed(j, rows):
        cx, cy = chips[j]
        return o_ref.at[2 * cx + cy, rows]

    @_at_step(i, 0)
    def _():
        for cp in sends():
            cp.start()

    @_at_step(i, max(nt - 4, 0))
    def _():
        for j, (cx, cy) in enumerate(chips):
            rc(j, landed(j, mine), landed(j, mine), (cx, cy, c)).wait_recv()
            rc(3 + j, landed(j, mine), landed(j, mine), sib).start()

    @_at_step(i, nt - 1)
    def _():
        for j in range(3):
            rc(3 + j, landed(j, other), landed(j, other), sib).wait_recv()
        for cp in sends():
            cp.wait_send()
        for j in range(3):
            rc(3 + j, landed(j, mine), landed(j, mine), sib).wait_send()


def _swap_steps(i, nt, s_ref, r_ref, ssem, rsem):
    r2 = r_ref.shape[1]
    x, y, c, _ = _place()

    def cp():
        return pltpu.make_async_remote_copy(src_ref=s_ref.at[:, pl.ds(pl.multiple_of((1 - c) * r2, 16), r2), :], dst_ref=r_ref,
                                            send_sem=ssem, recv_sem=rsem, device_id=(x, y, 1 - c), device_id_type=MESH)

    @_at_step(i, 0)
    def _():
        cp().start()

    @_at_step(i, nt - 1)
    def _():
        cp().wait()


def _exchange_steps(i, nt, b_ref, o_ref, ssem, rsem):
    x, y, c, chips = _place()
    me = 2 * x + y

    def sends():
        return [pltpu.make_async_remote_copy(src_ref=b_ref.at[2 * cx + cy], dst_ref=o_ref.at[me], send_sem=ssem.at[j],
                                             recv_sem=rsem.at[j], device_id=(cx, cy, c), device_id_type=MESH)
                for j, (cx, cy) in enumerate(chips)]

    @_at_step(i, 0)
    def _():
        for cp in sends():
            cp.start()

    @_at_step(i, nt - 1)
    def _():
        for j, (cx, cy) in enumerate(chips):
            blk = o_ref.at[2 * cx + cy]
            pltpu.make_async_remote_copy(src_ref=blk, dst_ref=blk, send_sem=ssem.at[j], recv_sem=rsem.at[j],
                                         device_id=(cx, cy, c), device_id_type=MESH).wait_recv()
        for cp in sends():
            cp.wait_send()


def _pair_swap(send, name, sibling_half=False):
    shape = send.shape
    if sibling_half:
        r2 = shape[1] // 2
        shape = (shape[0], r2, shape[2])

    def body(s_ref, r_ref, ssem, rsem):
        x, y, c, _ = _place()
        src = s_ref.at[:, pl.ds(pl.multiple_of((1 - c) * r2, 16), r2), :] if sibling_half else s_ref
        cp = pltpu.make_async_remote_copy(src_ref=src, dst_ref=r_ref, send_sem=ssem, recv_sem=rsem,
                                          device_id=(x, y, 1 - c), device_id_type=MESH)
        cp.start()
        cp.wait()

    return pl.pallas_call(
        body, name=name, in_specs=[HBM], out_specs=HBM, out_shape=jax.ShapeDtypeStruct(shape, send.dtype),
        scratch_shapes=[pltpu.SemaphoreType.DMA, pltpu.SemaphoreType.DMA],
    )(send)


def _chip_exchange(blocks, name):
    def body(b_ref, o_ref, ssem, rsem):
        _exchange_steps(0, 1, b_ref, o_ref, ssem, rsem)

    got = pl.pallas_call(
        body, name=name, in_specs=[HBM], out_specs=HBM, out_shape=jax.ShapeDtypeStruct(blocks.shape, blocks.dtype),
        scratch_shapes=[pltpu.SemaphoreType.DMA((3,)), pltpu.SemaphoreType.DMA((3,))],
    )(blocks)
    return _own_block(got, blocks)


def _own_block(got, blocks):
    me = 2 * lax.axis_index("x") + lax.axis_index("y")
    own = lax.dynamic_slice_in_dim(blocks, me, 1, axis=0)
    return lax.dynamic_update_slice_in_dim(got, own, me, axis=0)


def _pair_gather(half, name):
    c = lax.axis_index("c")
    got = _pair_swap(half, name)
    return jnp.stack([jnp.where(c == 0, half, got), jnp.where(c == 0, got, half)])


def _pair_sum(keep, got):
    n, r2, cdim = keep.shape

    def body(a_ref, b_ref, o_ref):
        o_ref[...] = (a_ref[...].astype(F32) + b_ref[...].astype(F32)).astype(BF16)

    spec = pl.BlockSpec((1, RS_TILE, cdim), lambda k, i: (k, i, 0))
    return pl.pallas_call(
        body, name="rs_pair_sum", grid=(n, r2 // RS_TILE), in_specs=[spec, spec], out_specs=spec,
        out_shape=jax.ShapeDtypeStruct(keep.shape, BF16), compiler_params=_cp(("parallel", "parallel")),
    )(keep, got)


def _chip_sum(slots):
    n, r2, cdim = slots.shape

    def body(s_ref, o_ref):
        o_ref[...] = ((s_ref[0].astype(F32) + s_ref[1].astype(F32)) + s_ref[2].astype(F32)) + s_ref[3].astype(F32)

    return pl.pallas_call(
        body, name="rs_chip_sum", grid=(r2 // RS_TILE,),
        in_specs=[pl.BlockSpec((n, RS_TILE, cdim), lambda i: (0, i, 0))],
        out_specs=pl.BlockSpec((RS_TILE, cdim), lambda i: (i, 0)),
        out_shape=jax.ShapeDtypeStruct((r2, cdim), F32), compiler_params=_cp(("parallel",)),
    )(slots)


def _rs_pair_stage(packed, tag):
    r2 = packed.shape[1] // 2
    c = lax.axis_index("c")
    keep = lax.dynamic_slice_in_dim(packed, c * r2, r2, axis=1)
    return _pair_sum(keep, _pair_swap(packed, "rs_pair_swap" + tag, sibling_half=True))


def _rs_finish(slots, tag):
    both = _pair_gather(_chip_sum(slots), "rs_pair_gather" + tag)
    return both.reshape(2 * slots.shape[1], slots.shape[2])


TRANSPOSED = ("ffn1_w_gate", "ffn1_w_up", "ffn2_w_gate", "ffn2_w_up", "w_in")
BIG = [("ffn1_w_gate", (F_SH, D_MODEL)), ("ffn1_w_up", (F_SH, D_MODEL)), ("ffn1_w_down", (F_SH, D_MODEL)),
       ("ffn2_w_gate", (F_SH, D_MODEL)), ("ffn2_w_up", (F_SH, D_MODEL)), ("ffn2_w_down", (F_SH, D_MODEL)),
       ("w_in", (D_IN_SH, D_MODEL)), ("lru_wa", (2, 4, 64, LRU_HD)), ("lru_wx", (2, 4, 64, LRU_HD)),
       ("lru_w_out", (256, D_MODEL)), ("sc_w_out", (SC_W, 256)), ("sgu_w_out", (SGU_W, 256)), ("w_o", (256, D_MODEL))]
FFN_NAMES = [n for n, _ in BIG[:6]]
TAIL_BIG = BIG[7:]
FFN_BLK0 = 3
TAIL_ROW0 = (FFN_BLK0 + 6) * F_SH
TAIL_ROWS = 1152
SMALL = [("lru_conv_w", (4, 256)), ("lru_ba", (2, 256)), ("lru_bx", (2, 256)), ("lru_lambda", (2, 256)),
         ("sc_conv_w", (3, 128))]
REPL = [("ffn1_pre_g", (D_MODEL,)), ("ffn1_post_g", (D_MODEL,)), ("mix_pre_g", (D_MODEL,)), ("lru_conv_b", (LRU_W,)),
        ("sgu_ln_g", (SGU_W,)), ("sgu_ln_b", (SGU_W,)), ("sgu_w_s", (4, CHUNK, CHUNK)), ("sgu_b", (4, CHUNK)),
        ("mix_post_g", (D_MODEL,)), ("ffn2_pre_g", (D_MODEL,)), ("ffn2_post_g", (D_MODEL,))]
WEIGHTS = ['ffn1_pre_g', 'ffn1_w_gate', 'ffn1_w_up', 'ffn1_w_down', 'ffn1_post_g', 'mix_pre_g', 'w_in', 'lru_conv_w',
           'lru_conv_b', 'lru_wa', 'lru_ba', 'lru_wx', 'lru_bx', 'lru_lambda', 'lru_w_out', 'sc_conv_w', 'sc_w_out',
           'sgu_ln_g', 'sgu_ln_b', 'sgu_w_s', 'sgu_b', 'sgu_w_out', 'w_o', 'mix_post_g', 'ffn2_pre_g', 'ffn2_w_gate',
           'ffn2_w_up', 'ffn2_w_down', 'ffn2_post_g']


def _seg_rows(shape):
    return -(-int(np.prod(shape)) // PACK_C)


def _pack(parts, rows_total, dtype):
    lead = parts[0].shape[0]
    segs = []
    used = 0
    for p in parts:
        flat = p.reshape(lead, -1).astype(dtype)
        nr = -(-flat.shape[1] // PACK_C)
        pad = nr * PACK_C - flat.shape[1]
        if pad:
            flat = jnp.pad(flat, ((0, 0), (0, pad)))
        segs.append(flat.reshape(lead, nr, PACK_C))
        used += nr
    if rows_total > used:
        segs.append(jnp.zeros((lead, rows_total - used, PACK_C), dtype))
    return jnp.concatenate(segs, axis=1)


def _unpack(buf, specs):
    lead = buf.shape[0]
    out = {}
    r0 = 0
    for name, shape in specs:
        nr = _seg_rows(shape)
        n = int(np.prod(shape))
        out[name] = buf[:, r0:r0 + nr].reshape(lead, nr * PACK_C)[:, :n].reshape((lead,) + tuple(shape))
        r0 += nr
    return out


def _full_from_shards(name, t):
    if name in ("ffn1_w_gate", "ffn1_w_up", "ffn1_w_down", "ffn2_w_gate", "ffn2_w_up", "ffn2_w_down"):
        return t
    if name == "w_in":
        return t.reshape(D_IN, D_MODEL)
    if name in ("sc_w_out", "sgu_w_out", "lru_conv_w", "lru_ba", "lru_bx", "lru_lambda", "sc_conv_w"):
        return jnp.moveaxis(t, 0, -2).reshape(t.shape[1:-1] + (N_SHARD * t.shape[-1],))
    if name in ("lru_wa", "lru_wx"):
        return jnp.moveaxis(t, 0, 2).reshape(2, 4, LRU_HD, LRU_HD)
    if name in ("lru_w_out", "w_o"):
        return t.reshape(N_SHARD * t.shape[1], t.shape[2])
    raise ValueError(name)


def _shards_from_full(name, gfull):
    if name in ("ffn1_w_gate", "ffn1_w_up", "ffn1_w_down", "ffn2_w_gate", "ffn2_w_up", "ffn2_w_down"):
        return gfull
    if name == "w_in":
        return gfull.reshape(N_SHARD, D_IN_SH, D_MODEL)
    if name in ("sc_w_out", "sgu_w_out", "lru_conv_w", "lru_ba", "lru_bx", "lru_lambda", "sc_conv_w"):
        lastdim = gfull.shape[-1] // N_SHARD
        return jnp.moveaxis(gfull.reshape(gfull.shape[:-1] + (N_SHARD, lastdim)), -2, 0)
    if name in ("lru_wa", "lru_wx"):
        return jnp.moveaxis(gfull.reshape(2, 4, N_SHARD, 64, LRU_HD), 2, 0)
    if name in ("lru_w_out", "w_o"):
        return gfull.reshape(N_SHARD, gfull.shape[0] // N_SHARD, gfull.shape[1])
    raise ValueError(name)


def _tiles(s):
    return dict(ffn=min(512, s), ffn_bwd=min(512, s), tn=min(2048, s), tn_ffn=min(4096, s), mix_in=min(512, s), lru=min(512, s), mix=min(256, s), loss=min(512, s))


def _mixer_fwd(x, w, t):
    hb, za, zm = _mix_in_fwd(x, w["mix_pre_g"], w["w_in"], t["mix_in"])
    hf = _lru_fwd(za, w["lru_conv_w"], w["lru_conv_b"], w["lru_wa"][0], w["lru_ba"][0:1], w["lru_wx"][0],
                  w["lru_bx"][0:1], w["lru_lambda"][0:1], False, t["lru"])
    hr = _lru_fwd(za, w["lru_conv_w"], w["lru_conv_b"], w["lru_wa"][1], w["lru_ba"][1:2], w["lru_wx"][1],
                  w["lru_bx"][1:2], w["lru_lambda"][1:2], True, t["lru"])
    out, yain, q, ycin, ya, yb, yc, mb, mx = _mix_out_fwd(
        x, za, zm, hf, hr, w["lru_w_out"], w["sc_conv_w"], w["sc_w_out"], w["sgu_ln_g"], w["sgu_ln_b"], w["sgu_w_s"],
        w["sgu_b_t"], w["sgu_w_out"], w["w_o"], w["mix_post_g"], t["mix"])
    return out, dict(x=x, hb=hb, za=za, zm=zm, hf=hf, hr=hr, yain=yain, q=q, ycin=ycin, ya=ya, yb=yb, yc=yc, mb=mb, mx=mx)


def _mixer_bwd(dy, sv, w, t, buf, exchange=None):
    g = {}
    (dmx, dya, dyb, dyc, dgate, dscb, dsu, dsv, dzm, dhs, dcp, g["mix_post_g"], g["sgu_ln_g"], g["sgu_ln_b"],
     g["sgu_w_s"], dbs, *slots) = _mix_out_bwd(
        dy, sv["za"], sv["zm"], sv["hf"], sv["hr"], sv["ya"], sv["yb"], sv["yc"], sv["mx"], w["lru_w_out"], w["sc_conv_w"],
        w["sc_w_out"], w["sgu_ln_g"], w["sgu_ln_b"], w["sgu_w_s"], w["sgu_w_s_t"], w["sgu_b_t"], w["sgu_w_out"], w["w_o"],
        w["mix_post_g"], t["mix"], exchange)
    g["sgu_b"] = jnp.sum(dbs, axis=-1)
    ts = t["tn"]
    g["w_o"] = _tn_plain(sv["mb"], dmx, "dw_o", ts)
    g["lru_w_out"] = _tn_plain(sv["yain"], dya, "dw_lru_out", ts)
    g["sc_w_out"] = _tn_plain(sv["q"], dyb, "dw_sc_out", ts)
    g["sgu_w_out"] = _tn_plain(sv["ycin"], dyc, "dw_sgu_out", ts)
    dxc, dwa, dba, dwx, dbx, dlam = [], [], [], [], [], []
    for d, rev in enumerate((False, True)):
        o = _lru_bwd(sv["za"], sv["hr"] if rev else sv["hf"], dhs, w["lru_conv_w"], w["lru_conv_b"], w["lru_wa"][d],
                     w["lru_ba"][d:d + 1], w["lru_wx"][d], w["lru_bx"][d:d + 1], w["lru_lambda"][d:d + 1], rev, t["lru"])
        for lst, val in zip((dxc, dwa, dba, dwx, dbx, dlam), o):
            lst.append(val)
    g["lru_wa"] = jnp.stack(dwa)
    g["lru_wx"] = jnp.stack(dwx)
    g["lru_ba"] = jnp.concatenate(dba, axis=0)
    g["lru_bx"] = jnp.concatenate(dbx, axis=0)
    g["lru_lambda"] = jnp.concatenate(dlam, axis=0)
    dx, dz, g["mix_pre_g"], g["lru_conv_w"], g["lru_conv_b"], g["sc_conv_w"] = _mix_in_bwd(
        dy, sv["x"], sv["za"], dxc[0], dxc[1], dcp, dgate, dscb, dsu, dsv, dzm, w["mix_pre_g"], w["lru_conv_w"],
        w["sc_conv_w"], w["w_in"], t["mix"])
    buf = _tn(dz, sv["hb"], "dw_in", N_SHARD, (ts, D_IN_SH), lambda b, s_: (s_, b), (ts, D_MODEL), lambda b, s_: (s_, 0),
              None, (None, D_IN_SH, D_MODEL), None, ts, into=(buf, 0))
    return dx, g, buf, (slots[0] if slots else None)


def kernel(x, ffn1_pre_g, ffn1_w_gate, ffn1_w_up, ffn1_w_down, ffn1_post_g, mix_pre_g, w_in, lru_conv_w, lru_conv_b, lru_wa, lru_ba, lru_wx, lru_bx, lru_lambda, lru_w_out, sc_conv_w, sc_w_out, sgu_ln_g, sgu_ln_b, sgu_w_s, sgu_b, sgu_w_out, w_o, mix_post_g, ffn2_pre_g, ffn2_w_gate, ffn2_w_up, ffn2_w_down, ffn2_post_g, loss_target, m_ffn1_pre_g, m_ffn1_w_gate, m_ffn1_w_up, m_ffn1_w_down, m_ffn1_post_g, m_mix_pre_g, m_w_in, m_lru_conv_w, m_lru_conv_b, m_lru_wa, m_lru_ba, m_lru_wx, m_lru_bx, m_lru_lambda, m_lru_w_out, m_sc_conv_w, m_sc_w_out, m_sgu_ln_g, m_sgu_ln_b, m_sgu_w_s, m_sgu_b, m_sgu_w_out, m_w_o, m_mix_post_g, m_ffn2_pre_g, m_ffn2_w_gate, m_ffn2_w_up, m_ffn2_w_down, m_ffn2_post_g, v_ffn1_pre_g, v_ffn1_w_gate, v_ffn1_w_up, v_ffn1_w_down, v_ffn1_post_g, v_mix_pre_g, v_w_in, v_lru_conv_w, v_lru_conv_b, v_lru_wa, v_lru_ba, v_lru_wx, v_lru_bx, v_lru_lambda, v_lru_w_out, v_sc_conv_w, v_sc_w_out, v_sgu_ln_g, v_sgu_ln_b, v_sgu_w_s, v_sgu_b, v_sgu_w_out, v_w_o, v_mix_post_g, v_ffn2_pre_g, v_ffn2_w_gate, v_ffn2_w_up, v_ffn2_w_down, v_ffn2_post_g):
    args = (ffn1_pre_g, ffn1_w_gate, ffn1_w_up, ffn1_w_down, ffn1_post_g, mix_pre_g, w_in, lru_conv_w, lru_conv_b, lru_wa, lru_ba, lru_wx, lru_bx, lru_lambda, lru_w_out, sc_conv_w, sc_w_out, sgu_ln_g, sgu_ln_b, sgu_w_s, sgu_b, sgu_w_out, w_o, mix_post_g, ffn2_pre_g, ffn2_w_gate, ffn2_w_up, ffn2_w_down, ffn2_post_g)
    margs = (m_ffn1_pre_g, m_ffn1_w_gate, m_ffn1_w_up, m_ffn1_w_down, m_ffn1_post_g, m_mix_pre_g, m_w_in, m_lru_conv_w, m_lru_conv_b, m_lru_wa, m_lru_ba, m_lru_wx, m_lru_bx, m_lru_lambda, m_lru_w_out, m_sc_conv_w, m_sc_w_out, m_sgu_ln_g, m_sgu_ln_b, m_sgu_w_s, m_sgu_b, m_sgu_w_out, m_w_o, m_mix_post_g, m_ffn2_pre_g, m_ffn2_w_gate, m_ffn2_w_up, m_ffn2_w_down, m_ffn2_post_g)
    vargs = (v_ffn1_pre_g, v_ffn1_w_gate, v_ffn1_w_up, v_ffn1_w_down, v_ffn1_post_g, v_mix_pre_g, v_w_in, v_lru_conv_w, v_lru_conv_b, v_lru_wa, v_lru_ba, v_lru_wx, v_lru_bx, v_lru_lambda, v_lru_w_out, v_sc_conv_w, v_sc_w_out, v_sgu_ln_g, v_sgu_ln_b, v_sgu_w_s, v_sgu_b, v_sgu_w_out, v_w_o, v_mix_post_g, v_ffn2_pre_g, v_ffn2_w_gate, v_ffn2_w_up, v_ffn2_w_down, v_ffn2_post_g)
    wsh = dict(zip(WEIGHTS, args))
    msh = dict(zip(WEIGHTS, margs))
    vsh = dict(zip(WEIGHTS, vargs))
    xs = x[0]
    s = xs.shape[0]
    t = _tiles(s)

    small = _pack([wsh[n].reshape(1, -1) for n, _ in SMALL], SMALL_ROWS, F32)[0]
    small_all = _all_gather(small, "ag_small")
    small_un = _unpack(small_all, [(n, (DEPTH,) + sh) for n, sh in SMALL])
    packed_w = [_pack([(wsh[n][l].T if n in TRANSPOSED else wsh[n][l])[None] for n, _ in BIG], AG_ROWS, BF16)[0]
                for l in range(DEPTH)]

    def layer_weights(l, gathered):
        un = _unpack(gathered, BIG)
        w = {n: _full_from_shards(n, un[n]) for n, _ in BIG}
        for n, _ in SMALL:
            w[n] = _full_from_shards(n, small_un[n][:, l])
        for n, _ in REPL:
            w[n] = wsh[n][l]
        for n in ("ffn1_pre_g", "ffn1_post_g", "mix_pre_g", "lru_conv_b", "sgu_ln_g", "sgu_ln_b", "mix_post_g", "ffn2_pre_g",
                  "ffn2_post_g"):
            w[n] = w[n].reshape(1, -1)
        w["sgu_w_s_t"] = jnp.swapaxes(w["sgu_w_s"], 1, 2).astype(BF16)
        w["sgu_w_s"] = w["sgu_w_s"].astype(BF16)
        w["sgu_b_t"] = w["sgu_b"].T
        return w

    saved = []
    layers = []
    cur = xs
    gathered = _all_gather(packed_w[0], "ag_weights_l0")
    for l in range(DEPTH):
        w = layer_weights(l, gathered)
        layers.append(w)
        sv = {}
        x1, *rest = _ffn_fwd(cur, w["ffn1_pre_g"], w["ffn1_w_gate"], w["ffn1_w_up"], w["ffn1_w_down"], w["ffn1_post_g"], t["ffn"],
                             gather=packed_w[l + 1] if l + 1 < DEPTH else None)
        if l + 1 < DEPTH:
            gathered = rest.pop()
        sv["ffn1"] = (cur, *rest)
        x2, sv["mix"] = _mixer_fwd(x1, w, t)
        x3, *rest = _ffn_fwd(x2, w["ffn2_pre_g"], w["ffn2_w_gate"], w["ffn2_w_up"], w["ffn2_w_down"], w["ffn2_post_g"], t["ffn"])
        sv["ffn2"] = (x2, *rest)
        saved.append(sv)
        cur = x3

    dy, sq = _loss_head(cur, loss_target[0], t["loss"])
    loss = lax.psum(0.5 * jnp.sum(sq) / D_MODEL, ("x", "y", "c"))

    grads = [None] * DEPTH

    def finish(layer, slots):
        total = _rs_finish(slots, "_l%d" % layer)
        un = _unpack(total[None, TAIL_ROW0:TAIL_ROW0 + TAIL_ROWS], TAIL_BIG + SMALL + REPL)
        un["w_in"] = total[None, 0:D_IN_SH]
        for j, n in enumerate(FFN_NAMES):
            un[n] = total[None, (FFN_BLK0 + j) * F_SH:(FFN_BLK0 + j + 1) * F_SH]
        grads[layer] = {n: (un[n][0].T if n in TRANSPOSED else un[n][0]) for n in WEIGHTS}

    pending = None
    above = None
    r2 = RS_ROWS // 2
    for l in reversed(range(DEPTH)):
        w = layers[l]
        sv = saved[l]
        g = {}
        buf = jnp.zeros((N_SHARD, RS_ROWS, PACK_C), BF16)
        for tag in ("ffn2", "mix", "ffn1"):
            if tag == "mix":
                dy, gm, buf, slots = _mixer_bwd(dy, sv["mix"], w, t, buf, pending)
                if pending is not None:
                    finish(l + 1, slots)
                g.update(gm)
                continue
            xin, ab, d1, d2, ff = sv[tag]
            dy, hb, dgb, dub, dfb, g[tag + "_pre_g"], g[tag + "_post_g"], *got = _ffn_bwd(
                dy, xin, d1, d2, ff, w[tag + "_pre_g"], w[tag + "_post_g"], w[tag + "_w_gate"], w[tag + "_w_up"],
                w[tag + "_w_down"], t["ffn_bwd"], swap=above if tag == "ffn2" else None)
            if got:
                keep = lax.dynamic_slice_in_dim(above, lax.axis_index("c") * r2, r2, axis=1)
                pending = _pair_sum(keep, got[0])
            buf = _ffn_wgrads(hb, ab, dgb, dub, dfb, t["tn_ffn"], buf, FFN_BLK0 + (3 if tag == "ffn2" else 0))
        parts = [_shards_from_full(n, g[n]) for n, _ in TAIL_BIG] + [_shards_from_full(n, g[n]) for n, _ in SMALL]
        parts += [jnp.broadcast_to(g[n].reshape((1,) + sh), (N_SHARD,) + sh) for n, sh in REPL]
        buf = lax.dynamic_update_slice(buf, _pack(parts, TAIL_ROWS, BF16), (0, TAIL_ROW0, 0))
        above = buf
    finish(0, _chip_exchange(_rs_pair_stage(above, "_l0"), "rs_chip_exchange_l0"))
    grad_x = dy[None]

    gw, dw, nm, nv = [], [], [], []
    for n in WEIGHTS:
        gfull = jnp.stack([grads[l][n] for l in range(DEPTH)])
        d_, m_, v_ = _adamw(wsh[n], gfull, msh[n], vsh[n], n)
        gw.append(gfull)
        dw.append(d_)
        nm.append(m_)
        nv.append(v_)
    return (loss, grad_x, *gw, *dw, *nm, *nv)
```
